```python
import jax, jax.numpy as jnp
from jax import lax
import numpy as np

D_MODEL = 2048
BATCH = 8
SEQ = 2048
DEPTH = 4

NORM_EPS = 1e-5
ROPE_THETA = 500000.0
ROPE_FRACTION = 4
BAND_BLOCK = 128

A_DIM = 128
A_HEADS = D_MODEL // 2 // A_DIM
A_BRANCHES = ((128, 1), (512, 4), (2048, 16))
A_WIDTH = A_HEADS * A_DIM
B_KDIM = 128
B_VDIM = 128
B_HEADS = D_MODEL // 2 // B_VDIM
B_KWIDTH = B_HEADS * B_KDIM
B_WIDTH = B_HEADS * B_VDIM
B_CHUNK = 64
EVEN_SPLITS = (A_WIDTH, A_WIDTH, A_WIDTH, B_KWIDTH, B_KWIDTH, B_WIDTH, B_WIDTH)
EVEN_IN = sum(EVEN_SPLITS)
EVEN_MIX = A_WIDTH + B_WIDTH

C_DIM = 64
C_Q_HEADS = D_MODEL // C_DIM
C_KV_HEADS = C_Q_HEADS // 8
C_GROUP = C_Q_HEADS // C_KV_HEADS
C_WINDOW = 128
C_QKV = (C_Q_HEADS + 2 * C_KV_HEADS) * C_DIM

D_FF = 4 * D_MODEL
N_EVEN = (DEPTH + 1) // 2
N_ODD = DEPTH // 2

kernel_name = "hybrid_dilated_hgrn2_swa_sink_trunk"


def rmsnorm(x, g):
    xf = x.astype(jnp.float32)
    y = xf * lax.rsqrt(jnp.mean(jnp.square(xf), axis=-1, keepdims=True) + NORM_EPS)
    return (y * g.astype(jnp.float32)).astype(x.dtype)


def rope_tables(seq, head_dim):
    rot = head_dim // ROPE_FRACTION
    inv_freq = 1.0 / (ROPE_THETA ** (jnp.arange(0, rot, 2, dtype=jnp.float32) / rot))
    ang = jnp.arange(seq, dtype=jnp.float32)[:, None] * inv_freq[None, :]
    return jnp.cos(ang), jnp.sin(ang)


def apply_partial_rope(x, cos, sin):
    half = cos.shape[-1]
    xf = x.astype(jnp.float32)
    x1, x2, rest = xf[..., :half], xf[..., half:2 * half], xf[..., 2 * half:]
    out = jnp.concatenate([x1 * cos - x2 * sin, x2 * cos + x1 * sin, rest], axis=-1)
    return out.astype(x.dtype)


def banded_attention(q, k, v, max_dist, sink=None):
    L, D = q.shape[-2], q.shape[-1]
    nb = -(-L // BAND_BLOCK)
    pad_r = nb * BAND_BLOCK - L

    def pad_seq(t, left):
        return jnp.pad(t, [(0, 0)] * (t.ndim - 2) + [(left, pad_r), (0, 0)])

    qb = pad_seq(q, 0).reshape(q.shape[:-2] + (nb, BAND_BLOCK, D))
    kp = pad_seq(k, BAND_BLOCK).reshape(k.shape[:-2] + (nb + 1, BAND_BLOCK, D))
    vp = pad_seq(v, BAND_BLOCK).reshape(v.shape[:-2] + (nb + 1, BAND_BLOCK, D))
    kb = jnp.concatenate([kp[..., :-1, :, :], kp[..., 1:, :, :]], axis=-2)
    vb = jnp.concatenate([vp[..., :-1, :, :], vp[..., 1:, :, :]], axis=-2)
    s = jnp.einsum('...gnid,...njd->...gnij', qb, kb,
                   preferred_element_type=jnp.float32) * (D ** -0.5)
    qi = jnp.arange(BAND_BLOCK)[:, None]
    kj = jnp.arange(2 * BAND_BLOCK)[None, :]
    dist = qi + BAND_BLOCK - kj
    kpos = jnp.arange(nb)[:, None, None] * BAND_BLOCK - BAND_BLOCK + kj
    mask = (dist >= 0) & (dist <= max_dist) & (kpos >= 0)
    s = jnp.where(mask, s, -jnp.inf)
    m = jnp.max(s, axis=-1)
    if sink is not None:
        sink = sink.astype(jnp.float32)[..., None, None]
        m = jnp.maximum(m, sink)
    p = jnp.exp(s - m[..., None])
    l = jnp.sum(p, axis=-1)
    if sink is not None:
        l = l + jnp.exp(sink - m)
    num = jnp.einsum('...gnij,...njd->...gnid', p.astype(v.dtype), vb,
                     preferred_element_type=jnp.float32)
    num = num.reshape(num.shape[:-3] + (nb * BAND_BLOCK, D))[..., :L, :]
    m = m.reshape(m.shape[:-2] + (nb * BAND_BLOCK,))[..., :L]
    l = l.reshape(l.shape[:-2] + (nb * BAND_BLOCK,))[..., :L]
    return num, m, l


def dilated_attention(q, k, v):
    B, H, S, Dh = q.shape
    nums, ms, ls = [], [], []
    for window, dil in A_BRANCHES:
        L = S // dil

        def to_res(t):
            return jnp.swapaxes(t.reshape(B, H, L, dil, Dh), 2, 3)

        num, m, l = banded_attention(to_res(q)[..., None, :, :], to_res(k), to_res(v),
                                     window // dil)
        nums.append(jnp.swapaxes(num[..., 0, :, :], 2, 3).reshape(B, H, S, Dh))
        ms.append(jnp.swapaxes(m[..., 0, :], 2, 3).reshape(B, H, S))
        ls.append(jnp.swapaxes(l[..., 0, :], 2, 3).reshape(B, H, S))
    m_all = jnp.stack(ms)
    w = jnp.exp(m_all - jnp.max(m_all, axis=0, keepdims=True))
    num = jnp.sum(w[..., None] * jnp.stack(nums), axis=0)
    den = jnp.sum(w * jnp.stack(ls), axis=0)
    return num / den[..., None]


def hgrn2_chunkwise(q, k, v, log_f):
    B, H, S, K = q.shape
    V = v.shape[-1]
    n = S // B_CHUNK

    def chunks(t):
        return jnp.moveaxis(t.reshape(B, H, n, B_CHUNK, t.shape[-1]), 2, 0)

    causal = jnp.tril(jnp.ones((B_CHUNK, B_CHUNK), bool))

    def step(state, xs):
        qc, kc, vc, gc = xs
        b = jnp.cumsum(gc, axis=-2)
        diff = b[..., :, None, :] - b[..., None, :, :]
        decay = jnp.exp(jnp.where(causal[..., None], diff, -jnp.inf))
        attn = jnp.einsum('bhtk,bhsk,bhtsk->bhts', qc, kc, decay)
        o = (jnp.einsum('bhts,bhsv->bhtv', attn, vc)
             + jnp.einsum('bhtk,bhkv->bhtv', qc * jnp.exp(b), state))
        b_last = b[..., -1:, :]
        new_state = (jnp.exp(b_last[..., 0, :])[..., None] * state
                     + jnp.einsum('bhsk,bhsv->bhkv', kc * jnp.exp(b_last - b), vc))
        return new_state, o

    state0 = jnp.zeros((B, H, K, V), jnp.float32)
    _, o = lax.scan(step, state0, (chunks(q), chunks(k), chunks(v), chunks(log_f)))
    return jnp.moveaxis(o, 0, 2).reshape(B, H, S, V)


def split_heads(t, n_heads):
    B, S, _ = t.shape
    return t.reshape(B, S, n_heads, -1).transpose(0, 2, 1, 3)


def merge_heads(t):
    B, H, S, Dh = t.shape
    return t.transpose(0, 2, 1, 3).reshape(B, S, H * Dh)


def even_mixer(h, w_in, w_out, lower_bound, out_norm_g, cos_a, sin_a):
    proj = h @ w_in
    qa, ka, va, qb, fb, ib, gb = jnp.split(proj, np.cumsum(EVEN_SPLITS)[:-1].tolist(), axis=-1)
    qa = apply_partial_rope(split_heads(qa, A_HEADS), cos_a, sin_a)
    ka = apply_partial_rope(split_heads(ka, A_HEADS), cos_a, sin_a)
    oa = dilated_attention(qa, ka, split_heads(va, A_HEADS))
    lb = lower_bound.astype(jnp.float32).reshape(B_HEADS, 1, B_KDIM)
    gate = lb + (1.0 - lb) * jax.nn.sigmoid(split_heads(fb, B_HEADS).astype(jnp.float32))
    q_b = jax.nn.silu(split_heads(qb, B_HEADS).astype(jnp.float32)) * (B_KDIM ** -0.5)
    ob = hgrn2_chunkwise(q_b, 1.0 - gate, split_heads(ib, B_HEADS).astype(jnp.float32),
                         jnp.log(gate))
    ob = rmsnorm(ob, out_norm_g) * jax.nn.silu(split_heads(gb, B_HEADS).astype(jnp.float32))
    mixed = jnp.concatenate([merge_heads(oa), merge_heads(ob)], axis=-1)
    return mixed.astype(h.dtype) @ w_out


def odd_mixer(h, w_qkv, b_qkv, sinks, w_o, b_o, cos_c, sin_c):
    B, S, _ = h.shape
    proj = h @ w_qkv + b_qkv
    q, k, v = jnp.split(proj, [C_Q_HEADS * C_DIM, (C_Q_HEADS + C_KV_HEADS) * C_DIM], axis=-1)
    q = apply_partial_rope(split_heads(q, C_Q_HEADS), cos_c, sin_c)
    q = q.reshape(B, C_KV_HEADS, C_GROUP, S, C_DIM)
    k = apply_partial_rope(split_heads(k, C_KV_HEADS), cos_c, sin_c)
    v = split_heads(v, C_KV_HEADS)
    num, _, l = banded_attention(q, k, v, C_WINDOW - 1,
                                 sink=sinks.reshape(C_KV_HEADS, C_GROUP))
    o = (num / l[..., None]).reshape(B, C_Q_HEADS, S, C_DIM)
    return merge_heads(o).astype(h.dtype) @ w_o + b_o


def squared_relu_mlp(h, w1, w2):
    return jnp.square(jax.nn.relu(h @ w1)) @ w2


def setup_inputs(seed: int = 0) -> dict:
    key = jax.random.key(seed)
    ks = jax.random.split(key, 15)

    def nrm(k, shape, scale):
        return scale * jax.random.normal(k, shape, jnp.float32)

    return {
        "x": nrm(ks[0], (BATCH, SEQ, D_MODEL), 1.0),
        "norm_mix_g": 1.0 + nrm(ks[1], (DEPTH, D_MODEL), 0.02),
        "norm_mlp_g": 1.0 + nrm(ks[2], (DEPTH, D_MODEL), 0.02),
        "final_norm_g": 1.0 + nrm(ks[3], (D_MODEL,), 0.02),
        "even_w_in": nrm(ks[4], (N_EVEN, D_MODEL, EVEN_IN), D_MODEL ** -0.5),
        "even_w_out": nrm(ks[5], (N_EVEN, EVEN_MIX, D_MODEL), EVEN_MIX ** -0.5),
        "hgrn_lb_raw": 1.0 + nrm(ks[6], (N_EVEN, B_KWIDTH), 0.1),
        "hgrn_norm_g": 1.0 + nrm(ks[7], (N_EVEN, B_VDIM), 0.02),
        "odd_w_qkv": nrm(ks[8], (N_ODD, D_MODEL, C_QKV), D_MODEL ** -0.5),
        "odd_b_qkv": nrm(ks[9], (N_ODD, C_QKV), 0.02),
        "odd_sinks": nrm(ks[10], (N_ODD, C_Q_HEADS), 1.0),
        "odd_w_o": nrm(ks[11], (N_ODD, C_Q_HEADS * C_DIM, D_MODEL), (C_Q_HEADS * C_DIM) ** -0.5),
        "odd_b_o": nrm(ks[12], (N_ODD, D_MODEL), 0.02),
        "mlp_w1": nrm(ks[13], (DEPTH, D_MODEL, D_FF), D_MODEL ** -0.5),
        "mlp_w2": nrm(ks[14], (DEPTH, D_FF, D_MODEL), D_FF ** -0.5),
    }


def reference(x, norm_mix_g, norm_mlp_g, final_norm_g, even_w_in, even_w_out, hgrn_lb_raw,
              hgrn_norm_g, odd_w_qkv, odd_b_qkv, odd_sinks, odd_w_o, odd_b_o, mlp_w1, mlp_w2):
    S = x.shape[1]
    cos_a, sin_a = rope_tables(S, A_DIM)
    cos_c, sin_c = rope_tables(S, C_DIM)
    lb_soft = jax.nn.softmax(hgrn_lb_raw.astype(jnp.float32), axis=0)
    lower_bounds = jnp.cumsum(lb_soft, axis=0) - lb_soft[0:1]
    for layer in range(DEPTH):
        h = rmsnorm(x, norm_mix_g[layer])
        if layer % 2 == 0:
            e = layer // 2
            mix = even_mixer(h, even_w_in[e], even_w_out[e], lower_bounds[e], hgrn_norm_g[e],
                             cos_a, sin_a)
        else:
            o = layer // 2
            mix = odd_mixer(h, odd_w_qkv[o], odd_b_qkv[o], odd_sinks[o], odd_w_o[o], odd_b_o[o],
                            cos_c, sin_c)
        x = x + mix.astype(x.dtype)
        h = rmsnorm(x, norm_mlp_g[layer])
        x = x + squared_relu_mlp(h, mlp_w1[layer], mlp_w2[layer]).astype(x.dtype)
    return rmsnorm(x, final_norm_g)
```

```python
import functools

import numpy as np
import jax
import jax.numpy as jnp
from jax import lax
from jax.experimental import pallas as pl
from jax.experimental.pallas import tpu as pltpu

D_MODEL = 2048
NORM_EPS = 1e-5
ROPE_THETA = 500000.0
ROPE_FRACTION = 4
BAND = 128

A_DIM = 128
A_HEADS = 8
A_BRANCHES = ((128, 1), (512, 4), (2048, 16))
A_WIDTH = A_HEADS * A_DIM
B_DIM = 128
B_HEADS = 8
B_WIDTH = B_HEADS * B_DIM
B_CHUNK = 64
EVEN_IN = 3 * A_WIDTH + 4 * B_WIDTH

C_DIM = 64
C_Q_HEADS = 32
C_KV_HEADS = 4
C_GROUP = C_Q_HEADS // C_KV_HEADS
C_WINDOW = 128
C_QKV = (C_Q_HEADS + 2 * C_KV_HEADS) * C_DIM
D_FF = 4 * D_MODEL

LANES = 128
V7X_VMEM_BYTES = 64 * 1024 * 1024
COMPILER_TEMP_BYTES = 6 * 1024 * 1024

MASKED = -1e30

_BF16 = jnp.bfloat16
_F32 = jnp.float32
_NT = (((1,), (1,)), ((), ()))
_TN = (((0,), (0,)), ((), ()))


def _tiles():
    return dict(
        proj_tm=1024, proj_tn_even=1024, proj_tn_odd=1280,
        out_tm=512,
        mlp_tm=1024, mlp_tf=512,
        norm_tm=1024,
    )


def _vmem_limit(pipelined_bytes, scratch_bytes=0):
    need = 2 * pipelined_bytes + scratch_bytes + COMPILER_TEMP_BYTES
    assert need <= V7X_VMEM_BYTES, need
    return int(need)


def _nbytes(shape, dtype):
    return int(np.prod(shape)) * jnp.dtype(dtype).itemsize


def _rmsnorm_rows(x, g):
    ms = jnp.mean(x * x, axis=-1, keepdims=True)
    return x * lax.rsqrt(ms + NORM_EPS) * g


def _norm_matmul_kernel(x_ref, g_ref, w_ref, b_ref, o_ref, h_ref):
    @pl.when(pl.program_id(1) == 0)
    def _():
        h_ref[...] = _rmsnorm_rows(x_ref[...], g_ref[...]).astype(_BF16)

    acc = jnp.dot(h_ref[...], w_ref[...], preferred_element_type=_F32)
    o_ref[...] = acc + b_ref[...]


def _norm_matmul(x, g, w, b, tm, tn):
    m, d = x.shape
    n = w.shape[1]
    blocks = (_nbytes((tm, d), _F32) + _nbytes((d, tn), _BF16) + _nbytes((tm, tn), _F32)
              + _nbytes((1, d), _F32) + _nbytes((1, tn), _F32))
    return pl.pallas_call(
        _norm_matmul_kernel,
        grid=(m // tm, n // tn),
        in_specs=[
            pl.BlockSpec((tm, d), lambda i, j: (i, 0)),
            pl.BlockSpec((1, d), lambda i, j: (0, 0)),
            pl.BlockSpec((d, tn), lambda i, j: (0, j)),
            pl.BlockSpec((1, tn), lambda i, j: (0, j)),
        ],
        out_specs=pl.BlockSpec((tm, tn), lambda i, j: (i, j)),
        out_shape=jax.ShapeDtypeStruct((m, n), _F32),
        scratch_shapes=[pltpu.VMEM((tm, d), _BF16)],
        compiler_params=pltpu.CompilerParams(
            dimension_semantics=("arbitrary", "arbitrary"),
            vmem_limit_bytes=_vmem_limit(blocks, _nbytes((tm, d), _BF16))),
        name="norm_matmul",
    )(x, g.reshape(1, d), w, b.reshape(1, n))


def _res_matmul_kernel(*refs, n_in):
    a_refs, w_refs = refs[:n_in], refs[n_in:2 * n_in]
    b_ref, r_ref, o_ref = refs[2 * n_in:]
    acc = r_ref[...] + b_ref[...]
    for a_ref, w_ref in zip(a_refs, w_refs):
        acc = acc + jnp.dot(a_ref[...], w_ref[...], preferred_element_type=_F32)
    o_ref[...] = acc


def _res_matmul(a_list, w_list, b, res, tm):
    m, n = res.shape
    blocks = _nbytes((tm, n), _F32) * 2 + _nbytes((1, n), _F32)
    in_specs = []
    for a in a_list:
        in_specs.append(pl.BlockSpec((tm, a.shape[1]), lambda i: (i, 0)))
        blocks += _nbytes((tm, a.shape[1]), a.dtype)
    for w in w_list:
        in_specs.append(pl.BlockSpec(w.shape, lambda i: (0, 0)))
        blocks += _nbytes(w.shape, w.dtype)
    in_specs += [pl.BlockSpec((1, n), lambda i: (0, 0)), pl.BlockSpec((tm, n), lambda i: (i, 0))]
    return pl.pallas_call(
        functools.partial(_res_matmul_kernel, n_in=len(a_list)),
        grid=(m // tm,),
        in_specs=in_specs,
        out_specs=pl.BlockSpec((tm, n), lambda i: (i, 0)),
        out_shape=jax.ShapeDtypeStruct((m, n), _F32),
        compiler_params=pltpu.CompilerParams(
            dimension_semantics=("arbitrary",), vmem_limit_bytes=_vmem_limit(blocks)),
        name="res_matmul",
    )(*a_list, *w_list, b.reshape(1, n), res)


def _mlp_kernel(x_ref, g_ref, w1_ref, w2_ref, o_ref, h_ref):
    @pl.when(pl.program_id(1) == 0)
    def _():
        x = x_ref[...]
        h_ref[...] = _rmsnorm_rows(x, g_ref[...]).astype(_BF16)
        o_ref[...] = x

    a = jnp.dot(h_ref[...], w1_ref[...], preferred_element_type=_F32)
    a = jnp.square(jnp.maximum(a, 0.0)).astype(_BF16)
    o_ref[...] += jnp.dot(a, w2_ref[...], preferred_element_type=_F32)


def _mlp(x, g, w1, w2, tm, tf):
    m, d = x.shape
    f = w1.shape[1]
    blocks = (2 * _nbytes((tm, d), _F32) + _nbytes((1, d), _F32)
              + _nbytes((d, tf), _BF16) + _nbytes((tf, d), _BF16))
    return pl.pallas_call(
        _mlp_kernel,
        grid=(m // tm, f // tf),
        in_specs=[
            pl.BlockSpec((tm, d), lambda i, j: (i, 0)),
            pl.BlockSpec((1, d), lambda i, j: (0, 0)),
            pl.BlockSpec((d, tf), lambda i, j: (0, j)),
            pl.BlockSpec((tf, d), lambda i, j: (j, 0)),
        ],
        out_specs=pl.BlockSpec((tm, d), lambda i, j: (i, 0)),
        out_shape=jax.ShapeDtypeStruct((m, d), _F32),
        scratch_shapes=[pltpu.VMEM((tm, d), _BF16)],
        compiler_params=pltpu.CompilerParams(
            dimension_semantics=("arbitrary", "arbitrary"),
            vmem_limit_bytes=_vmem_limit(blocks, _nbytes((tm, d), _BF16))),
        name="mlp",
    )(x, g.reshape(1, d), w1, w2)


def _final_norm_kernel(x_ref, g_ref, o_ref):
    o_ref[...] = _rmsnorm_rows(x_ref[...], g_ref[...])


def _final_norm(x, g, tm):
    m, d = x.shape
    return pl.pallas_call(
        _final_norm_kernel,
        grid=(m // tm,),
        in_specs=[pl.BlockSpec((tm, d), lambda i: (i, 0)), pl.BlockSpec((1, d), lambda i: (0, 0))],
        out_specs=pl.BlockSpec((tm, d), lambda i: (i, 0)),
        out_shape=jax.ShapeDtypeStruct((m, d), _F32),
        compiler_params=pltpu.CompilerParams(
            dimension_semantics=("arbitrary",),
            vmem_limit_bytes=_vmem_limit(2 * _nbytes((tm, d), _F32))),
        name="final_norm",
    )(x, g.reshape(1, d))


def _rope_tables(seq, head_dim):
    rot = head_dim // ROPE_FRACTION
    half = rot // 2
    inv_freq = 1.0 / (ROPE_THETA ** (jnp.arange(0, rot, 2, dtype=_F32) / rot))
    ang = jnp.arange(seq, dtype=_F32)[:, None] * inv_freq[None, :]
    cos, sin = jnp.cos(ang), jnp.sin(ang)
    pad = head_dim - 2 * half
    ones = jnp.ones((seq, pad), _F32)
    zeros = jnp.zeros((seq, pad), _F32)
    zh = jnp.zeros((seq, half), _F32)
    reps = LANES // head_dim
    cos_full = jnp.tile(jnp.concatenate([cos, cos, ones], axis=1), (1, reps))
    sin_lo = jnp.tile(jnp.concatenate([-sin, zh, zeros], axis=1), (1, reps))
    sin_hi = jnp.tile(jnp.concatenate([zh, sin, zeros], axis=1), (1, reps))
    return cos_full, sin_lo, sin_hi, half


def _rope(x, cos_full, sin_lo, sin_hi, half):
    up = pltpu.roll(x, LANES - half, 1)
    down = pltpu.roll(x, half, 1)
    return x * cos_full + up * sin_lo + down * sin_hi


def _band_bias():
    i = np.arange(BAND)[:, None]
    j = np.arange(2 * BAND)[None, :]
    dist = i + BAND - j
    return np.where((dist >= 0) & (dist <= BAND), 0.0, MASKED).astype(np.float32)


def _band_bias_window(window):
    i = np.arange(BAND)[:, None]
    j = np.arange(2 * BAND)[None, :]
    dist = i + BAND - j
    return np.where((dist >= 0) & (dist <= window), 0.0, MASKED).astype(np.float32)


def _dilated_kernel(q_ref, k_ref, v_ref, cos_ref, slo_ref, shi_ref, bias_ref, o_ref,
                    qs_ref, ks_ref, vs_ref, acc_ref, m_ref, l_ref, *, half):
    seq = qs_ref.shape[0]
    cos_full, sin_lo, sin_hi = cos_ref[...], slo_ref[...], shi_ref[...]
    qs_ref[...] = _rope(q_ref[0], cos_full, sin_lo, sin_hi, half) * (A_DIM ** -0.5)
    ks_ref[...] = _rope(k_ref[0], cos_full, sin_lo, sin_hi, half)
    vs_ref[...] = v_ref[0]

    def rows(start, stride):
        return pl.ds(start, BAND) if stride == 1 else pl.ds(start, BAND, stride=stride)

    def block(cur, prev, stride, first):
        r_cur = rows(cur, stride)
        q = qs_ref[r_cur, :].astype(_BF16)
        k = ks_ref[r_cur, :]
        v = vs_ref[r_cur, :]
        if prev is None:
            bias = bias_ref[:, BAND:]
        else:
            r_prev = rows(prev, stride)
            k = jnp.concatenate([ks_ref[r_prev, :], k], axis=0)
            v = jnp.concatenate([vs_ref[r_prev, :], v], axis=0)
            bias = bias_ref[...]
        s = lax.dot_general(q, k.astype(_BF16), _NT, preferred_element_type=_F32) + bias
        m = jnp.max(s, axis=-1, keepdims=True)
        p = jnp.exp(s - m)
        l = jnp.sum(p, axis=-1, keepdims=True)
        num = jnp.dot(p.astype(_BF16), v.astype(_BF16), preferred_element_type=_F32)
        if first:
            acc_ref[r_cur, :] = num
            m_ref[r_cur, :] = jnp.broadcast_to(m, (BAND, LANES))
            l_ref[r_cur, :] = jnp.broadcast_to(l, (BAND, LANES))
        else:
            m_old = m_ref[r_cur, :]
            m_new = jnp.maximum(m_old, m)
            a = jnp.exp(m_old - m_new)
            b = jnp.exp(m - m_new)
            acc_ref[r_cur, :] = acc_ref[r_cur, :] * a + num * b
            l_ref[r_cur, :] = l_ref[r_cur, :] * a + l * b
            m_ref[r_cur, :] = m_new

    for idx, (window, dil) in enumerate(A_BRANCHES):
        assert window // dil == BAND
        first = idx == 0
        n_blocks = seq // dil // BAND
        step = BAND * dil

        def residue(r, carry, dil=dil, first=first, n_blocks=n_blocks, step=step):
            block(r, None, dil, first)

            def later(n, c):
                block(r + n * step, r + (n - 1) * step, dil, first)
                return c

            if n_blocks > 1:
                lax.fori_loop(1, n_blocks, later, 0)
            return carry

        if dil == 1:
            residue(0, 0)
        else:
            lax.fori_loop(0, dil, residue, 0)

    o_ref[0] = (acc_ref[...] / l_ref[...]).astype(o_ref.dtype)


def _dilated_attention(proj, tables, bias):
    bsz, seq, _ = proj.shape
    cos_full, sin_lo, sin_hi, half = tables
    head = lambda off: pl.BlockSpec((1, seq, A_DIM), lambda b, h, off=off: (b, 0, off + h))
    full = lambda shape: pl.BlockSpec(shape, lambda b, h: (0, 0))
    blocks = 3 * _nbytes((seq, A_DIM), _F32) + _nbytes((seq, A_DIM), _BF16) \
        + 3 * _nbytes((seq, LANES), _F32) + _nbytes(bias.shape, _F32)
    scratch = 6 * _nbytes((seq, LANES), _F32)
    return pl.pallas_call(
        functools.partial(_dilated_kernel, half=half),
        grid=(bsz, A_HEADS),
        in_specs=[head(0), head(A_HEADS), head(2 * A_HEADS),
                  full((seq, LANES)), full((seq, LANES)), full((seq, LANES)), full(bias.shape)],
        out_specs=pl.BlockSpec((1, seq, A_DIM), lambda b, h: (b, 0, h)),
        out_shape=jax.ShapeDtypeStruct((bsz, seq, A_WIDTH), _BF16),
        scratch_shapes=[pltpu.VMEM((seq, LANES), _F32) for _ in range(6)],
        compiler_params=pltpu.CompilerParams(
            dimension_semantics=("arbitrary", "arbitrary"),
            vmem_limit_bytes=_vmem_limit(blocks, scratch)),
        name="dilated",
    )(proj, proj, proj, cos_full, sin_lo, sin_hi, bias)


_B_LEVELS = (32, 16, 8, 4, 2, 1)


def _hgrn_constants():
    c = B_CHUNK
    t = np.arange(c)[:, None]
    r = np.arange(c)[None, :]
    mats = [(r <= t), (r > t)]
    for h in _B_LEVELS:
        base = (t // (2 * h)) * (2 * h)
        mid = base + h - 1
        upper = (t & h) != 0
        mats.append(np.where(upper, (r > mid) & (r <= t), (r > t) & (r <= mid)))
    sums = np.concatenate(mats, axis=0).astype(np.float32)
    s = np.arange(c)[None, :]
    x = t ^ s
    level = np.where(s > t, -1, np.where(s == t, 0, 1 << (np.floor(np.log2(np.maximum(x, 1))).astype(np.int64))))
    return sums, level.astype(np.int32)


def _hgrn_kernel(q_ref, f_ref, i_ref, g_ref, lb_ref, gain_ref, sums_ref, level_ref, o_ref):
    seq = q_ref.shape[1]
    c = B_CHUNK
    lb = lb_ref[0]
    gain = gain_ref[...]
    sums = sums_ref[...]
    level = level_ref[...]

    def chunk(n, state_t):
        rows = pl.ds(pl.multiple_of(n * c, c), c)
        gate = lb + (1.0 - lb) * (1.0 / (1.0 + jnp.exp(-f_ref[0, rows, :])))
        log_gate = jnp.log(gate)
        k = 1.0 - gate
        qraw = q_ref[0, rows, :]
        q = qraw * (1.0 / (1.0 + jnp.exp(-qraw))) * (B_DIM ** -0.5)
        v = i_ref[0, rows, :].astype(_BF16)

        g_hi = log_gate.astype(_BF16)
        g_lo = (log_gate - g_hi.astype(_F32)).astype(_BF16)
        expo = (jnp.dot(sums, g_hi, preferred_element_type=_F32)
                + jnp.dot(sums, g_lo, preferred_element_type=_F32))
        decay = jnp.exp(expo)

        q_in = (q * decay[0:c]).astype(_BF16)
        k_out = (k * decay[c:2 * c]).astype(_BF16)
        attn = jnp.where(level == 0,
                         lax.dot_general(q.astype(_BF16), k.astype(_BF16), _NT, preferred_element_type=_F32), 0.0)
        for idx, h in enumerate(_B_LEVELS):
            d = decay[(2 + idx) * c:(3 + idx) * c]
            part = lax.dot_general((q * d).astype(_BF16), (k * d).astype(_BF16), _NT,
                                   preferred_element_type=_F32)
            attn = attn + jnp.where(level == h, part, 0.0)

        o = (jnp.dot(attn.astype(_BF16), v, preferred_element_type=_F32)
             + lax.dot_general(q_in, state_t.astype(_BF16), _NT, preferred_element_type=_F32))
        new_state_t = (state_t * decay[c - 1:c]
                       + lax.dot_general(v, k_out, _TN, preferred_element_type=_F32))

        y = _rmsnorm_rows(o, gain)
        graw = g_ref[0, rows, :]
        y = y * (graw * (1.0 / (1.0 + jnp.exp(-graw))))
        o_ref[0, rows, :] = y.astype(o_ref.dtype)
        return new_state_t

    lax.fori_loop(0, seq // c, chunk, jnp.zeros((B_DIM, B_DIM), _F32))


def _hgrn2(proj, lower_bound, gain, sums, level):
    bsz, seq, _ = proj.shape
    base = 3 * A_HEADS
    head = lambda off: pl.BlockSpec((1, seq, B_DIM), lambda b, h, off=off: (b, 0, base + off + h))
    blocks = 4 * _nbytes((seq, B_DIM), _F32) + _nbytes((seq, B_DIM), _BF16) \
        + _nbytes(sums.shape, _BF16) + _nbytes(level.shape, jnp.int32) + 2 * _nbytes((8, LANES), _F32)
    return pl.pallas_call(
        _hgrn_kernel,
        grid=(bsz, B_HEADS),
        in_specs=[head(0), head(B_HEADS), head(2 * B_HEADS), head(3 * B_HEADS),
                  pl.BlockSpec((1, 1, B_DIM), lambda b, h: (h, 0, 0)),
                  pl.BlockSpec((1, B_DIM), lambda b, h: (0, 0)),
                  pl.BlockSpec(sums.shape, lambda b, h: (0, 0)),
                  pl.BlockSpec(level.shape, lambda b, h: (0, 0))],
        out_specs=pl.BlockSpec((1, seq, B_DIM), lambda b, h: (b, 0, h)),
        out_shape=jax.ShapeDtypeStruct((bsz, seq, B_WIDTH), _BF16),
        compiler_params=pltpu.CompilerParams(
            dimension_semantics=("arbitrary", "arbitrary"), vmem_limit_bytes=_vmem_limit(blocks)),
        name="hgrn2",
    )(proj, proj, proj, proj, lower_bound.reshape(B_HEADS, 1, B_DIM), gain.reshape(1, B_DIM), sums, level)


def _swa_kernel(q_ref, k_ref, v_ref, sink_ref, cos_ref, slo_ref, shi_ref, bias_ref, o_ref,
                qs_ref, klo_ref, khi_ref, vlo_ref, vhi_ref, *, half):
    seq = q_ref.shape[1]
    kv_head = pl.program_id(1)
    cos_full, sin_lo, sin_hi = cos_ref[...], slo_ref[...], shi_ref[...]
    lane = lax.broadcasted_iota(jnp.int32, (seq, LANES), 1)
    low = lane < C_DIM
    odd_head = (kv_head % 2) == 1

    def place(x):
        x = jnp.where(odd_head, pltpu.roll(x, C_DIM, 1), x)
        lo = jnp.where(low, x, 0.0)
        return lo.astype(_BF16), pltpu.roll(lo, C_DIM, 1).astype(_BF16)

    klo_ref[...], khi_ref[...] = place(_rope(k_ref[0], cos_full, sin_lo, sin_hi, half))
    vlo_ref[...], vhi_ref[...] = place(v_ref[0])
    pairs = C_GROUP // 2
    for j in range(pairs):
        sl = slice(j * LANES, (j + 1) * LANES)
        qs_ref[:, sl] = (_rope(q_ref[0, :, sl], cos_full, sin_lo, sin_hi, half) * (C_DIM ** -0.5)).astype(_BF16)

    low_q = lax.broadcasted_iota(jnp.int32, (BAND, LANES), 1) < C_DIM

    def block(n, with_prev):
        cur = pl.ds(pl.multiple_of(n * BAND, BAND), BAND)
        if with_prev:
            keys = pl.ds(pl.multiple_of((n - 1) * BAND, BAND), 2 * BAND)
            bias = bias_ref[...]
        else:
            keys = cur
            bias = bias_ref[:, BAND:]
        k_halves = (klo_ref[keys, :], khi_ref[keys, :])
        v_halves = (vlo_ref[keys, :], vhi_ref[keys, :])
        for j in range(pairs):
            q2 = qs_ref[cur, j * LANES:(j + 1) * LANES]
            outs = []
            for side in range(2):
                head = kv_head * C_GROUP + 2 * j + side
                sink = sink_ref[pl.ds(head, 1), :][:, :1]
                s = lax.dot_general(q2, k_halves[side], _NT, preferred_element_type=_F32) + bias
                m = jnp.maximum(jnp.max(s, axis=-1, keepdims=True), sink)
                p = jnp.exp(s - m)
                l = jnp.sum(p, axis=-1, keepdims=True) + jnp.exp(sink - m)
                num = jnp.dot(p.astype(_BF16), v_halves[side], preferred_element_type=_F32)
                outs.append(num / l)
            o_ref[0, cur, j * LANES:(j + 1) * LANES] = jnp.where(low_q, outs[0], outs[1]).astype(o_ref.dtype)

    block(0, False)

    def later(n, carry):
        block(n, True)
        return carry

    lax.fori_loop(1, seq // BAND, later, 0)


def _swa(proj, sinks, tables, bias):
    bsz, seq, _ = proj.shape
    cos_full, sin_lo, sin_hi, half = tables
    qw = C_GROUP * C_DIM
    k_base = C_Q_HEADS * C_DIM // LANES
    v_base = k_base + C_KV_HEADS * C_DIM // LANES
    full = lambda shape: pl.BlockSpec(shape, lambda b, h: (0, 0))
    sink_tab = jnp.broadcast_to(sinks.astype(_F32)[:, None], (C_Q_HEADS, LANES))
    blocks = _nbytes((seq, qw), _F32) + 2 * _nbytes((seq, LANES), _F32) + _nbytes((seq, qw), _BF16) \
        + 3 * _nbytes((seq, LANES), _F32) + _nbytes(bias.shape, _F32) + _nbytes((C_Q_HEADS, LANES), _F32)
    scratch = _nbytes((seq, qw), _BF16) + 4 * _nbytes((seq, LANES), _BF16)
    return pl.pallas_call(
        functools.partial(_swa_kernel, half=half),
        grid=(bsz, C_KV_HEADS),
        in_specs=[pl.BlockSpec((1, seq, qw), lambda b, h: (b, 0, h)),
                  pl.BlockSpec((1, seq, LANES), lambda b, h: (b, 0, k_base + h // 2)),
                  pl.BlockSpec((1, seq, LANES), lambda b, h: (b, 0, v_base + h // 2)),
                  full((C_Q_HEADS, LANES)),
                  full((seq, LANES)), full((seq, LANES)), full((seq, LANES)), full(bias.shape)],
        out_specs=pl.BlockSpec((1, seq, qw), lambda b, h: (b, 0, h)),
        out_shape=jax.ShapeDtypeStruct((bsz, seq, C_Q_HEADS * C_DIM), _BF16),
        scratch_shapes=[pltpu.VMEM((seq, qw), _BF16)] + [pltpu.VMEM((seq, LANES), _BF16) for _ in range(4)],
        compiler_params=pltpu.CompilerParams(
            dimension_semantics=("arbitrary", "arbitrary"),
            vmem_limit_bytes=_vmem_limit(blocks, scratch)),
        name="swa",
    )(proj, proj, proj, sink_tab, cos_full, sin_lo, sin_hi, bias)


def kernel(x, norm_mix_g, norm_mlp_g, final_norm_g, even_w_in, even_w_out, hgrn_lb_raw, hgrn_norm_g,
           odd_w_qkv, odd_b_qkv, odd_sinks, odd_w_o, odd_b_o, mlp_w1, mlp_w2):
    bsz, seq, d = x.shape
    depth = norm_mix_g.shape[0]
    t = _tiles()
    m = bsz * seq

    tables_a = _rope_tables(seq, A_DIM)
    tables_c = _rope_tables(seq, C_DIM)
    bias_a = jnp.asarray(_band_bias_window(BAND))
    bias_c = jnp.asarray(_band_bias_window(C_WINDOW - 1))
    sums_np, level_np = _hgrn_constants()
    sums, level = jnp.asarray(sums_np, _BF16), jnp.asarray(level_np)

    lb_soft = jax.nn.softmax(hgrn_lb_raw.astype(_F32), axis=0)
    lower_bounds = jnp.cumsum(lb_soft, axis=0) - lb_soft[0:1]

    zeros_in = jnp.zeros((EVEN_IN,), _F32)
    zeros_d = jnp.zeros((d,), _F32)

    x2 = x.reshape(m, d)
    for layer in range(depth):
        if layer % 2 == 0:
            e = layer // 2
            proj = _norm_matmul(x2, norm_mix_g[layer], even_w_in[e].astype(_BF16), zeros_in,
                                t["proj_tm"], t["proj_tn_even"]).reshape(bsz, seq, EVEN_IN)
            oa = _dilated_attention(proj, tables_a, bias_a).reshape(m, A_WIDTH)
            ob = _hgrn2(proj, lower_bounds[e], hgrn_norm_g[e], sums, level).reshape(m, B_WIDTH)
            w_out = even_w_out[e].astype(_BF16)
            x2 = _res_matmul([oa, ob], [w_out[:A_WIDTH], w_out[A_WIDTH:]], zeros_d, x2, t["out_tm"])
        else:
            o = layer // 2
            proj = _norm_matmul(x2, norm_mix_g[layer], odd_w_qkv[o].astype(_BF16), odd_b_qkv[o],
                                t["proj_tm"], t["proj_tn_odd"]).reshape(bsz, seq, C_QKV)
            attn = _swa(proj, odd_sinks[o], tables_c, bias_c).reshape(m, C_Q_HEADS * C_DIM)
            x2 = _res_matmul([attn], [odd_w_o[o].astype(_BF16)], odd_b_o[o], x2, t["out_tm"])
        x2 = _mlp(x2, norm_mlp_g[layer], mlp_w1[layer].astype(_BF16), mlp_w2[layer].astype(_BF16),
                  t["mlp_tm"], t["mlp_tf"])
    return _final_norm(x2, final_norm_g, t["norm_tm"]).reshape(bsz, seq, d)
```

```python
import functools

import numpy as np
import jax
import jax.numpy as jnp
from jax import lax
from jax.experimental import pallas as pl
from jax.experimental.pallas import tpu as pltpu

D_MODEL = 2048
NORM_EPS = 1e-5
ROPE_THETA = 500000.0
ROPE_FRACTION = 4
BAND = 128

A_DIM = 128
A_HEADS = 8
A_BRANCHES = ((128, 1), (512, 4), (2048, 16))
A_WIDTH = A_HEADS * A_DIM
B_DIM = 128
B_HEADS = 8
B_WIDTH = B_HEADS * B_DIM
B_CHUNK = 64
EVEN_IN = 3 * A_WIDTH + 4 * B_WIDTH

C_DIM = 64
C_Q_HEADS = 32
C_KV_HEADS = 4
C_GROUP = C_Q_HEADS // C_KV_HEADS
C_WINDOW = 128
C_QKV = (C_Q_HEADS + 2 * C_KV_HEADS) * C_DIM
D_FF = 4 * D_MODEL

LANES = 128
V7X_VMEM_BYTES = 64 * 1024 * 1024
COMPILER_TEMP_BYTES = 6 * 1024 * 1024

MASKED = -1e30

_BF16 = jnp.bfloat16
_F32 = jnp.float32
_NT = (((1,), (1,)), ((), ()))
_TN = (((0,), (0,)), ((), ()))


def _tiles():
    return dict(
        proj_tm=1024, proj_tn_even=1024, proj_tn_odd=1280,
        out_tm=512,
        mlp_tm=1024, mlp_tf=512,
        norm_tm=1024,
        dilated_group=8,
        hgrn_group=4,
    )


def _vmem_limit(pipelined_bytes, scratch_bytes=0):
    need = 2 * pipelined_bytes + scratch_bytes + COMPILER_TEMP_BYTES
    assert need <= V7X_VMEM_BYTES, need
    return int(need)


def _nbytes(shape, dtype):
    return int(np.prod(shape)) * jnp.dtype(dtype).itemsize


def _rmsnorm_rows(x, g):
    ms = jnp.mean(x * x, axis=-1, keepdims=True)
    return x * lax.rsqrt(ms + NORM_EPS) * g


def _norm_matmul_kernel(x_ref, g_ref, w_ref, b_ref, o_ref, h_ref):
    @pl.when(pl.program_id(1) == 0)
    def _():
        h_ref[...] = _rmsnorm_rows(x_ref[...], g_ref[...]).astype(_BF16)

    acc = jnp.dot(h_ref[...], w_ref[...], preferred_element_type=_F32)
    o_ref[...] = acc + b_ref[...]


def _norm_matmul(x, g, w, b, tm, tn):
    m, d = x.shape
    n = w.shape[1]
    blocks = (_nbytes((tm, d), _F32) + _nbytes((d, tn), _BF16) + _nbytes((tm, tn), _F32)
              + _nbytes((1, d), _F32) + _nbytes((1, tn), _F32))
    return pl.pallas_call(
        _norm_matmul_kernel,
        grid=(m // tm, n // tn),
        in_specs=[
            pl.BlockSpec((tm, d), lambda i, j: (i, 0)),
            pl.BlockSpec((1, d), lambda i, j: (0, 0)),
            pl.BlockSpec((d, tn), lambda i, j: (0, j)),
            pl.BlockSpec((1, tn), lambda i, j: (0, j)),
        ],
        out_specs=pl.BlockSpec((tm, tn), lambda i, j: (i, j)),
        out_shape=jax.ShapeDtypeStruct((m, n), _F32),
        scratch_shapes=[pltpu.VMEM((tm, d), _BF16)],
        compiler_params=pltpu.CompilerParams(
            dimension_semantics=("arbitrary", "arbitrary"),
            vmem_limit_bytes=_vmem_limit(blocks, _nbytes((tm, d), _BF16))),
        name="norm_matmul",
    )(x, g.reshape(1, d), w, b.reshape(1, n))


def _res_matmul_kernel(*refs, n_in):
    a_refs, w_refs = refs[:n_in], refs[n_in:2 * n_in]
    b_ref, r_ref, o_ref = refs[2 * n_in:]
    acc = r_ref[...] + b_ref[...]
    for a_ref, w_ref in zip(a_refs, w_refs):
        acc = acc + jnp.dot(a_ref[...], w_ref[...], preferred_element_type=_F32)
    o_ref[...] = acc


def _res_matmul(a_list, w_list, b, res, tm):
    m, n = res.shape
    blocks = _nbytes((tm, n), _F32) * 2 + _nbytes((1, n), _F32)
    in_specs = []
    for a in a_list:
        in_specs.append(pl.BlockSpec((tm, a.shape[1]), lambda i: (i, 0)))
        blocks += _nbytes((tm, a.shape[1]), a.dtype)
    for w in w_list:
        in_specs.append(pl.BlockSpec(w.shape, lambda i: (0, 0)))
        blocks += _nbytes(w.shape, w.dtype)
    in_specs += [pl.BlockSpec((1, n), lambda i: (0, 0)), pl.BlockSpec((tm, n), lambda i: (i, 0))]
    return pl.pallas_call(
        functools.partial(_res_matmul_kernel, n_in=len(a_list)),
        grid=(m // tm,),
        in_specs=in_specs,
        out_specs=pl.BlockSpec((tm, n), lambda i: (i, 0)),
        out_shape=jax.ShapeDtypeStruct((m, n), _F32),
        compiler_params=pltpu.CompilerParams(
            dimension_semantics=("arbitrary",), vmem_limit_bytes=_vmem_limit(blocks)),
        name="res_matmul",
    )(*a_list, *w_list, b.reshape(1, n), res)


def _mlp_kernel(x_ref, g_ref, w1_ref, w2_ref, o_ref, h_ref):
    @pl.when(pl.program_id(1) == 0)
    def _():
        x = x_ref[...]
        h_ref[...] = _rmsnorm_rows(x, g_ref[...]).astype(_BF16)
        o_ref[...] = x

    a = jnp.dot(h_ref[...], w1_ref[...], preferred_element_type=_F32)
    a = jnp.square(jnp.maximum(a, 0.0)).astype(_BF16)
    o_ref[...] += jnp.dot(a, w2_ref[...], preferred_element_type=_F32)


def _mlp(x, g, w1, w2, tm, tf):
    m, d = x.shape
    f = w1.shape[1]
    blocks = (2 * _nbytes((tm, d), _F32) + _nbytes((1, d), _F32)
              + _nbytes((d, tf), _BF16) + _nbytes((tf, d), _BF16))
    return pl.pallas_call(
        _mlp_kernel,
        grid=(m // tm, f // tf),
        in_specs=[
            pl.BlockSpec((tm, d), lambda i, j: (i, 0)),
            pl.BlockSpec((1, d), lambda i, j: (0, 0)),
            pl.BlockSpec((d, tf), lambda i, j: (0, j)),
            pl.BlockSpec((tf, d), lambda i, j: (j, 0)),
        ],
        out_specs=pl.BlockSpec((tm, d), lambda i, j: (i, 0)),
        out_shape=jax.ShapeDtypeStruct((m, d), _F32),
        scratch_shapes=[pltpu.VMEM((tm, d), _BF16)],
        compiler_params=pltpu.CompilerParams(
            dimension_semantics=("arbitrary", "arbitrary"),
            vmem_limit_bytes=_vmem_limit(blocks, _nbytes((tm, d), _BF16))),
        name="mlp",
    )(x, g.reshape(1, d), w1, w2)


def _final_norm_kernel(x_ref, g_ref, o_ref):
    o_ref[...] = _rmsnorm_rows(x_ref[...], g_ref[...])


def _final_norm(x, g, tm):
    m, d = x.shape
    return pl.pallas_call(
        _final_norm_kernel,
        grid=(m // tm,),
        in_specs=[pl.BlockSpec((tm, d), lambda i: (i, 0)), pl.BlockSpec((1, d), lambda i: (0, 0))],
        out_specs=pl.BlockSpec((tm, d), lambda i: (i, 0)),
        out_shape=jax.ShapeDtypeStruct((m, d), _F32),
        compiler_params=pltpu.CompilerParams(
            dimension_semantics=("arbitrary",),
            vmem_limit_bytes=_vmem_limit(2 * _nbytes((tm, d), _F32))),
        name="final_norm",
    )(x, g.reshape(1, d))


def _rope_tables(seq, head_dim):
    rot = head_dim // ROPE_FRACTION
    half = rot // 2
    inv_freq = 1.0 / (ROPE_THETA ** (jnp.arange(0, rot, 2, dtype=_F32) / rot))
    ang = jnp.arange(seq, dtype=_F32)[:, None] * inv_freq[None, :]
    cos, sin = jnp.cos(ang), jnp.sin(ang)
    pad = head_dim - 2 * half
    ones = jnp.ones((seq, pad), _F32)
    zeros = jnp.zeros((seq, pad), _F32)
    zh = jnp.zeros((seq, half), _F32)
    reps = LANES // head_dim
    cos_full = jnp.tile(jnp.concatenate([cos, cos, ones], axis=1), (1, reps))
    sin_lo = jnp.tile(jnp.concatenate([-sin, zh, zeros], axis=1), (1, reps))
    sin_hi = jnp.tile(jnp.concatenate([zh, sin, zeros], axis=1), (1, reps))
    return cos_full, sin_lo, sin_hi, half


def _rope(x, cos_full, sin_lo, sin_hi, half):
    up = pltpu.roll(x, LANES - half, 1)
    down = pltpu.roll(x, half, 1)
    return x * cos_full + up * sin_lo + down * sin_hi


def _band_bias():
    i = np.arange(BAND)[:, None]
    j = np.arange(2 * BAND)[None, :]
    dist = i + BAND - j
    return np.where((dist >= 0) & (dist <= BAND), 0.0, MASKED).astype(np.float32)


def _band_bias_window(window):
    i = np.arange(BAND)[:, None]
    j = np.arange(2 * BAND)[None, :]
    dist = i + BAND - j
    return np.where((dist >= 0) & (dist <= window), 0.0, MASKED).astype(np.float32)


def _dilated_kernel(q_ref, k_ref, v_ref, cos_ref, slo_ref, shi_ref, bias_ref, o_ref,
                    qs_ref, ks_ref, vs_ref, qp_ref, kp_ref, vp_ref, acc_ref, m_ref, l_ref, *, half, group):
    seq = qs_ref.shape[0]
    n_all = seq // BAND
    cos_full, sin_lo, sin_hi = cos_ref[...], slo_ref[...], shi_ref[...]
    qs_ref[...] = _rope(q_ref[0], cos_full, sin_lo, sin_hi, half) * (A_DIM ** -0.5)
    ks_ref[...] = _rope(k_ref[0], cos_full, sin_lo, sin_hi, half)
    vs_ref[...] = v_ref[0]

    bias_full = bias_ref[...]
    col = lax.broadcasted_iota(jnp.int32, (BAND, 2 * BAND), 1)
    bias_first = jnp.where(col < BAND, MASKED, bias_full)
    batch_nt = (((2,), (2,)), ((0,), (0,)))
    batch_nn = (((2,), (1,)), ((0,), (0,)))

    for idx, (window, dil) in enumerate(A_BRANCHES):
        assert window // dil == BAND
        length = seq // dil
        nb = length // BAND
        for r in range(dil):
            src_rows = slice(None) if dil == 1 else pl.ds(r, length, stride=dil)
            dst_rows = slice(r * length, (r + 1) * length)
            qp_ref[dst_rows, :] = qs_ref[src_rows, :].astype(_BF16)
            kp_ref[dst_rows, :] = ks_ref[src_rows, :].astype(_BF16)
            vp_ref[dst_rows, :] = vs_ref[src_rows, :].astype(_BF16)

        for g0 in range(0, n_all, group):
            cur = slice(g0 * BAND, (g0 + group) * BAND)
            q = qp_ref[cur, :].reshape(group, BAND, A_DIM)
            k = kp_ref[cur, :].reshape(group, BAND, A_DIM)
            v = vp_ref[cur, :].reshape(group, BAND, A_DIM)
            if nb > 1:
                if g0 == 0:
                    prev_of = lambda ref: jnp.concatenate(
                        [ref[0:BAND, :], ref[0:(group - 1) * BAND, :]], axis=0)
                else:
                    prev_of = lambda ref: ref[(g0 - 1) * BAND:(g0 + group - 1) * BAND, :]
                k = jnp.concatenate([prev_of(kp_ref).reshape(group, BAND, A_DIM), k], axis=1)
                v = jnp.concatenate([prev_of(vp_ref).reshape(group, BAND, A_DIM), v], axis=1)
                biases = [bias_first if (g0 + g) % nb == 0 else bias_full for g in range(group)]
            else:
                biases = [bias_full[:, BAND:]] * group
            s = lax.dot_general(q, k, batch_nt, preferred_element_type=_F32)
            s = jnp.stack([s[g] + biases[g] for g in range(group)])
            m = jnp.max(s, axis=-1, keepdims=True)
            p = jnp.exp(s - m)
            l = jnp.sum(p, axis=-1, keepdims=True)
            num = lax.dot_general(p.astype(_BF16), v, batch_nn, preferred_element_type=_F32)

            for r in range(g0 // nb, -(-(g0 + group) // nb)):
                lo_b = max(g0, r * nb)
                hi_b = min(g0 + group, (r + 1) * nb)
                rows_n = (hi_b - lo_b) * BAND
                start = r + (lo_b - r * nb) * BAND * dil
                rows_o = pl.ds(start, rows_n) if dil == 1 else pl.ds(start, rows_n, stride=dil)
                sel = slice(lo_b - g0, hi_b - g0)
                num_c = num[sel].reshape(rows_n, A_DIM)
                m_c = m[sel].reshape(rows_n, 1)
                l_c = l[sel].reshape(rows_n, 1)
                if idx == 0:
                    acc_ref[rows_o, :] = num_c
                    m_ref[rows_o, :] = jnp.broadcast_to(m_c, (rows_n, LANES))
                    l_ref[rows_o, :] = jnp.broadcast_to(l_c, (rows_n, LANES))
                else:
                    m_old = m_ref[rows_o, :][:, :1]
                    m_new = jnp.maximum(m_old, m_c)
                    a = jnp.exp(m_old - m_new)
                    b = jnp.exp(m_c - m_new)
                    acc_ref[rows_o, :] = acc_ref[rows_o, :] * a + num_c * b
                    l_ref[rows_o, :] = l_ref[rows_o, :] * a + jnp.broadcast_to(l_c * b, (rows_n, LANES))
                    m_ref[rows_o, :] = jnp.broadcast_to(m_new, (rows_n, LANES))

    o_ref[0] = (acc_ref[...] / l_ref[...]).astype(o_ref.dtype)


def _dilated_attention(proj, tables, bias, group):
    bsz, seq, _ = proj.shape
    cos_full, sin_lo, sin_hi, half = tables
    head = lambda off: pl.BlockSpec((1, seq, A_DIM), lambda b, h, off=off: (b, 0, off + h))
    full = lambda shape: pl.BlockSpec(shape, lambda b, h: (0, 0))
    blocks = 3 * _nbytes((seq, A_DIM), _F32) + _nbytes((seq, A_DIM), _BF16) \
        + 3 * _nbytes((seq, LANES), _F32) + _nbytes(bias.shape, _F32)
    scratch = 6 * _nbytes((seq, LANES), _F32) + 3 * _nbytes((seq, LANES), _BF16)
    temporaries = 2 * group * _nbytes((BAND, 2 * BAND), _F32)
    return pl.pallas_call(
        functools.partial(_dilated_kernel, half=half, group=group),
        grid=(bsz, A_HEADS),
        in_specs=[head(0), head(A_HEADS), head(2 * A_HEADS),
                  full((seq, LANES)), full((seq, LANES)), full((seq, LANES)), full(bias.shape)],
        out_specs=pl.BlockSpec((1, seq, A_DIM), lambda b, h: (b, 0, h)),
        out_shape=jax.ShapeDtypeStruct((bsz, seq, A_WIDTH), _BF16),
        scratch_shapes=[pltpu.VMEM((seq, LANES), _F32) for _ in range(3)]
        + [pltpu.VMEM((seq, LANES), _BF16) for _ in range(3)]
        + [pltpu.VMEM((seq, LANES), _F32) for _ in range(3)],
        compiler_params=pltpu.CompilerParams(
            dimension_semantics=("arbitrary", "arbitrary"),
            vmem_limit_bytes=_vmem_limit(blocks, scratch + temporaries)),
        name="dilated",
    )(proj, proj, proj, cos_full, sin_lo, sin_hi, bias)


_B_LEVELS = (32, 16, 8, 4, 2, 1)


def _hgrn_constants():
    c = B_CHUNK
    t = np.arange(c)[:, None]
    r = np.arange(c)[None, :]
    mats = [(r <= t), (r > t)]
    for h in _B_LEVELS:
        base = (t // (2 * h)) * (2 * h)
        mid = base + h - 1
        upper = (t & h) != 0
        mats.append(np.where(upper, (r > mid) & (r <= t), (r > t) & (r <= mid)))
    sums = np.concatenate(mats, axis=0).astype(np.float32)
    s = np.arange(c)[None, :]
    x = t ^ s
    level = np.where(s > t, -1, np.where(s == t, 0, 1 << (np.floor(np.log2(np.maximum(x, 1))).astype(np.int64))))
    return sums, level.astype(np.int32)


def _hgrn_kernel(q_ref, f_ref, i_ref, g_ref, lb_ref, gain_ref, sums_ref, level_ref, o_ref, *, group):
    seq = q_ref.shape[1]
    c = B_CHUNK
    span = group * c
    lb = lb_ref[0]
    gain = gain_ref[...]
    sums2 = sums_ref[...]
    level = level_ref[...]
    chunks = range(group)

    def piece(x, j):
        return x[j * c:(j + 1) * c]

    def step(n, state_t):
        rows = pl.ds(pl.multiple_of(n * span, span), span)
        gate = lb + (1.0 - lb) * (1.0 / (1.0 + jnp.exp(-f_ref[0, rows, :])))
        log_gate = jnp.log(gate)
        k = 1.0 - gate
        qraw = q_ref[0, rows, :]
        q = qraw * (1.0 / (1.0 + jnp.exp(-qraw))) * (B_DIM ** -0.5)
        v = i_ref[0, rows, :].astype(_BF16)

        g_hi = log_gate.astype(_BF16)
        g_lo = (log_gate - g_hi.astype(_F32)).astype(_BF16)
        hi_lo = [jnp.concatenate([piece(g_hi, j), piece(g_lo, j)], axis=0) for j in chunks]
        expo = []
        for j in range(0, group, 2):
            both = jnp.dot(sums2, jnp.concatenate([hi_lo[j], hi_lo[j + 1]], axis=1),
                           preferred_element_type=_F32)
            expo += [both[:, :B_DIM], both[:, B_DIM:]]
        decay = [jnp.exp(e) for e in expo]

        qc = [piece(q, j) for j in chunks]
        kc = [piece(k, j) for j in chunks]
        vc = [piece(v, j) for j in chunks]
        scaled_q = [[qc[j].astype(_BF16)] + [(qc[j] * piece(decay[j], 2 + i)).astype(_BF16)
                                             for i in range(len(_B_LEVELS))] for j in chunks]
        scaled_k = [[kc[j].astype(_BF16)] + [(kc[j] * piece(decay[j], 2 + i)).astype(_BF16)
                                             for i in range(len(_B_LEVELS))] for j in chunks]
        q_in = [(qc[j] * piece(decay[j], 0)).astype(_BF16) for j in chunks]
        k_out = [(kc[j] * piece(decay[j], 1)).astype(_BF16) for j in chunks]
        parts = [[lax.dot_general(a, b, _NT, preferred_element_type=_F32)
                  for a, b in zip(scaled_q[j], scaled_k[j])] for j in chunks]
        attn = []
        for j in chunks:
            total = jnp.where(level == 0, parts[j][0], 0.0)
            for i, h in enumerate(_B_LEVELS):
                total = total + jnp.where(level == h, parts[j][1 + i], 0.0)
            attn.append(total.astype(_BF16))
        intra = [jnp.dot(attn[j], vc[j], preferred_element_type=_F32) for j in chunks]
        update = [lax.dot_general(vc[j], k_out[j], _TN, preferred_element_type=_F32) for j in chunks]

        outs = []
        for j in chunks:
            outs.append(intra[j] + lax.dot_general(q_in[j], state_t.astype(_BF16), _NT,
                                                   preferred_element_type=_F32))
            state_t = state_t * decay[j][c - 1:c] + update[j]

        y = _rmsnorm_rows(jnp.concatenate(outs, axis=0), gain)
        graw = g_ref[0, rows, :]
        y = y * (graw * (1.0 / (1.0 + jnp.exp(-graw))))
        o_ref[0, rows, :] = y.astype(o_ref.dtype)
        return state_t

    lax.fori_loop(0, seq // span, step, jnp.zeros((B_DIM, B_DIM), _F32))


def _hgrn2(proj, lower_bound, gain, sums, level, group):
    bsz, seq, _ = proj.shape
    assert group % 2 == 0 and seq % (group * B_CHUNK) == 0
    sums = jnp.concatenate([sums, sums], axis=1)
    base = 3 * A_HEADS
    head = lambda off: pl.BlockSpec((1, seq, B_DIM), lambda b, h, off=off: (b, 0, base + off + h))
    blocks = 4 * _nbytes((seq, B_DIM), _F32) + _nbytes((seq, B_DIM), _BF16) \
        + _nbytes(sums.shape, _BF16) + _nbytes(level.shape, jnp.int32) + 2 * _nbytes((8, LANES), _F32)
    return pl.pallas_call(
        functools.partial(_hgrn_kernel, group=group),
        grid=(bsz, B_HEADS),
        in_specs=[head(0), head(B_HEADS), head(2 * B_HEADS), head(3 * B_HEADS),
                  pl.BlockSpec((1, 1, B_DIM), lambda b, h: (h, 0, 0)),
                  pl.BlockSpec((1, B_DIM), lambda b, h: (0, 0)),
                  pl.BlockSpec(sums.shape, lambda b, h: (0, 0)),
                  pl.BlockSpec(level.shape, lambda b, h: (0, 0))],
        out_specs=pl.BlockSpec((1, seq, B_DIM), lambda b, h: (b, 0, h)),
        out_shape=jax.ShapeDtypeStruct((bsz, seq, B_WIDTH), _BF16),
        compiler_params=pltpu.CompilerParams(
            dimension_semantics=("arbitrary", "arbitrary"), vmem_limit_bytes=_vmem_limit(blocks)),
        name="hgrn2",
    )(proj, proj, proj, proj, lower_bound.reshape(B_HEADS, 1, B_DIM), gain.reshape(1, B_DIM), sums, level)


def _swa_kernel(q_ref, k_ref, v_ref, sink_ref, cos_ref, slo_ref, shi_ref, bias_ref, o_ref,
                qs_ref, klo_ref, khi_ref, vlo_ref, vhi_ref, *, half):
    seq = q_ref.shape[1]
    kv_head = pl.program_id(1)
    cos_full, sin_lo, sin_hi = cos_ref[...], slo_ref[...], shi_ref[...]
    lane = lax.broadcasted_iota(jnp.int32, (seq, LANES), 1)
    low = lane < C_DIM
    odd_head = (kv_head % 2) == 1

    def place(x):
        x = jnp.where(odd_head, pltpu.roll(x, C_DIM, 1), x)
        lo = jnp.where(low, x, 0.0)
        return lo.astype(_BF16), pltpu.roll(lo, C_DIM, 1).astype(_BF16)

    klo_ref[...], khi_ref[...] = place(_rope(k_ref[0], cos_full, sin_lo, sin_hi, half))
    vlo_ref[...], vhi_ref[...] = place(v_ref[0])
    pairs = C_GROUP // 2
    for j in range(pairs):
        sl = slice(j * LANES, (j + 1) * LANES)
        qs_ref[j] = (_rope(q_ref[0, :, sl], cos_full, sin_lo, sin_hi, half) * (C_DIM ** -0.5)).astype(_BF16)

    heads = [(j, side) for side in range(2) for j in range(pairs)]
    sinks = [sink_ref[kv_head * C_GROUP + 2 * j + side] for j, side in heads]
    low_q = lax.broadcasted_iota(jnp.int32, (BAND, LANES), 1) < C_DIM
    k_refs, v_refs = (klo_ref, khi_ref), (vlo_ref, vhi_ref)

    def block(n, with_prev):
        cur = pl.ds(pl.multiple_of(n * BAND, BAND), BAND)
        if with_prev:
            keys = pl.ds(pl.multiple_of((n - 1) * BAND, BAND), 2 * BAND)
            bias = bias_ref[...]
        else:
            keys = cur
            bias = bias_ref[:, BAND:]
        q4 = qs_ref[:, cur, :].reshape(pairs * BAND, LANES)
        s_side = [lax.dot_general(q4, k_refs[side][keys, :], _NT, preferred_element_type=_F32)
                  for side in range(2)]
        s = [s_side[side][j * BAND:(j + 1) * BAND] + bias for j, side in heads]
        m = [jnp.maximum(jnp.max(s_h, axis=-1, keepdims=True), sink) for s_h, sink in zip(s, sinks)]
        p = [jnp.exp(s_h - m_h) for s_h, m_h in zip(s, m)]
        l = [jnp.sum(p_h, axis=-1, keepdims=True) + jnp.exp(sink - m_h) for p_h, m_h, sink in zip(p, m, sinks)]
        num = [jnp.dot(jnp.concatenate(p[side * pairs:(side + 1) * pairs], axis=0).astype(_BF16),
                       v_refs[side][keys, :], preferred_element_type=_F32) for side in range(2)]
        inv = [1.0 / l_h for l_h in l]
        for j in range(pairs):
            rows = slice(j * BAND, (j + 1) * BAND)
            out = jnp.where(low_q, num[0][rows] * inv[j], num[1][rows] * inv[pairs + j])
            o_ref[0, cur, j * LANES:(j + 1) * LANES] = out.astype(o_ref.dtype)

    block(0, False)

    def later(n, carry):
        block(n, True)
        return carry

    lax.fori_loop(1, seq // BAND, later, 0)


def _swa(proj, sinks, tables, bias):
    bsz, seq, _ = proj.shape
    cos_full, sin_lo, sin_hi, half = tables
    qw = C_GROUP * C_DIM
    k_base = C_Q_HEADS * C_DIM // LANES
    v_base = k_base + C_KV_HEADS * C_DIM // LANES
    full = lambda shape: pl.BlockSpec(shape, lambda b, h: (0, 0))
    blocks = _nbytes((seq, qw), _F32) + 2 * _nbytes((seq, LANES), _F32) + _nbytes((seq, qw), _BF16) \
        + 3 * _nbytes((seq, LANES), _F32) + _nbytes(bias.shape, _F32)
    scratch = _nbytes((seq, qw), _BF16) + 4 * _nbytes((seq, LANES), _BF16)
    return pl.pallas_call(
        functools.partial(_swa_kernel, half=half),
        grid=(bsz, C_KV_HEADS),
        in_specs=[pl.BlockSpec((1, seq, qw), lambda b, h: (b, 0, h)),
                  pl.BlockSpec((1, seq, LANES), lambda b, h: (b, 0, k_base + h // 2)),
                  pl.BlockSpec((1, seq, LANES), lambda b, h: (b, 0, v_base + h // 2)),
                  pl.BlockSpec(memory_space=pltpu.SMEM),
                  full((seq, LANES)), full((seq, LANES)), full((seq, LANES)), full(bias.shape)],
        out_specs=pl.BlockSpec((1, seq, qw), lambda b, h: (b, 0, h)),
        out_shape=jax.ShapeDtypeStruct((bsz, seq, C_Q_HEADS * C_DIM), _BF16),
        scratch_shapes=[pltpu.VMEM((C_GROUP // 2, seq, LANES), _BF16)]
        + [pltpu.VMEM((seq, LANES), _BF16) for _ in range(4)],
        compiler_params=pltpu.CompilerParams(
            dimension_semantics=("arbitrary", "arbitrary"),
            vmem_limit_bytes=_vmem_limit(blocks, scratch)),
        name="swa",
    )(proj, proj, proj, sinks.astype(_F32), cos_full, sin_lo, sin_hi, bias)


def kernel(x, norm_mix_g, norm_mlp_g, final_norm_g, even_w_in, even_w_out, hgrn_lb_raw, hgrn_norm_g,
           odd_w_qkv, odd_b_qkv, odd_sinks, odd_w_o, odd_b_o, mlp_w1, mlp_w2):
    bsz, seq, d = x.shape
    depth = norm_mix_g.shape[0]
    t = _tiles()
    m = bsz * seq

    tables_a = _rope_tables(seq, A_DIM)
    tables_c = _rope_tables(seq, C_DIM)
    bias_a = jnp.asarray(_band_bias_window(BAND))
    bias_c = jnp.asarray(_band_bias_window(C_WINDOW - 1))
    sums_np, level_np = _hgrn_constants()
    sums, level = jnp.asarray(sums_np, _BF16), jnp.asarray(level_np)

    lb_soft = jax.nn.softmax(hgrn_lb_raw.astype(_F32), axis=0)
    lower_bounds = jnp.cumsum(lb_soft, axis=0) - lb_soft[0:1]

    zeros_in = jnp.zeros((EVEN_IN,), _F32)
    zeros_d = jnp.zeros((d,), _F32)

    x2 = x.reshape(m, d)
    for layer in range(depth):
        if layer % 2 == 0:
            e = layer // 2
            proj = _norm_matmul(x2, norm_mix_g[layer], even_w_in[e].astype(_BF16), zeros_in,
                                t["proj_tm"], t["proj_tn_even"]).reshape(bsz, seq, EVEN_IN)
            oa = _dilated_attention(proj, tables_a, bias_a, t["dilated_group"]).reshape(m, A_WIDTH)
            ob = _hgrn2(proj, lower_bounds[e], hgrn_norm_g[e], sums, level, t["hgrn_group"]).reshape(m, B_WIDTH)
            w_out = even_w_out[e].astype(_BF16)
            x2 = _res_matmul([oa, ob], [w_out[:A_WIDTH], w_out[A_WIDTH:]], zeros_d, x2, t["out_tm"])
        else:
            o = layer // 2
            proj = _norm_matmul(x2, norm_mix_g[layer], odd_w_qkv[o].astype(_BF16), odd_b_qkv[o],
                                t["proj_tm"], t["proj_tn_odd"]).reshape(bsz, seq, C_QKV)
            attn = _swa(proj, odd_sinks[o], tables_c, bias_c).reshape(m, C_Q_HEADS * C_DIM)
            x2 = _res_matmul([attn], [odd_w_o[o].astype(_BF16)], odd_b_o[o], x2, t["out_tm"])
        x2 = _mlp(x2, norm_mlp_g[layer], mlp_w1[layer].astype(_BF16), mlp_w2[layer].astype(_BF16),
                  t["mlp_tm"], t["mlp_tf"])
    return _final_norm(x2, final_norm_g, t["norm_tm"]).reshape(bsz, seq, d)
```

```python
import functools

import numpy as np
import jax
import jax.numpy as jnp
from jax import lax
from jax.experimental import pallas as pl
from jax.experimental.pallas import tpu as pltpu

D_MODEL = 2048
NORM_EPS = 1e-5
ROPE_THETA = 500000.0
ROPE_FRACTION = 4
BAND = 128

A_DIM = 128
A_HEADS = 8
A_BRANCHES = ((128, 1), (512, 4), (2048, 16))
A_WIDTH = A_HEADS * A_DIM
B_DIM = 128
B_HEADS = 8
B_WIDTH = B_HEADS * B_DIM
B_CHUNK = 64
A_IN = 3 * A_WIDTH
B_IN = 4 * B_WIDTH

C_DIM = 64
C_Q_HEADS = 32
C_KV_HEADS = 4
C_GROUP = C_Q_HEADS // C_KV_HEADS
C_WINDOW = 128
C_QKV = (C_Q_HEADS + 2 * C_KV_HEADS) * C_DIM
D_FF = 4 * D_MODEL

LANES = 128
V7X_VMEM_BYTES = 64 * 1024 * 1024
COMPILER_TEMP_BYTES = 6 * 1024 * 1024

MASKED = -1e30

_BF16 = jnp.bfloat16
_F32 = jnp.float32
_NT = (((1,), (1,)), ((), ()))
_TN = (((0,), (0,)), ((), ()))


def _tiles():
    return dict(
        proj_tm=512,
        out_tm=512,
        mlp_tm=1024, mlp_tf=512,
        norm_tm=1024,
        dilated_group=16,
        hgrn_group=8,
    )


def _vmem_limit(pipelined_bytes, single_bytes=0):
    need = 2 * pipelined_bytes + single_bytes + COMPILER_TEMP_BYTES
    assert need <= V7X_VMEM_BYTES, need
    return int(need)


def _nbytes(shape, dtype):
    return int(np.prod(shape)) * jnp.dtype(dtype).itemsize


def _rmsnorm_rows(x, g):
    ms = jnp.mean(x * x, axis=-1, keepdims=True)
    return x * lax.rsqrt(ms + NORM_EPS) * g


def _rope_tables(seq, head_dim):
    rot = head_dim // ROPE_FRACTION
    half = rot // 2
    inv_freq = 1.0 / (ROPE_THETA ** (jnp.arange(0, rot, 2, dtype=_F32) / rot))
    ang = jnp.arange(seq, dtype=_F32)[:, None] * inv_freq[None, :]
    cos, sin = jnp.cos(ang), jnp.sin(ang)
    pad = head_dim - 2 * half
    ones = jnp.ones((seq, pad), _F32)
    zeros = jnp.zeros((seq, pad), _F32)
    zh = jnp.zeros((seq, half), _F32)
    reps = LANES // head_dim
    cos_full = jnp.tile(jnp.concatenate([cos, cos, ones], axis=1), (1, reps))
    sin_lo = jnp.tile(jnp.concatenate([-sin, zh, zeros], axis=1), (1, reps))
    sin_hi = jnp.tile(jnp.concatenate([zh, sin, zeros], axis=1), (1, reps))
    return cos_full, sin_lo, sin_hi, half


def _rope(x, cos_full, sin_lo, sin_hi, half):
    up = pltpu.roll(x, LANES - half, 1)
    down = pltpu.roll(x, half, 1)
    return x * cos_full + up * sin_lo + down * sin_hi


def _band_bias_window(window):
    i = np.arange(BAND)[:, None]
    j = np.arange(2 * BAND)[None, :]
    dist = i + BAND - j
    return np.where((dist >= 0) & (dist <= window), 0.0, MASKED).astype(np.float32)


def _norm_proj_kernel(*refs, rope_scales, half):
    if half is None:
        x_ref, g_ref, w_ref, b_ref, o_ref = refs
    else:
        x_ref, g_ref, w_ref, b_ref, cos_ref, slo_ref, shi_ref, o_ref = refs
        cos_full, sin_lo, sin_hi = cos_ref[...], slo_ref[...], shi_ref[...]
    h = _rmsnorm_rows(x_ref[...], g_ref[...]).astype(_BF16)
    acc = jnp.dot(h, w_ref[0], preferred_element_type=_F32) + b_ref[...]
    for gi, scale in enumerate(rope_scales):
        sl = slice(gi * LANES, (gi + 1) * LANES)
        if scale is None:
            o_ref[:, sl] = acc[:, sl]
        else:
            o_ref[:, sl] = _rope(acc[:, sl], cos_full, sin_lo, sin_hi, half) * scale


def _norm_proj(x, g, w_stack, layer, col_block, n, b, tables, rope_scales, seq, tm):
    m, d = x.shape
    assert len(rope_scales) * LANES == n
    in_specs = [
        pl.BlockSpec((tm, d), lambda i: (i, 0)),
        pl.BlockSpec((1, d), lambda i: (0, 0)),
        pl.BlockSpec((1, d, n), lambda i: (layer, 0, col_block), pipeline_mode=pl.Buffered(1)),
        pl.BlockSpec((1, n), lambda i: (0, 0)),
    ]
    args = [x, g.reshape(1, d), w_stack, b.reshape(1, n)]
    blocks = _nbytes((tm, d), _F32) + _nbytes((tm, n), _F32) + _nbytes((1, d), _F32) + _nbytes((1, n), _F32)
    half = None
    if tables is not None:
        cos_full, sin_lo, sin_hi, half = tables
        per_seq = seq // tm
        in_specs += [pl.BlockSpec((tm, LANES), lambda i: (i % per_seq, 0))] * 3
        args += [cos_full, sin_lo, sin_hi]
        blocks += 3 * _nbytes((tm, LANES), _F32)
    single = _nbytes((d, n), _BF16) + _nbytes((tm, d), _BF16) + _nbytes((tm, n), _F32)
    return pl.pallas_call(
        functools.partial(_norm_proj_kernel, rope_scales=tuple(rope_scales), half=half),
        grid=(m // tm,),
        in_specs=in_specs,
        out_specs=pl.BlockSpec((tm, n), lambda i: (i, 0)),
        out_shape=jax.ShapeDtypeStruct((m, n), _F32),
        compiler_params=pltpu.CompilerParams(
            dimension_semantics=("arbitrary",), vmem_limit_bytes=_vmem_limit(blocks, single)),
        name="norm_proj",
    )(*args)


def _res_matmul_kernel(*refs, n_in):
    a_refs, w_refs = refs[:n_in], refs[n_in:2 * n_in]
    b_ref, r_ref, o_ref = refs[2 * n_in:]
    acc = r_ref[...] + b_ref[...]
    for a_ref, w_ref in zip(a_refs, w_refs):
        acc = acc + jnp.dot(a_ref[...], w_ref[0], preferred_element_type=_F32)
    o_ref[...] = acc


def _res_matmul(a_list, w_stack, layer, b, res, tm):
    m, n = res.shape
    k = a_list[0].shape[1]
    blocks = _nbytes((tm, n), _F32) * 2 + _nbytes((1, n), _F32) + len(a_list) * _nbytes((tm, k), _BF16)
    in_specs = [pl.BlockSpec((tm, k), lambda i: (i, 0)) for _ in a_list]
    in_specs += [pl.BlockSpec((1, k, n), lambda i, r=r: (layer, r, 0), pipeline_mode=pl.Buffered(1))
                 for r in range(len(a_list))]
    in_specs += [pl.BlockSpec((1, n), lambda i: (0, 0)), pl.BlockSpec((tm, n), lambda i: (i, 0))]
    return pl.pallas_call(
        functools.partial(_res_matmul_kernel, n_in=len(a_list)),
        grid=(m // tm,),
        in_specs=in_specs,
        out_specs=pl.BlockSpec((tm, n), lambda i: (i, 0)),
        out_shape=jax.ShapeDtypeStruct((m, n), _F32),
        compiler_params=pltpu.CompilerParams(
            dimension_semantics=("arbitrary",),
            vmem_limit_bytes=_vmem_limit(blocks, len(a_list) * _nbytes((k, n), _BF16))),
        name="res_matmul",
    )(*a_list, *([w_stack] * len(a_list)), b.reshape(1, n), res)


def _mlp_kernel(x_ref, g_ref, w1_ref, w2_ref, o_ref, h_ref):
    @pl.when(pl.program_id(1) == 0)
    def _():
        x = x_ref[...]
        h_ref[...] = _rmsnorm_rows(x, g_ref[...]).astype(_BF16)
        o_ref[...] = x

    a = jnp.dot(h_ref[...], w1_ref[0], preferred_element_type=_F32)
    a = jnp.square(jnp.maximum(a, 0.0)).astype(_BF16)
    o_ref[...] += jnp.dot(a, w2_ref[0], preferred_element_type=_F32)


def _mlp(x, g, w1_stack, w2_stack, layer, tm, tf):
    m, d = x.shape
    f = w1_stack.shape[2]
    blocks = (2 * _nbytes((tm, d), _F32) + _nbytes((1, d), _F32)
              + _nbytes((d, tf), _BF16) + _nbytes((tf, d), _BF16))
    return pl.pallas_call(
        _mlp_kernel,
        grid=(m // tm, f // tf),
        in_specs=[
            pl.BlockSpec((tm, d), lambda i, j: (i, 0)),
            pl.BlockSpec((1, d), lambda i, j: (0, 0)),
            pl.BlockSpec((1, d, tf), lambda i, j: (layer, 0, j)),
            pl.BlockSpec((1, tf, d), lambda i, j: (layer, j, 0)),
        ],
        out_specs=pl.BlockSpec((tm, d), lambda i, j: (i, 0)),
        out_shape=jax.ShapeDtypeStruct((m, d), _F32),
        scratch_shapes=[pltpu.VMEM((tm, d), _BF16)],
        compiler_params=pltpu.CompilerParams(
            dimension_semantics=("arbitrary", "arbitrary"),
            vmem_limit_bytes=_vmem_limit(blocks, _nbytes((tm, d), _BF16))),
        name="mlp",
    )(x, g.reshape(1, d), w1_stack, w2_stack)


def _final_norm_kernel(x_ref, g_ref, o_ref):
    o_ref[...] = _rmsnorm_rows(x_ref[...], g_ref[...])


def _final_norm(x, g, tm):
    m, d = x.shape
    return pl.pallas_call(
        _final_norm_kernel,
        grid=(m // tm,),
        in_specs=[pl.BlockSpec((tm, d), lambda i: (i, 0)), pl.BlockSpec((1, d), lambda i: (0, 0))],
        out_specs=pl.BlockSpec((tm, d), lambda i: (i, 0)),
        out_shape=jax.ShapeDtypeStruct((m, d), _F32),
        compiler_params=pltpu.CompilerParams(
            dimension_semantics=("arbitrary",),
            vmem_limit_bytes=_vmem_limit(2 * _nbytes((tm, d), _F32))),
        name="final_norm",
    )(x, g.reshape(1, d))


def _dilated_kernel(q_ref, k_ref, v_ref, bias_ref, o_ref,
                    qs_ref, ks_ref, vs_ref, qp_ref, kp_ref, vp_ref, acc_ref, m_ref, l_ref, *, group):
    seq = acc_ref.shape[0]
    n_all = seq // BAND
    qs_ref[...] = q_ref[0]
    ks_ref[...] = k_ref[0]
    vs_ref[...] = v_ref[0]
    bias_full = bias_ref[...]
    col = lax.broadcasted_iota(jnp.int32, (BAND, 2 * BAND), 1)
    bias_first = jnp.where(col < BAND, MASKED, bias_full)
    batch_nt = (((2,), (2,)), ((0,), (0,)))
    batch_nn = (((2,), (1,)), ((0,), (0,)))

    for idx, (window, dil) in enumerate(A_BRANCHES):
        assert window // dil == BAND
        length = seq // dil
        nb = length // BAND
        for r in range(dil):
            src_rows = slice(None) if dil == 1 else pl.ds(r, length, stride=dil)
            dst_rows = slice(r * length, (r + 1) * length)
            qp_ref[dst_rows, :] = qs_ref[src_rows, :].astype(_BF16)
            kp_ref[dst_rows, :] = ks_ref[src_rows, :].astype(_BF16)
            vp_ref[dst_rows, :] = vs_ref[src_rows, :].astype(_BF16)

        for g0 in range(0, n_all, group):
            cur = slice(g0 * BAND, (g0 + group) * BAND)
            q = qp_ref[cur, :].reshape(group, BAND, A_DIM)
            k = kp_ref[cur, :].reshape(group, BAND, A_DIM)
            v = vp_ref[cur, :].reshape(group, BAND, A_DIM)
            if nb > 1:
                if g0 == 0:
                    prev_of = lambda ref: jnp.concatenate(
                        [ref[0:BAND, :], ref[0:(group - 1) * BAND, :]], axis=0)
                else:
                    prev_of = lambda ref: ref[(g0 - 1) * BAND:(g0 + group - 1) * BAND, :]
                k = jnp.concatenate([prev_of(kp_ref).reshape(group, BAND, A_DIM), k], axis=1)
                v = jnp.concatenate([prev_of(vp_ref).reshape(group, BAND, A_DIM), v], axis=1)
                biases = [bias_first if (g0 + g) % nb == 0 else bias_full for g in range(group)]
            else:
                biases = [bias_full[:, BAND:]] * group
            s = lax.dot_general(q, k, batch_nt, preferred_element_type=_F32)
            s = jnp.stack([s[g] + biases[g] for g in range(group)])
            m = jnp.max(s, axis=-1, keepdims=True)
            p = jnp.exp(s - m)
            l = jnp.sum(p, axis=-1, keepdims=True)
            num = lax.dot_general(p.astype(_BF16), v, batch_nn, preferred_element_type=_F32)

            for r in range(g0 // nb, -(-(g0 + group) // nb)):
                lo_b = max(g0, r * nb)
                hi_b = min(g0 + group, (r + 1) * nb)
                rows_n = (hi_b - lo_b) * BAND
                start = r + (lo_b - r * nb) * BAND * dil
                rows_o = pl.ds(start, rows_n) if dil == 1 else pl.ds(start, rows_n, stride=dil)
                sel = slice(lo_b - g0, hi_b - g0)
                num_c = num[sel].reshape(rows_n, A_DIM)
                l_c = jnp.broadcast_to(l[sel].reshape(rows_n, 1), (rows_n, LANES))
                m_c = jnp.broadcast_to(m[sel].reshape(rows_n, 1), (rows_n, LANES))
                if idx == 0:
                    acc_ref[rows_o, :] = num_c
                    m_ref[rows_o, :] = m_c
                    l_ref[rows_o, :] = l_c
                else:
                    m_old = m_ref[rows_o, :]
                    m_new = jnp.maximum(m_old, m_c)
                    a = jnp.exp(m_old - m_new)
                    b = jnp.exp(m_c - m_new)
                    acc_ref[rows_o, :] = acc_ref[rows_o, :] * a + num_c * b
                    l_ref[rows_o, :] = l_ref[rows_o, :] * a + l_c * b
                    m_ref[rows_o, :] = m_new

    o_ref[0] = (acc_ref[...] / l_ref[...]).astype(o_ref.dtype)


def _dilated_attention(proj, bias, group):
    bsz, seq, _ = proj.shape
    head = lambda off: pl.BlockSpec((1, seq, A_DIM), lambda b, h, off=off: (b, 0, off + h))
    blocks = 3 * _nbytes((seq, A_DIM), _F32) + _nbytes((seq, A_DIM), _BF16) + _nbytes(bias.shape, _F32)
    scratch = 6 * _nbytes((seq, LANES), _F32) + 3 * _nbytes((seq, LANES), _BF16)
    temporaries = 2 * group * _nbytes((BAND, 2 * BAND), _F32)
    return pl.pallas_call(
        functools.partial(_dilated_kernel, group=group),
        grid=(bsz, A_HEADS),
        in_specs=[head(0), head(A_HEADS), head(2 * A_HEADS),
                  pl.BlockSpec(bias.shape, lambda b, h: (0, 0))],
        out_specs=pl.BlockSpec((1, seq, A_DIM), lambda b, h: (b, 0, h)),
        out_shape=jax.ShapeDtypeStruct((bsz, seq, A_WIDTH), _BF16),
        scratch_shapes=[pltpu.VMEM((seq, A_DIM), _F32) for _ in range(3)]
        + [pltpu.VMEM((seq, A_DIM), _BF16) for _ in range(3)]
        + [pltpu.VMEM((seq, LANES), _F32) for _ in range(3)],
        compiler_params=pltpu.CompilerParams(
            dimension_semantics=("arbitrary", "arbitrary"),
            vmem_limit_bytes=_vmem_limit(blocks, scratch + temporaries)),
        name="dilated",
    )(proj, proj, proj, bias)


_B_LEVELS = (32, 16, 8, 4, 2, 1)


def _hgrn_constants():
    c = B_CHUNK
    t = np.arange(c)[:, None]
    r = np.arange(c)[None, :]
    mats = [(r <= t), (r > t)]
    for h in _B_LEVELS:
        base = (t // (2 * h)) * (2 * h)
        mid = base + h - 1
        upper = (t & h) != 0
        mats.append(np.where(upper, (r > mid) & (r <= t), (r > t) & (r <= mid)))
    sums = np.concatenate(mats, axis=0).astype(np.float32)
    s = np.arange(c)[None, :]
    x = t ^ s
    level = np.where(s > t, -1, np.where(s == t, 0, 1 << (np.floor(np.log2(np.maximum(x, 1))).astype(np.int64))))
    return np.concatenate([sums, sums], axis=1), level.astype(np.int32)


def _hgrn_kernel(q_ref, f_ref, i_ref, g_ref, lb_ref, gain_ref, sums_ref, level_ref, o_ref, *, group):
    seq = q_ref.shape[1]
    c = B_CHUNK
    span = group * c
    lb = lb_ref[0]
    gain = gain_ref[...]
    sums2 = sums_ref[...]
    level = level_ref[...]
    chunks = range(group)

    def piece(x, j):
        return x[j * c:(j + 1) * c]

    def step(n, state_t):
        rows = pl.ds(pl.multiple_of(n * span, span), span)
        gate = lb + (1.0 - lb) * (1.0 / (1.0 + jnp.exp(-f_ref[0, rows, :])))
        log_gate = jnp.log(gate)
        k = 1.0 - gate
        qraw = q_ref[0, rows, :]
        q = qraw * (1.0 / (1.0 + jnp.exp(-qraw))) * (B_DIM ** -0.5)
        v = i_ref[0, rows, :].astype(_BF16)

        g_hi = log_gate.astype(_BF16)
        g_lo = (log_gate - g_hi.astype(_F32)).astype(_BF16)
        hi_lo = [jnp.concatenate([piece(g_hi, j), piece(g_lo, j)], axis=0) for j in chunks]
        expo = []
        for j in range(0, group, 2):
            both = jnp.dot(sums2, jnp.concatenate([hi_lo[j], hi_lo[j + 1]], axis=1),
                           preferred_element_type=_F32)
            expo += [both[:, :B_DIM], both[:, B_DIM:]]
        decay = [jnp.exp(e) for e in expo]

        qc = [piece(q, j) for j in chunks]
        kc = [piece(k, j) for j in chunks]
        vc = [piece(v, j) for j in chunks]
        scaled_q = [[qc[j].astype(_BF16)] + [(qc[j] * piece(decay[j], 2 + i)).astype(_BF16)
                                             for i in range(len(_B_LEVELS))] for j in chunks]
        scaled_k = [[kc[j].astype(_BF16)] + [(kc[j] * piece(decay[j], 2 + i)).astype(_BF16)
                                             for i in range(len(_B_LEVELS))] for j in chunks]
        q_in = [(qc[j] * piece(decay[j], 0)).astype(_BF16) for j in chunks]
        k_out = [(kc[j] * piece(decay[j], 1)).astype(_BF16) for j in chunks]
        parts = [[lax.dot_general(a, b, _NT, preferred_element_type=_F32)
                  for a, b in zip(scaled_q[j], scaled_k[j])] for j in chunks]
        attn = []
        for j in chunks:
            total = jnp.where(level == 0, parts[j][0], 0.0)
            for i, h in enumerate(_B_LEVELS):
                total = total + jnp.where(level == h, parts[j][1 + i], 0.0)
            attn.append(total.astype(_BF16))
        intra = [jnp.dot(attn[j], vc[j], preferred_element_type=_F32) for j in chunks]
        update = [lax.dot_general(vc[j], k_out[j], _TN, preferred_element_type=_F32) for j in chunks]

        outs = []
        for j in chunks:
            outs.append(intra[j] + lax.dot_general(q_in[j], state_t.astype(_BF16), _NT,
                                                   preferred_element_type=_F32))
            state_t = state_t * decay[j][c - 1:c] + update[j]

        y = _rmsnorm_rows(jnp.concatenate(outs, axis=0), gain)
        graw = g_ref[0, rows, :]
        y = y * (graw * (1.0 / (1.0 + jnp.exp(-graw))))
        o_ref[0, rows, :] = y.astype(o_ref.dtype)
        return state_t

    lax.fori_loop(0, seq // span, step, jnp.zeros((B_DIM, B_DIM), _F32))


def _hgrn2(proj, lower_bound, gain, sums, level, group):
    bsz, seq, _ = proj.shape
    assert group % 2 == 0 and seq % (group * B_CHUNK) == 0
    head = lambda off: pl.BlockSpec((1, seq, B_DIM), lambda b, h, off=off: (b, 0, off + h))
    blocks = 4 * _nbytes((seq, B_DIM), _F32) + _nbytes((seq, B_DIM), _BF16) \
        + _nbytes(sums.shape, _BF16) + _nbytes(level.shape, jnp.int32) + 2 * _nbytes((8, LANES), _F32)
    return pl.pallas_call(
        functools.partial(_hgrn_kernel, group=group),
        grid=(bsz, B_HEADS),
        in_specs=[head(0), head(B_HEADS), head(2 * B_HEADS), head(3 * B_HEADS),
                  pl.BlockSpec((1, 1, B_DIM), lambda b, h: (h, 0, 0)),
                  pl.BlockSpec((1, B_DIM), lambda b, h: (0, 0)),
                  pl.BlockSpec(sums.shape, lambda b, h: (0, 0)),
                  pl.BlockSpec(level.shape, lambda b, h: (0, 0))],
        out_specs=pl.BlockSpec((1, seq, B_DIM), lambda b, h: (b, 0, h)),
        out_shape=jax.ShapeDtypeStruct((bsz, seq, B_WIDTH), _BF16),
        compiler_params=pltpu.CompilerParams(
            dimension_semantics=("arbitrary", "arbitrary"), vmem_limit_bytes=_vmem_limit(blocks)),
        name="hgrn2",
    )(proj, proj, proj, proj, lower_bound.reshape(B_HEADS, 1, B_DIM), gain.reshape(1, B_DIM), sums, level)


def _swa_kernel(q_ref, k_ref, v_ref, sink_ref, bias_ref, o_ref, klo_ref, khi_ref, vlo_ref, vhi_ref):
    seq = q_ref.shape[1]
    kv_head = pl.program_id(1)
    lane_half = lax.broadcasted_iota(jnp.int32, (seq, LANES), 1) // C_DIM
    odd_head = (kv_head % 2) == 1

    def place(x):
        own = jnp.where(lane_half == kv_head % 2, x, 0.0)
        other = pltpu.roll(own, C_DIM, 1)
        lo = jnp.where(odd_head, other, own)
        hi = jnp.where(odd_head, own, other)
        return lo.astype(_BF16), hi.astype(_BF16)

    klo_ref[...], khi_ref[...] = place(k_ref[0])
    vlo_ref[...], vhi_ref[...] = place(v_ref[0])
    pairs = C_GROUP // 2

    heads = [(j, side) for side in range(2) for j in range(pairs)]
    sinks = [sink_ref[kv_head * C_GROUP + 2 * j + side] for j, side in heads]
    low_q = lax.broadcasted_iota(jnp.int32, (BAND, LANES), 1) < C_DIM
    k_refs, v_refs = (klo_ref, khi_ref), (vlo_ref, vhi_ref)

    def block(n, with_prev):
        cur = pl.ds(pl.multiple_of(n * BAND, BAND), BAND)
        if with_prev:
            keys = pl.ds(pl.multiple_of((n - 1) * BAND, BAND), 2 * BAND)
            bias = bias_ref[...]
        else:
            keys = cur
            bias = bias_ref[:, BAND:]
        q4 = jnp.concatenate([q_ref[0, cur, j * LANES:(j + 1) * LANES] for j in range(pairs)],
                             axis=0).astype(_BF16)
        s_side = [lax.dot_general(q4, k_refs[side][keys, :], _NT, preferred_element_type=_F32)
                  for side in range(2)]
        s = [s_side[side][j * BAND:(j + 1) * BAND] + bias for j, side in heads]
        m = [jnp.maximum(jnp.max(s_h, axis=-1, keepdims=True), sink) for s_h, sink in zip(s, sinks)]
        p = [jnp.exp(s_h - m_h) for s_h, m_h in zip(s, m)]
        l = [jnp.sum(p_h, axis=-1, keepdims=True) + jnp.exp(sink - m_h) for p_h, m_h, sink in zip(p, m, sinks)]
        num = [jnp.dot(jnp.concatenate(p[side * pairs:(side + 1) * pairs], axis=0).astype(_BF16),
                       v_refs[side][keys, :], preferred_element_type=_F32) for side in range(2)]
        inv = [1.0 / l_h for l_h in l]
        for j in range(pairs):
            rows = slice(j * BAND, (j + 1) * BAND)
            out = jnp.where(low_q, num[0][rows] * inv[j], num[1][rows] * inv[pairs + j])
            o_ref[0, cur, j * LANES:(j + 1) * LANES] = out.astype(o_ref.dtype)

    block(0, False)

    def later(n, carry):
        block(n, True)
        return carry

    lax.fori_loop(1, seq // BAND, later, 0)


def _swa(proj, sinks, bias):
    bsz, seq, _ = proj.shape
    qw = C_GROUP * C_DIM
    k_base = C_Q_HEADS * C_DIM // LANES
    v_base = k_base + C_KV_HEADS * C_DIM // LANES
    blocks = _nbytes((seq, qw), _F32) + 2 * _nbytes((seq, LANES), _F32) + _nbytes((seq, qw), _BF16) \
        + _nbytes(bias.shape, _F32)
    scratch = 4 * _nbytes((seq, LANES), _BF16)
    return pl.pallas_call(
        _swa_kernel,
        grid=(bsz, C_KV_HEADS),
        in_specs=[pl.BlockSpec((1, seq, qw), lambda b, h: (b, 0, h)),
                  pl.BlockSpec((1, seq, LANES), lambda b, h: (b, 0, k_base + h // 2)),
                  pl.BlockSpec((1, seq, LANES), lambda b, h: (b, 0, v_base + h // 2)),
                  pl.BlockSpec(memory_space=pltpu.SMEM),
                  pl.BlockSpec(bias.shape, lambda b, h: (0, 0))],
        out_specs=pl.BlockSpec((1, seq, qw), lambda b, h: (b, 0, h)),
        out_shape=jax.ShapeDtypeStruct((bsz, seq, C_Q_HEADS * C_DIM), _BF16),
        scratch_shapes=[pltpu.VMEM((seq, LANES), _BF16) for _ in range(4)],
        compiler_params=pltpu.CompilerParams(
            dimension_semantics=("arbitrary", "arbitrary"),
            vmem_limit_bytes=_vmem_limit(blocks, scratch)),
        name="swa",
    )(proj, proj, proj, sinks.astype(_F32), bias)


def kernel(x, norm_mix_g, norm_mlp_g, final_norm_g, even_w_in, even_w_out, hgrn_lb_raw, hgrn_norm_g,
           odd_w_qkv, odd_b_qkv, odd_sinks, odd_w_o, odd_b_o, mlp_w1, mlp_w2):
    bsz, seq, d = x.shape
    depth = norm_mix_g.shape[0]
    t = _tiles()
    m = bsz * seq

    tables_a = _rope_tables(seq, A_DIM)
    tables_c = _rope_tables(seq, C_DIM)
    bias_a = jnp.asarray(_band_bias_window(BAND))
    bias_c = jnp.asarray(_band_bias_window(C_WINDOW - 1))
    sums_np, level_np = _hgrn_constants()
    sums, level = jnp.asarray(sums_np, _BF16), jnp.asarray(level_np)

    lb_soft = jax.nn.softmax(hgrn_lb_raw.astype(_F32), axis=0)
    lower_bounds = jnp.cumsum(lb_soft, axis=0) - lb_soft[0:1]

    w_in_a = even_w_in[:, :, :A_IN].astype(_BF16)
    w_in_b = even_w_in[:, :, A_IN:].astype(_BF16)
    w_out = even_w_out.astype(_BF16)
    w_qkv = odd_w_qkv.astype(_BF16)
    w_o = odd_w_o.astype(_BF16)
    w1 = mlp_w1.astype(_BF16)
    w2 = mlp_w2.astype(_BF16)

    group_a = A_WIDTH // LANES
    rope_a = [A_DIM ** -0.5] * group_a + [1.0] * group_a + [None] * group_a
    q_groups = C_Q_HEADS * C_DIM // LANES
    kv_groups = C_KV_HEADS * C_DIM // LANES
    rope_c = [C_DIM ** -0.5] * q_groups + [1.0] * kv_groups + [None] * kv_groups
    zeros_d = jnp.zeros((d,), _F32)

    x2 = x.reshape(m, d)
    for layer in range(depth):
        if layer % 2 == 0:
            e = layer // 2
            proj_a = _norm_proj(x2, norm_mix_g[layer], w_in_a, e, 0, A_IN, jnp.zeros((A_IN,), _F32),
                                tables_a, rope_a, seq, t["proj_tm"]).reshape(bsz, seq, A_IN)
            proj_b = _norm_proj(x2, norm_mix_g[layer], w_in_b, e, 0, B_IN, jnp.zeros((B_IN,), _F32),
                                None, [None] * (B_IN // LANES), seq, t["proj_tm"]).reshape(bsz, seq, B_IN)
            oa = _dilated_attention(proj_a, bias_a, t["dilated_group"]).reshape(m, A_WIDTH)
            ob = _hgrn2(proj_b, lower_bounds[e], hgrn_norm_g[e], sums, level, t["hgrn_group"]).reshape(m, B_WIDTH)
            x2 = _res_matmul([oa, ob], w_out, e, zeros_d, x2, t["out_tm"])
        else:
            o = layer // 2
            proj = _norm_proj(x2, norm_mix_g[layer], w_qkv, o, 0, C_QKV, odd_b_qkv[o],
                              tables_c, rope_c, seq, t["proj_tm"]).reshape(bsz, seq, C_QKV)
            attn = _swa(proj, odd_sinks[o], bias_c).reshape(m, C_Q_HEADS * C_DIM)
            x2 = _res_matmul([attn], w_o, o, odd_b_o[o], x2, t["out_tm"])
        x2 = _mlp(x2, norm_mlp_g[layer], w1, w2, layer, t["mlp_tm"], t["mlp_tf"])
    return _final_norm(x2, final_norm_g, t["norm_tm"]).reshape(bsz, seq, d)
```

```python
import functools
import math

import numpy as np
import jax
import jax.numpy as jnp
from jax import lax
from jax.experimental import pallas as pl
from jax.experimental.pallas import tpu as pltpu

D_MODEL = 2048
NORM_EPS = 1e-5
ROPE_THETA = 500000.0
ROPE_FRACTION = 4
BAND = 128

A_DIM = 128
A_HEADS = 8
A_BRANCHES = ((128, 1), (512, 4), (2048, 16))
A_WIDTH = A_HEADS * A_DIM
B_DIM = 128
B_HEADS = 8
B_WIDTH = B_HEADS * B_DIM
B_CHUNK = 64
A_IN = 3 * A_WIDTH
B_IN = 4 * B_WIDTH

C_DIM = 64
C_Q_HEADS = 32
C_KV_HEADS = 4
C_GROUP = C_Q_HEADS // C_KV_HEADS
C_WINDOW = 128
C_QKV = (C_Q_HEADS + 2 * C_KV_HEADS) * C_DIM
D_FF = 4 * D_MODEL

LANES = 128
V7X_VMEM_BYTES = 64 * 1024 * 1024
COMPILER_TEMP_BYTES = 6 * 1024 * 1024

MASKED = -1e30
LOG2E = math.log2(math.e)

_BF16 = jnp.bfloat16
_F32 = jnp.float32
_NT = (((1,), (1,)), ((), ()))
_TN = (((0,), (0,)), ((), ()))


def _tiles():
    return dict(
        proj_tm=512,
        out_tm=512,
        mlp_tm=1024, mlp_tf=512,
        dilated_group=16,
        hgrn_group=8,
    )


def _vmem_limit(pipelined_bytes, single_bytes=0):
    need = 2 * pipelined_bytes + single_bytes + COMPILER_TEMP_BYTES
    assert need <= V7X_VMEM_BYTES, need
    return int(need)


def _nbytes(shape, dtype):
    return int(np.prod(shape)) * jnp.dtype(dtype).itemsize


def _rmsnorm_rows(x, g):
    ms = jnp.mean(x * x, axis=-1, keepdims=True)
    return x * lax.rsqrt(ms + NORM_EPS) * g


def _sigmoid(x):
    return 1.0 / (1.0 + jnp.exp2(x * (-LOG2E)))


def _rope_tables(seq, head_dim):
    rot = head_dim // ROPE_FRACTION
    half = rot // 2
    inv_freq = 1.0 / (ROPE_THETA ** (jnp.arange(0, rot, 2, dtype=_F32) / rot))
    ang = jnp.arange(seq, dtype=_F32)[:, None] * inv_freq[None, :]
    cos, sin = jnp.cos(ang), jnp.sin(ang)
    pad = head_dim - 2 * half
    ones = jnp.ones((seq, pad), _F32)
    zeros = jnp.zeros((seq, pad), _F32)
    zh = jnp.zeros((seq, half), _F32)
    reps = LANES // head_dim
    cos_full = jnp.tile(jnp.concatenate([cos, cos, ones], axis=1), (1, reps))
    sin_lo = jnp.tile(jnp.concatenate([-sin, zh, zeros], axis=1), (1, reps))
    sin_hi = jnp.tile(jnp.concatenate([zh, sin, zeros], axis=1), (1, reps))
    return cos_full, sin_lo, sin_hi, half


def _rope(x, cos_full, sin_lo, sin_hi, half):
    up = pltpu.roll(x, LANES - half, 1)
    down = pltpu.roll(x, half, 1)
    return x * cos_full + up * sin_lo + down * sin_hi


def _band_bias_window(window):
    i = np.arange(BAND)[:, None]
    j = np.arange(2 * BAND)[None, :]
    dist = i + BAND - j
    return np.where((dist >= 0) & (dist <= window), 0.0, MASKED).astype(np.float32)


def _norm_proj_kernel(*refs, rope_scales, half):
    if half is None:
        x_ref, g_ref, w_ref, b_ref, o_ref = refs
    else:
        x_ref, g_ref, w_ref, b_ref, cos_ref, slo_ref, shi_ref, o_ref = refs
        cos_full, sin_lo, sin_hi = cos_ref[...], slo_ref[...], shi_ref[...]
    h = _rmsnorm_rows(x_ref[...], g_ref[...]).astype(_BF16)
    acc = jnp.dot(h, w_ref[0], preferred_element_type=_F32) + b_ref[...]
    for gi, scale in enumerate(rope_scales):
        sl = slice(gi * LANES, (gi + 1) * LANES)
        if scale is None:
            o_ref[:, sl] = acc[:, sl]
        else:
            o_ref[:, sl] = _rope(acc[:, sl], cos_full, sin_lo, sin_hi, half) * scale


def _norm_proj(x, g, w_stack, layer, col_block, n, b, tables, rope_scales, seq, tm):
    m, d = x.shape
    assert len(rope_scales) * LANES == n
    in_specs = [
        pl.BlockSpec((tm, d), lambda i: (i, 0)),
        pl.BlockSpec((1, d), lambda i: (0, 0)),
        pl.BlockSpec((1, d, n), lambda i: (layer, 0, col_block), pipeline_mode=pl.Buffered(1)),
        pl.BlockSpec((1, n), lambda i: (0, 0)),
    ]
    args = [x, g.reshape(1, d), w_stack, b.reshape(1, n)]
    blocks = _nbytes((tm, d), _F32) + _nbytes((tm, n), _F32) + _nbytes((1, d), _F32) + _nbytes((1, n), _F32)
    half = None
    if tables is not None:
        cos_full, sin_lo, sin_hi, half = tables
        per_seq = seq // tm
        in_specs += [pl.BlockSpec((tm, LANES), lambda i: (i % per_seq, 0))] * 3
        args += [cos_full, sin_lo, sin_hi]
        blocks += 3 * _nbytes((tm, LANES), _F32)
    single = _nbytes((d, n), _BF16) + _nbytes((tm, d), _BF16) + _nbytes((tm, n), _F32)
    return pl.pallas_call(
        functools.partial(_norm_proj_kernel, rope_scales=tuple(rope_scales), half=half),
        grid=(m // tm,),
        in_specs=in_specs,
        out_specs=pl.BlockSpec((tm, n), lambda i: (i, 0)),
        out_shape=jax.ShapeDtypeStruct((m, n), _F32),
        compiler_params=pltpu.CompilerParams(
            dimension_semantics=("arbitrary",), vmem_limit_bytes=_vmem_limit(blocks, single)),
        name="norm_proj",
    )(*args)


def _res_matmul_kernel(*refs, n_in):
    a_refs, w_refs = refs[:n_in], refs[n_in:2 * n_in]
    b_ref, r_ref, g_ref, o_ref, h_ref = refs[2 * n_in:]
    acc = r_ref[...] + b_ref[...]
    for a_ref, w_ref in zip(a_refs, w_refs):
        acc = acc + jnp.dot(a_ref[...], w_ref[0], preferred_element_type=_F32)
    o_ref[...] = acc
    h_ref[...] = _rmsnorm_rows(acc, g_ref[...]).astype(_BF16)


def _res_matmul(a_list, w_stack, layer, b, res, g_next, tm):
    m, n = res.shape
    k = a_list[0].shape[1]
    blocks = (_nbytes((tm, n), _F32) * 2 + _nbytes((tm, n), _BF16) + 2 * _nbytes((1, n), _F32)
              + len(a_list) * _nbytes((tm, k), _BF16))
    in_specs = [pl.BlockSpec((tm, k), lambda i: (i, 0)) for _ in a_list]
    in_specs += [pl.BlockSpec((1, k, n), lambda i, r=r: (layer, r, 0), pipeline_mode=pl.Buffered(1))
                 for r in range(len(a_list))]
    in_specs += [pl.BlockSpec((1, n), lambda i: (0, 0)), pl.BlockSpec((tm, n), lambda i: (i, 0)),
                 pl.BlockSpec((1, n), lambda i: (0, 0))]
    return pl.pallas_call(
        functools.partial(_res_matmul_kernel, n_in=len(a_list)),
        grid=(m // tm,),
        in_specs=in_specs,
        out_specs=[pl.BlockSpec((tm, n), lambda i: (i, 0)), pl.BlockSpec((tm, n), lambda i: (i, 0))],
        out_shape=[jax.ShapeDtypeStruct((m, n), _F32), jax.ShapeDtypeStruct((m, n), _BF16)],
        compiler_params=pltpu.CompilerParams(
            dimension_semantics=("arbitrary",),
            vmem_limit_bytes=_vmem_limit(blocks, len(a_list) * _nbytes((k, n), _BF16))),
        name="res_matmul",
    )(*a_list, *([w_stack] * len(a_list)), b.reshape(1, n), res, g_next.reshape(1, n))


def _mlp_kernel(*refs, final):
    if final:
        x_ref, h_ref, w1_ref, w2_ref, gf_ref, o_ref = refs
    else:
        x_ref, h_ref, w1_ref, w2_ref, o_ref = refs
    j = pl.program_id(1)
    a = jnp.dot(h_ref[...], w1_ref[0], preferred_element_type=_F32)
    a = jnp.square(jnp.maximum(a, 0.0)).astype(_BF16)
    start = jnp.where(j == 0, x_ref[...], o_ref[...])
    o_ref[...] = start + jnp.dot(a, w2_ref[0], preferred_element_type=_F32)
    if final:
        @pl.when(j == pl.num_programs(1) - 1)
        def _():
            o_ref[...] = _rmsnorm_rows(o_ref[...], gf_ref[...])


def _mlp(x, h, w1_stack, w2_stack, layer, final_gain, tm, tf):
    m, d = x.shape
    f = w1_stack.shape[2]
    blocks = (2 * _nbytes((tm, d), _F32) + _nbytes((tm, d), _BF16) + _nbytes((1, d), _F32)
              + _nbytes((d, tf), _BF16) + _nbytes((tf, d), _BF16))
    in_specs = [
        pl.BlockSpec((tm, d), lambda i, j: (i, 0)),
        pl.BlockSpec((tm, d), lambda i, j: (i, 0)),
        pl.BlockSpec((1, d, tf), lambda i, j: (layer, 0, j)),
        pl.BlockSpec((1, tf, d), lambda i, j: (layer, j, 0)),
    ]
    args = [x, h, w1_stack, w2_stack]
    if final_gain is not None:
        in_specs.append(pl.BlockSpec((1, d), lambda i, j: (0, 0)))
        args.append(final_gain.reshape(1, d))
    return pl.pallas_call(
        functools.partial(_mlp_kernel, final=final_gain is not None),
        grid=(m // tm, f // tf),
        in_specs=in_specs,
        out_specs=pl.BlockSpec((tm, d), lambda i, j: (i, 0)),
        out_shape=jax.ShapeDtypeStruct((m, d), _F32),
        compiler_params=pltpu.CompilerParams(
            dimension_semantics=("arbitrary", "arbitrary"), vmem_limit_bytes=_vmem_limit(blocks)),
        name="mlp",
    )(*args)


def _dilated_kernel(q_ref, k_ref, v_ref, bias_ref, o_ref,
                    qs_ref, ks_ref, vs_ref, qp_ref, kp_ref, vp_ref, acc_ref, m_ref, l_ref, *, group):
    seq = acc_ref.shape[0]
    n_all = seq // BAND
    qs_ref[...] = q_ref[0]
    ks_ref[...] = k_ref[0]
    vs_ref[...] = v_ref[0]
    bias_full = bias_ref[...]
    col = lax.broadcasted_iota(jnp.int32, (BAND, 2 * BAND), 1)
    bias_first = jnp.where(col < BAND, MASKED, bias_full)
    batch_nt = (((2,), (2,)), ((0,), (0,)))
    batch_nn = (((2,), (1,)), ((0,), (0,)))

    for idx, (window, dil) in enumerate(A_BRANCHES):
        assert window // dil == BAND
        length = seq // dil
        nb = length // BAND
        for r in range(dil):
            src_rows = slice(None) if dil == 1 else pl.ds(r, length, stride=dil)
            dst_rows = slice(r * length, (r + 1) * length)
            qp_ref[dst_rows, :] = qs_ref[src_rows, :].astype(_BF16)
            kp_ref[dst_rows, :] = ks_ref[src_rows, :].astype(_BF16)
            vp_ref[dst_rows, :] = vs_ref[src_rows, :].astype(_BF16)

        for g0 in range(0, n_all, group):
            cur = slice(g0 * BAND, (g0 + group) * BAND)
            q = qp_ref[cur, :].reshape(group, BAND, A_DIM)
            k = kp_ref[cur, :].reshape(group, BAND, A_DIM)
            v = vp_ref[cur, :].reshape(group, BAND, A_DIM)
            if nb > 1:
                if g0 == 0:
                    prev_of = lambda ref: jnp.concatenate(
                        [ref[0:BAND, :], ref[0:(group - 1) * BAND, :]], axis=0)
                else:
                    prev_of = lambda ref: ref[(g0 - 1) * BAND:(g0 + group - 1) * BAND, :]
                k = jnp.concatenate([prev_of(kp_ref).reshape(group, BAND, A_DIM), k], axis=1)
                v = jnp.concatenate([prev_of(vp_ref).reshape(group, BAND, A_DIM), v], axis=1)
                biases = [bias_first if (g0 + g) % nb == 0 else bias_full for g in range(group)]
            else:
                biases = [bias_full[:, BAND:]] * group
            s = lax.dot_general(q, k, batch_nt, preferred_element_type=_F32)
            s = jnp.stack([s[g] + biases[g] for g in range(group)])
            m = jnp.max(s, axis=-1, keepdims=True)
            p = jnp.exp2(s - m)
            l = jnp.sum(p, axis=-1, keepdims=True)
            num = lax.dot_general(p.astype(_BF16), v, batch_nn, preferred_element_type=_F32)

            for r in range(g0 // nb, -(-(g0 + group) // nb)):
                lo_b = max(g0, r * nb)
                hi_b = min(g0 + group, (r + 1) * nb)
                rows_n = (hi_b - lo_b) * BAND
                start = r + (lo_b - r * nb) * BAND * dil
                rows_o = pl.ds(start, rows_n) if dil == 1 else pl.ds(start, rows_n, stride=dil)
                sel = slice(lo_b - g0, hi_b - g0)
                num_c = num[sel].reshape(rows_n, A_DIM)
                l_c = jnp.broadcast_to(l[sel].reshape(rows_n, 1), (rows_n, LANES))
                m_c = jnp.broadcast_to(m[sel].reshape(rows_n, 1), (rows_n, LANES))
                if idx == 0:
                    acc_ref[rows_o, :] = num_c
                    m_ref[rows_o, :] = m_c
                    l_ref[rows_o, :] = l_c
                else:
                    m_old = m_ref[rows_o, :]
                    m_new = jnp.maximum(m_old, m_c)
                    a = jnp.exp2(m_old - m_new)
                    b = jnp.exp2(m_c - m_new)
                    acc_ref[rows_o, :] = acc_ref[rows_o, :] * a + num_c * b
                    l_ref[rows_o, :] = l_ref[rows_o, :] * a + l_c * b
                    m_ref[rows_o, :] = m_new

    o_ref[0] = (acc_ref[...] / l_ref[...]).astype(o_ref.dtype)


def _dilated_attention(proj, bias, group):
    bsz, seq, _ = proj.shape
    head = lambda off: pl.BlockSpec((1, seq, A_DIM), lambda b, h, off=off: (b, 0, off + h))
    blocks = 3 * _nbytes((seq, A_DIM), _F32) + _nbytes((seq, A_DIM), _BF16) + _nbytes(bias.shape, _F32)
    scratch = 6 * _nbytes((seq, LANES), _F32) + 3 * _nbytes((seq, LANES), _BF16)
    temporaries = 2 * group * _nbytes((BAND, 2 * BAND), _F32)
    return pl.pallas_call(
        functools.partial(_dilated_kernel, group=group),
        grid=(bsz, A_HEADS),
        in_specs=[head(0), head(A_HEADS), head(2 * A_HEADS),
                  pl.BlockSpec(bias.shape, lambda b, h: (0, 0))],
        out_specs=pl.BlockSpec((1, seq, A_DIM), lambda b, h: (b, 0, h)),
        out_shape=jax.ShapeDtypeStruct((bsz, seq, A_WIDTH), _BF16),
        scratch_shapes=[pltpu.VMEM((seq, A_DIM), _F32) for _ in range(3)]
        + [pltpu.VMEM((seq, A_DIM), _BF16) for _ in range(3)]
        + [pltpu.VMEM((seq, LANES), _F32) for _ in range(3)],
        compiler_params=pltpu.CompilerParams(
            dimension_semantics=("arbitrary", "arbitrary"),
            vmem_limit_bytes=_vmem_limit(blocks, scratch + temporaries)),
        name="dilated",
    )(proj, proj, proj, bias)


_B_LEVELS = (32, 16, 8, 4, 2, 1)


def _hgrn_constants():
    c = B_CHUNK
    t = np.arange(c)[:, None]
    r = np.arange(c)[None, :]
    mats = [(r <= t), (r > t)]
    for h in _B_LEVELS:
        base = (t // (2 * h)) * (2 * h)
        mid = base + h - 1
        upper = (t & h) != 0
        mats.append(np.where(upper, (r > mid) & (r <= t), (r > t) & (r <= mid)))
    sums = np.concatenate(mats, axis=0).astype(np.float32)
    s = np.arange(c)[None, :]
    x = t ^ s
    level = np.where(s > t, -1, np.where(s == t, 0, 1 << (np.floor(np.log2(np.maximum(x, 1))).astype(np.int64))))
    return np.concatenate([sums, sums], axis=1), level.astype(np.int32)


def _hgrn_kernel(q_ref, f_ref, i_ref, g_ref, lb_ref, gain_ref, sums_ref, level_ref, o_ref, *, group):
    seq = q_ref.shape[1]
    c = B_CHUNK
    span = group * c
    lb = lb_ref[0]
    gain = gain_ref[...]
    sums2 = sums_ref[...]
    level = level_ref[...]
    chunks = range(group)

    def piece(x, j):
        return x[j * c:(j + 1) * c]

    def step(n, state_t):
        rows = pl.ds(pl.multiple_of(n * span, span), span)
        gate = lb + (1.0 - lb) * _sigmoid(f_ref[0, rows, :])
        log_gate = jnp.log2(gate)
        k = 1.0 - gate
        qraw = q_ref[0, rows, :]
        q = qraw * _sigmoid(qraw) * (B_DIM ** -0.5)
        v = i_ref[0, rows, :].astype(_BF16)

        g_hi = log_gate.astype(_BF16)
        g_lo = (log_gate - g_hi.astype(_F32)).astype(_BF16)
        hi_lo = [jnp.concatenate([piece(g_hi, j), piece(g_lo, j)], axis=0) for j in chunks]
        expo = []
        for j in range(0, group, 2):
            both = jnp.dot(sums2, jnp.concatenate([hi_lo[j], hi_lo[j + 1]], axis=1),
                           preferred_element_type=_F32)
            expo += [both[:, :B_DIM], both[:, B_DIM:]]
        decay = [jnp.exp2(e) for e in expo]

        qc = [piece(q, j) for j in chunks]
        kc = [piece(k, j) for j in chunks]
        vc = [piece(v, j) for j in chunks]
        scaled_q = [[qc[j].astype(_BF16)] + [(qc[j] * piece(decay[j], 2 + i)).astype(_BF16)
                                             for i in range(len(_B_LEVELS))] for j in chunks]
        scaled_k = [[kc[j].astype(_BF16)] + [(kc[j] * piece(decay[j], 2 + i)).astype(_BF16)
                                             for i in range(len(_B_LEVELS))] for j in chunks]
        q_in = [(qc[j] * piece(decay[j], 0)).astype(_BF16) for j in chunks]
        k_out = [(kc[j] * piece(decay[j], 1)).astype(_BF16) for j in chunks]
        parts = [[lax.dot_general(a, b, _NT, preferred_element_type=_F32)
                  for a, b in zip(scaled_q[j], scaled_k[j])] for j in chunks]
        attn = []
        for j in chunks:
            total = jnp.where(level == 0, parts[j][0], 0.0)
            for i, h in enumerate(_B_LEVELS):
                total = total + jnp.where(level == h, parts[j][1 + i], 0.0)
            attn.append(total.astype(_BF16))
        intra = [jnp.dot(attn[j], vc[j], preferred_element_type=_F32) for j in chunks]
        update = [lax.dot_general(vc[j], k_out[j], _TN, preferred_element_type=_F32) for j in chunks]

        outs = []
        for j in chunks:
            outs.append(intra[j] + lax.dot_general(q_in[j], state_t.astype(_BF16), _NT,
                                                   preferred_element_type=_F32))
            state_t = state_t * decay[j][c - 1:c] + update[j]

        y = _rmsnorm_rows(jnp.concatenate(outs, axis=0), gain)
        graw = g_ref[0, rows, :]
        y = y * (graw * _sigmoid(graw))
        o_ref[0, rows, :] = y.astype(o_ref.dtype)
        return state_t

    lax.fori_loop(0, seq // span, step, jnp.zeros((B_DIM, B_DIM), _F32))


def _hgrn2(proj, lower_bound, gain, sums, level, group):
    bsz, seq, _ = proj.shape
    assert group % 2 == 0 and seq % (group * B_CHUNK) == 0
    head = lambda off: pl.BlockSpec((1, seq, B_DIM), lambda b, h, off=off: (b, 0, off + h))
    blocks = 4 * _nbytes((seq, B_DIM), _F32) + _nbytes((seq, B_DIM), _BF16) \
        + _nbytes(sums.shape, _BF16) + _nbytes(level.shape, jnp.int32) + 2 * _nbytes((8, LANES), _F32)
    return pl.pallas_call(
        functools.partial(_hgrn_kernel, group=group),
        grid=(bsz, B_HEADS),
        in_specs=[head(0), head(B_HEADS), head(2 * B_HEADS), head(3 * B_HEADS),
                  pl.BlockSpec((1, 1, B_DIM), lambda b, h: (h, 0, 0)),
                  pl.BlockSpec((1, B_DIM), lambda b, h: (0, 0)),
                  pl.BlockSpec(sums.shape, lambda b, h: (0, 0)),
                  pl.BlockSpec(level.shape, lambda b, h: (0, 0))],
        out_specs=pl.BlockSpec((1, seq, B_DIM), lambda b, h: (b, 0, h)),
        out_shape=jax.ShapeDtypeStruct((bsz, seq, B_WIDTH), _BF16),
        compiler_params=pltpu.CompilerParams(
            dimension_semantics=("arbitrary", "arbitrary"), vmem_limit_bytes=_vmem_limit(blocks)),
        name="hgrn2",
    )(proj, proj, proj, proj, lower_bound.reshape(B_HEADS, 1, B_DIM), gain.reshape(1, B_DIM), sums, level)


def _swa_kernel(q_ref, k_ref, v_ref, sink_ref, bias_ref, o_ref, klo_ref, khi_ref, vlo_ref, vhi_ref):
    seq = q_ref.shape[1]
    kv_head = pl.program_id(1)
    lane_half = lax.broadcasted_iota(jnp.int32, (seq, LANES), 1) // C_DIM
    odd_head = (kv_head % 2) == 1

    def place(x):
        own = jnp.where(lane_half == kv_head % 2, x, 0.0)
        other = pltpu.roll(own, C_DIM, 1)
        lo = jnp.where(odd_head, other, own)
        hi = jnp.where(odd_head, own, other)
        return lo.astype(_BF16), hi.astype(_BF16)

    klo_ref[...], khi_ref[...] = place(k_ref[0])
    vlo_ref[...], vhi_ref[...] = place(v_ref[0])
    pairs = C_GROUP // 2

    heads = [(j, side) for side in range(2) for j in range(pairs)]
    sinks = [sink_ref[kv_head * C_GROUP + 2 * j + side] * LOG2E for j, side in heads]
    low_q = lax.broadcasted_iota(jnp.int32, (BAND, LANES), 1) < C_DIM
    k_refs, v_refs = (klo_ref, khi_ref), (vlo_ref, vhi_ref)

    def block(n, with_prev):
        cur = pl.ds(pl.multiple_of(n * BAND, BAND), BAND)
        if with_prev:
            keys = pl.ds(pl.multiple_of((n - 1) * BAND, BAND), 2 * BAND)
            bias = bias_ref[...]
        else:
            keys = cur
            bias = bias_ref[:, BAND:]
        q4 = jnp.concatenate([q_ref[0, cur, j * LANES:(j + 1) * LANES] for j in range(pairs)],
                             axis=0).astype(_BF16)
        s_side = [lax.dot_general(q4, k_refs[side][keys, :], _NT, preferred_element_type=_F32)
                  for side in range(2)]
        s = [s_side[side][j * BAND:(j + 1) * BAND] + bias for j, side in heads]
        m = [jnp.maximum(jnp.max(s_h, axis=-1, keepdims=True), sink) for s_h, sink in zip(s, sinks)]
        p = [jnp.exp2(s_h - m_h) for s_h, m_h in zip(s, m)]
        l = [jnp.sum(p_h, axis=-1, keepdims=True) + jnp.exp2(sink - m_h) for p_h, m_h, sink in zip(p, m, sinks)]
        num = [jnp.dot(jnp.concatenate(p[side * pairs:(side + 1) * pairs], axis=0).astype(_BF16),
                       v_refs[side][keys, :], preferred_element_type=_F32) for side in range(2)]
        inv = [1.0 / l_h for l_h in l]
        for j in range(pairs):
            rows = slice(j * BAND, (j + 1) * BAND)
            out = jnp.where(low_q, num[0][rows] * inv[j], num[1][rows] * inv[pairs + j])
            o_ref[0, cur, j * LANES:(j + 1) * LANES] = out.astype(o_ref.dtype)

    block(0, False)

    def later(n, carry):
        block(n, True)
        return carry

    lax.fori_loop(1, seq // BAND, later, 0)


def _swa(proj, sinks, bias):
    bsz, seq, _ = proj.shape
    qw = C_GROUP * C_DIM
    k_base = C_Q_HEADS * C_DIM // LANES
    v_base = k_base + C_KV_HEADS * C_DIM // LANES
    blocks = _nbytes((seq, qw), _F32) + 2 * _nbytes((seq, LANES), _F32) + _nbytes((seq, qw), _BF16) \
        + _nbytes(bias.shape, _F32)
    scratch = 4 * _nbytes((seq, LANES), _BF16)
    return pl.pallas_call(
        _swa_kernel,
        grid=(bsz, C_KV_HEADS),
        in_specs=[pl.BlockSpec((1, seq, qw), lambda b, h: (b, 0, h)),
                  pl.BlockSpec((1, seq, LANES), lambda b, h: (b, 0, k_base + h // 2)),
                  pl.BlockSpec((1, seq, LANES), lambda b, h: (b, 0, v_base + h // 2)),
                  pl.BlockSpec(memory_space=pltpu.SMEM),
                  pl.BlockSpec(bias.shape, lambda b, h: (0, 0))],
        out_specs=pl.BlockSpec((1, seq, qw), lambda b, h: (b, 0, h)),
        out_shape=jax.ShapeDtypeStruct((bsz, seq, C_Q_HEADS * C_DIM), _BF16),
        scratch_shapes=[pltpu.VMEM((seq, LANES), _BF16) for _ in range(4)],
        compiler_params=pltpu.CompilerParams(
            dimension_semantics=("arbitrary", "arbitrary"),
            vmem_limit_bytes=_vmem_limit(blocks, scratch)),
        name="swa",
    )(proj, proj, proj, sinks.astype(_F32), bias)


def kernel(x, norm_mix_g, norm_mlp_g, final_norm_g, even_w_in, even_w_out, hgrn_lb_raw, hgrn_norm_g,
           odd_w_qkv, odd_b_qkv, odd_sinks, odd_w_o, odd_b_o, mlp_w1, mlp_w2):
    bsz, seq, d = x.shape
    depth = norm_mix_g.shape[0]
    t = _tiles()
    m = bsz * seq

    tables_a = _rope_tables(seq, A_DIM)
    tables_c = _rope_tables(seq, C_DIM)
    bias_a = jnp.asarray(_band_bias_window(BAND))
    bias_c = jnp.asarray(_band_bias_window(C_WINDOW - 1))
    sums_np, level_np = _hgrn_constants()
    sums, level = jnp.asarray(sums_np, _BF16), jnp.asarray(level_np)

    lb_soft = jax.nn.softmax(hgrn_lb_raw.astype(_F32), axis=0)
    lower_bounds = jnp.cumsum(lb_soft, axis=0) - lb_soft[0:1]

    w_in_a = even_w_in[:, :, :A_IN].astype(_BF16)
    w_in_b = even_w_in[:, :, A_IN:].astype(_BF16)
    w_out = even_w_out.astype(_BF16)
    w_qkv = odd_w_qkv.astype(_BF16)
    w_o = odd_w_o.astype(_BF16)
    w1 = mlp_w1.astype(_BF16)
    w2 = mlp_w2.astype(_BF16)

    group_a = A_WIDTH // LANES
    rope_a = [A_DIM ** -0.5 * LOG2E] * group_a + [1.0] * group_a + [None] * group_a
    q_groups = C_Q_HEADS * C_DIM // LANES
    kv_groups = C_KV_HEADS * C_DIM // LANES
    rope_c = [C_DIM ** -0.5 * LOG2E] * q_groups + [1.0] * kv_groups + [None] * kv_groups
    zeros_d = jnp.zeros((d,), _F32)

    x2 = x.reshape(m, d)
    for layer in range(depth):
        if layer % 2 == 0:
            e = layer // 2
            proj_a = _norm_proj(x2, norm_mix_g[layer], w_in_a, e, 0, A_IN, jnp.zeros((A_IN,), _F32),
                                tables_a, rope_a, seq, t["proj_tm"]).reshape(bsz, seq, A_IN)
            proj_b = _norm_proj(x2, norm_mix_g[layer], w_in_b, e, 0, B_IN, jnp.zeros((B_IN,), _F32),
                                None, [None] * (B_IN // LANES), seq, t["proj_tm"]).reshape(bsz, seq, B_IN)
            oa = _dilated_attention(proj_a, bias_a, t["dilated_group"]).reshape(m, A_WIDTH)
            ob = _hgrn2(proj_b, lower_bounds[e], hgrn_norm_g[e], sums, level, t["hgrn_group"]).reshape(m, B_WIDTH)
            x2, h2 = _res_matmul([oa, ob], w_out, e, zeros_d, x2, norm_mlp_g[layer], t["out_tm"])
        else:
            o = layer // 2
            proj = _norm_proj(x2, norm_mix_g[layer], w_qkv, o, 0, C_QKV, odd_b_qkv[o],
                              tables_c, rope_c, seq, t["proj_tm"]).reshape(bsz, seq, C_QKV)
            attn = _swa(proj, odd_sinks[o], bias_c).reshape(m, C_Q_HEADS * C_DIM)
            x2, h2 = _res_matmul([attn], w_o, o, odd_b_o[o], x2, norm_mlp_g[layer], t["out_tm"])
        last = layer == depth - 1
        x2 = _mlp(x2, h2, w1, w2, layer, final_norm_g if last else None, t["mlp_tm"], t["mlp_tf"])
    return x2.reshape(bsz, seq, d)
```

```python
import functools
import math

import numpy as np
import jax
import jax.numpy as jnp
from jax import lax
from jax.experimental import pallas as pl
from jax.experimental.pallas import tpu as pltpu

D_MODEL = 2048
NORM_EPS = 1e-5
ROPE_THETA = 500000.0
ROPE_FRACTION = 4
BAND = 128

A_DIM = 128
A_HEADS = 8
A_BRANCHES = ((128, 1), (512, 4), (2048, 16))
A_WIDTH = A_HEADS * A_DIM
B_DIM = 128
B_HEADS = 8
B_WIDTH = B_HEADS * B_DIM
B_CHUNK = 64
A_IN = 3 * A_WIDTH
B_IN = 4 * B_WIDTH

C_DIM = 64
C_Q_HEADS = 32
C_KV_HEADS = 4
C_GROUP = C_Q_HEADS // C_KV_HEADS
C_WINDOW = 128
C_QKV = (C_Q_HEADS + 2 * C_KV_HEADS) * C_DIM
D_FF = 4 * D_MODEL

LANES = 128
V7X_VMEM_BYTES = 64 * 1024 * 1024
COMPILER_TEMP_BYTES = 6 * 1024 * 1024

MASKED = -1e30
LOG2E = math.log2(math.e)

_BF16 = jnp.bfloat16
_F32 = jnp.float32
_NT = (((1,), (1,)), ((), ()))
_TN = (((0,), (0,)), ((), ()))


def _tiles():
    return dict(
        proj_tm=512,
        out_tm=512,
        mlp_tm=1024, mlp_tf=1024,
        dilated_group=16,
        hgrn_group=16,
    )


def _vmem_limit(pipelined_bytes, single_bytes=0):
    need = 2 * pipelined_bytes + single_bytes + COMPILER_TEMP_BYTES
    assert need <= V7X_VMEM_BYTES, need
    return int(need)


def _nbytes(shape, dtype):
    return int(np.prod(shape)) * jnp.dtype(dtype).itemsize


def _rmsnorm_rows(x, g):
    ms = jnp.mean(x * x, axis=-1, keepdims=True)
    return x * lax.rsqrt(ms + NORM_EPS) * g


def _sigmoid(x):
    return 1.0 / (1.0 + jnp.exp2(x * (-LOG2E)))


def _rope_tables(seq, head_dim):
    rot = head_dim // ROPE_FRACTION
    half = rot // 2
    inv_freq = 1.0 / (ROPE_THETA ** (jnp.arange(0, rot, 2, dtype=_F32) / rot))
    ang = jnp.arange(seq, dtype=_F32)[:, None] * inv_freq[None, :]
    cos, sin = jnp.cos(ang), jnp.sin(ang)
    pad = head_dim - 2 * half
    ones = jnp.ones((seq, pad), _F32)
    zeros = jnp.zeros((seq, pad), _F32)
    zh = jnp.zeros((seq, half), _F32)
    reps = LANES // head_dim
    cos_full = jnp.tile(jnp.concatenate([cos, cos, ones], axis=1), (1, reps))
    sin_lo = jnp.tile(jnp.concatenate([-sin, zh, zeros], axis=1), (1, reps))
    sin_hi = jnp.tile(jnp.concatenate([zh, sin, zeros], axis=1), (1, reps))
    return cos_full, sin_lo, sin_hi, half


def _rope(x, cos_full, sin_lo, sin_hi, half):
    up = pltpu.roll(x, LANES - half, 1)
    down = pltpu.roll(x, half, 1)
    return x * cos_full + up * sin_lo + down * sin_hi


def _band_bias_window(window):
    i = np.arange(BAND)[:, None]
    j = np.arange(2 * BAND)[None, :]
    dist = i + BAND - j
    return np.where((dist >= 0) & (dist <= window), 0.0, MASKED).astype(np.float32)


def _norm_proj_kernel(*refs, rope_scales, half):
    if half is None:
        x_ref, g_ref, w_ref, b_ref, o_ref = refs
    else:
        x_ref, g_ref, w_ref, b_ref, cos_ref, slo_ref, shi_ref, o_ref = refs
        cos_full, sin_lo, sin_hi = cos_ref[...], slo_ref[...], shi_ref[...]
    h = _rmsnorm_rows(x_ref[...], g_ref[...]).astype(_BF16)
    acc = jnp.dot(h, w_ref[0], preferred_element_type=_F32) + b_ref[...]
    for gi, scale in enumerate(rope_scales):
        sl = slice(gi * LANES, (gi + 1) * LANES)
        if scale is None:
            o_ref[:, sl] = acc[:, sl]
        else:
            o_ref[:, sl] = _rope(acc[:, sl], cos_full, sin_lo, sin_hi, half) * scale


def _norm_proj(x, g, w_stack, layer, col_block, n, b, tables, rope_scales, seq, tm):
    m, d = x.shape
    assert len(rope_scales) * LANES == n
    in_specs = [
        pl.BlockSpec((tm, d), lambda i: (i, 0)),
        pl.BlockSpec((1, d), lambda i: (0, 0)),
        pl.BlockSpec((1, d, n), lambda i: (layer, 0, col_block), pipeline_mode=pl.Buffered(1)),
        pl.BlockSpec((1, n), lambda i: (0, 0)),
    ]
    args = [x, g.reshape(1, d), w_stack, b.reshape(1, n)]
    blocks = _nbytes((tm, d), _F32) + _nbytes((tm, n), _F32) + _nbytes((1, d), _F32) + _nbytes((1, n), _F32)
    half = None
    if tables is not None:
        cos_full, sin_lo, sin_hi, half = tables
        per_seq = seq // tm
        in_specs += [pl.BlockSpec((tm, LANES), lambda i: (i % per_seq, 0))] * 3
        args += [cos_full, sin_lo, sin_hi]
        blocks += 3 * _nbytes((tm, LANES), _F32)
    single = _nbytes((d, n), _BF16) + _nbytes((tm, d), _BF16) + _nbytes((tm, n), _F32)
    return pl.pallas_call(
        functools.partial(_norm_proj_kernel, rope_scales=tuple(rope_scales), half=half),
        grid=(m // tm,),
        in_specs=in_specs,
        out_specs=pl.BlockSpec((tm, n), lambda i: (i, 0)),
        out_shape=jax.ShapeDtypeStruct((m, n), _F32),
        compiler_params=pltpu.CompilerParams(
            dimension_semantics=("arbitrary",), vmem_limit_bytes=_vmem_limit(blocks, single)),
        name="norm_proj",
    )(*args)


def _res_matmul_kernel(*refs, n_in):
    a_refs, w_refs = refs[:n_in], refs[n_in:2 * n_in]
    b_ref, r_ref, g_ref, o_ref, h_ref = refs[2 * n_in:]
    acc = r_ref[...] + b_ref[...]
    for a_ref, w_ref in zip(a_refs, w_refs):
        acc = acc + jnp.dot(a_ref[...], w_ref[0], preferred_element_type=_F32)
    o_ref[...] = acc
    h_ref[...] = _rmsnorm_rows(acc, g_ref[...]).astype(_BF16)


def _res_matmul(a_list, w_stack, layer, b, res, g_next, tm):
    m, n = res.shape
    k = a_list[0].shape[1]
    blocks = (_nbytes((tm, n), _F32) * 2 + _nbytes((tm, n), _BF16) + 2 * _nbytes((1, n), _F32)
              + len(a_list) * _nbytes((tm, k), _BF16))
    in_specs = [pl.BlockSpec((tm, k), lambda i: (i, 0)) for _ in a_list]
    in_specs += [pl.BlockSpec((1, k, n), lambda i, r=r: (layer, r, 0), pipeline_mode=pl.Buffered(1))
                 for r in range(len(a_list))]
    in_specs += [pl.BlockSpec((1, n), lambda i: (0, 0)), pl.BlockSpec((tm, n), lambda i: (i, 0)),
                 pl.BlockSpec((1, n), lambda i: (0, 0))]
    return pl.pallas_call(
        functools.partial(_res_matmul_kernel, n_in=len(a_list)),
        grid=(m // tm,),
        in_specs=in_specs,
        out_specs=[pl.BlockSpec((tm, n), lambda i: (i, 0)), pl.BlockSpec((tm, n), lambda i: (i, 0))],
        out_shape=[jax.ShapeDtypeStruct((m, n), _F32), jax.ShapeDtypeStruct((m, n), _BF16)],
        compiler_params=pltpu.CompilerParams(
            dimension_semantics=("arbitrary",),
            vmem_limit_bytes=_vmem_limit(blocks, len(a_list) * _nbytes((k, n), _BF16))),
        name="res_matmul",
    )(*a_list, *([w_stack] * len(a_list)), b.reshape(1, n), res, g_next.reshape(1, n))


def _mlp_kernel(*refs, final):
    if final:
        x_ref, h_ref, w1_ref, w2_ref, gf_ref, o_ref = refs
    else:
        x_ref, h_ref, w1_ref, w2_ref, o_ref = refs
    j = pl.program_id(1)
    a = jnp.dot(h_ref[...], w1_ref[0], preferred_element_type=_F32)
    a = jnp.square(jnp.maximum(a, 0.0)).astype(_BF16)
    so_far = jnp.where(j == 0, 0.0, o_ref[...])
    o_ref[...] = so_far + jnp.dot(a, w2_ref[0], preferred_element_type=_F32)
    piece = x_ref.shape[0]
    rows = pl.ds(pl.multiple_of(j * piece, piece), piece)
    o_ref[rows, :] += x_ref[...]
    if final:
        @pl.when(j == pl.num_programs(1) - 1)
        def _():
            o_ref[...] = _rmsnorm_rows(o_ref[...], gf_ref[...])


def _mlp(x, h, w1_stack, w2_stack, layer, final_gain, tm, tf):
    m, d = x.shape
    f = w1_stack.shape[2]
    n_tiles = f // tf
    piece = tm // n_tiles
    blocks = (_nbytes((tm, d), _F32) + _nbytes((piece, d), _F32) + _nbytes((tm, d), _BF16) + _nbytes((1, d), _F32)
              + _nbytes((d, tf), _BF16) + _nbytes((tf, d), _BF16))
    temporaries = _nbytes((tm, tf), _F32) + _nbytes((tm, tf), _BF16)
    in_specs = [
        pl.BlockSpec((piece, d), lambda i, j: (i * n_tiles + j, 0)),
        pl.BlockSpec((tm, d), lambda i, j: (i, 0)),
        pl.BlockSpec((1, d, tf), lambda i, j: (layer, 0, j)),
        pl.BlockSpec((1, tf, d), lambda i, j: (layer, j, 0)),
    ]
    args = [x, h, w1_stack, w2_stack]
    if final_gain is not None:
        in_specs.append(pl.BlockSpec((1, d), lambda i, j: (0, 0)))
        args.append(final_gain.reshape(1, d))
    return pl.pallas_call(
        functools.partial(_mlp_kernel, final=final_gain is not None),
        grid=(m // tm, f // tf),
        in_specs=in_specs,
        out_specs=pl.BlockSpec((tm, d), lambda i, j: (i, 0)),
        out_shape=jax.ShapeDtypeStruct((m, d), _F32),
        compiler_params=pltpu.CompilerParams(
            dimension_semantics=("arbitrary", "arbitrary"),
            vmem_limit_bytes=_vmem_limit(blocks, temporaries)),
        name="mlp",
    )(*args)


def _dilated_kernel(q_ref, k_ref, v_ref, bias_ref, o_ref,
                    q4_ref, k4_ref, v4_ref, qp_ref, kp_ref, vp_ref,
                    acc_ref, m_ref, l_ref, acc4_ref, m4_ref, l4_ref, *, group):
    seq = acc_ref.shape[0]
    n_all = seq // BAND
    quarter = seq // 4
    bias_full = bias_ref[...]
    col = lax.broadcasted_iota(jnp.int32, (BAND, 2 * BAND), 1)
    bias_first = jnp.where(col < BAND, MASKED, bias_full)
    batch_nt = (((2,), (2,)), ((0,), (0,)))
    batch_nn = (((2,), (1,)), ((0,), (0,)))

    def by4_rows(c):
        return pl.ds(c, quarter, stride=4)

    for c in range(4):
        seg = slice(c * quarter, (c + 1) * quarter)
        q4_ref[seg, :] = q_ref[0, by4_rows(c), :]
        k4_ref[seg, :] = k_ref[0, by4_rows(c), :]
        v4_ref[seg, :] = v_ref[0, by4_rows(c), :]

    def fold(refs, rows, num_c, m_c, l_c, first):
        acc_r, m_r, l_r = refs
        if first:
            acc_r[rows, :] = num_c
            m_r[rows, :] = m_c
            l_r[rows, :] = l_c
        else:
            m_old = m_r[rows, :]
            m_new = jnp.maximum(m_old, m_c)
            a = jnp.exp2(m_old - m_new)
            b = jnp.exp2(m_c - m_new)
            acc_r[rows, :] = acc_r[rows, :] * a + num_c * b
            l_r[rows, :] = l_r[rows, :] * a + l_c * b
            m_r[rows, :] = m_new

    for window, dil in A_BRANCHES:
        assert window // dil == BAND and dil in (1, 4, 16)
        length = seq // dil
        nb = length // BAND
        if dil == 1:
            sources = (q_ref.at[0], k_ref.at[0], v_ref.at[0])
            runs = [(slice(None), slice(None))]
        elif dil == 4:
            sources = (q4_ref, k4_ref, v4_ref)
            runs = [(slice(None), slice(None))]
        else:
            sources = (q4_ref, k4_ref, v4_ref)
            runs = [(pl.ds((r % 4) * quarter + r // 4, length, stride=4), slice(r * length, (r + 1) * length))
                    for r in range(dil)]
        for src_rows, dst_rows in runs:
            for src, dst in zip(sources, (qp_ref, kp_ref, vp_ref)):
                dst[dst_rows, :] = src[src_rows, :].astype(_BF16)

        for g0 in range(0, n_all, group):
            cur = slice(g0 * BAND, (g0 + group) * BAND)
            q = qp_ref[cur, :].reshape(group, BAND, A_DIM)
            k = kp_ref[cur, :].reshape(group, BAND, A_DIM)
            v = vp_ref[cur, :].reshape(group, BAND, A_DIM)
            if nb > 1:
                if g0 == 0:
                    prev_of = lambda ref: jnp.concatenate(
                        [ref[0:BAND, :], ref[0:(group - 1) * BAND, :]], axis=0)
                else:
                    prev_of = lambda ref: ref[(g0 - 1) * BAND:(g0 + group - 1) * BAND, :]
                k = jnp.concatenate([prev_of(kp_ref).reshape(group, BAND, A_DIM), k], axis=1)
                v = jnp.concatenate([prev_of(vp_ref).reshape(group, BAND, A_DIM), v], axis=1)
                biases = [bias_first if (g0 + g) % nb == 0 else bias_full for g in range(group)]
            else:
                biases = [bias_full[:, BAND:]] * group
            s = lax.dot_general(q, k, batch_nt, preferred_element_type=_F32)
            s = jnp.stack([s[g] + biases[g] for g in range(group)])
            m = jnp.max(s, axis=-1, keepdims=True)
            p = jnp.exp2(s - m)
            l = jnp.sum(p, axis=-1, keepdims=True)
            num = lax.dot_general(p.astype(_BF16), v, batch_nn, preferred_element_type=_F32)

            def stats(sel):
                rows_n = (sel.stop - sel.start) * BAND
                return (num[sel].reshape(rows_n, A_DIM),
                        jnp.broadcast_to(m[sel].reshape(rows_n, 1), (rows_n, LANES)),
                        jnp.broadcast_to(l[sel].reshape(rows_n, 1), (rows_n, LANES)))

            if dil == 1:
                fold((acc_ref, m_ref, l_ref), cur, *stats(slice(0, group)), first=True)
            elif dil == 4:
                fold((acc4_ref, m4_ref, l4_ref), cur, *stats(slice(0, group)), first=True)
            else:
                for g in range(group):
                    r = g0 + g
                    rows = pl.ds((r % 4) * quarter + r // 4, length, stride=4)
                    fold((acc4_ref, m4_ref, l4_ref), rows, *stats(slice(g, g + 1)), first=False)

    for c in range(4):
        seg = slice(c * quarter, (c + 1) * quarter)
        fold((acc_ref, m_ref, l_ref), by4_rows(c), acc4_ref[seg, :], m4_ref[seg, :], l4_ref[seg, :], first=False)
    o_ref[0] = (acc_ref[...] / l_ref[...]).astype(o_ref.dtype)


def _dilated_attention(proj, bias, group):
    bsz, seq, _ = proj.shape
    head = lambda off: pl.BlockSpec((1, seq, A_DIM), lambda b, h, off=off: (b, 0, off + h))
    blocks = 3 * _nbytes((seq, A_DIM), _F32) + _nbytes((seq, A_DIM), _BF16) + _nbytes(bias.shape, _F32)
    scratch = 9 * _nbytes((seq, LANES), _F32) + 3 * _nbytes((seq, LANES), _BF16)
    temporaries = 2 * group * _nbytes((BAND, 2 * BAND), _F32)
    return pl.pallas_call(
        functools.partial(_dilated_kernel, group=group),
        grid=(bsz, A_HEADS),
        in_specs=[head(0), head(A_HEADS), head(2 * A_HEADS),
                  pl.BlockSpec(bias.shape, lambda b, h: (0, 0))],
        out_specs=pl.BlockSpec((1, seq, A_DIM), lambda b, h: (b, 0, h)),
        out_shape=jax.ShapeDtypeStruct((bsz, seq, A_WIDTH), _BF16),
        scratch_shapes=[pltpu.VMEM((seq, A_DIM), _F32) for _ in range(3)]
        + [pltpu.VMEM((seq, A_DIM), _BF16) for _ in range(3)]
        + [pltpu.VMEM((seq, LANES), _F32) for _ in range(6)],
        compiler_params=pltpu.CompilerParams(
            dimension_semantics=("arbitrary", "arbitrary"),
            vmem_limit_bytes=_vmem_limit(blocks, scratch + temporaries)),
        name="dilated",
    )(proj, proj, proj, bias)


_B_LEVELS = (32, 16, 8, 4, 2, 1)


def _hgrn_constants():
    c = B_CHUNK
    t = np.arange(c)[:, None]
    r = np.arange(c)[None, :]
    mats = [(r <= t), (r > t)]
    for h in _B_LEVELS:
        base = (t // (2 * h)) * (2 * h)
        mid = base + h - 1
        upper = (t & h) != 0
        mats.append(np.where(upper, (r > mid) & (r <= t), (r > t) & (r <= mid)))
    sums = np.concatenate(mats, axis=0).astype(np.float32)
    s = np.arange(c)[None, :]
    x = t ^ s
    level = np.where(s > t, -1, np.where(s == t, 0, 1 << (np.floor(np.log2(np.maximum(x, 1))).astype(np.int64))))
    return np.concatenate([sums, sums], axis=1), level.astype(np.int32)


def _hgrn_kernel(q_ref, f_ref, i_ref, g_ref, lb_ref, gain_ref, sums_ref, level_ref, o_ref, *, group):
    seq = q_ref.shape[1]
    c = B_CHUNK
    span = group * c
    lb = lb_ref[0]
    gain = gain_ref[...]
    sums2 = sums_ref[...]
    level = level_ref[...]
    chunks = range(group)

    def piece(x, j):
        return x[j * c:(j + 1) * c]

    def step(n, state_t):
        rows = pl.ds(pl.multiple_of(n * span, span), span)
        gate = lb + (1.0 - lb) * _sigmoid(f_ref[0, rows, :])
        log_gate = jnp.log2(gate)
        k = 1.0 - gate
        qraw = q_ref[0, rows, :]
        q = qraw * _sigmoid(qraw) * (B_DIM ** -0.5)
        v = i_ref[0, rows, :].astype(_BF16)

        g_hi = log_gate.astype(_BF16)
        g_lo = (log_gate - g_hi.astype(_F32)).astype(_BF16)
        hi_lo = [jnp.concatenate([piece(g_hi, j), piece(g_lo, j)], axis=0) for j in chunks]
        expo = []
        for j in range(0, group, 2):
            both = jnp.dot(sums2, jnp.concatenate([hi_lo[j], hi_lo[j + 1]], axis=1),
                           preferred_element_type=_F32)
            expo += [both[:, :B_DIM], both[:, B_DIM:]]
        decay = [jnp.exp2(e) for e in expo]

        qc = [piece(q, j) for j in chunks]
        kc = [piece(k, j) for j in chunks]
        vc = [piece(v, j) for j in chunks]
        scaled_q = [[qc[j].astype(_BF16)] + [(qc[j] * piece(decay[j], 2 + i)).astype(_BF16)
                                             for i in range(len(_B_LEVELS))] for j in chunks]
        scaled_k = [[kc[j].astype(_BF16)] + [(kc[j] * piece(decay[j], 2 + i)).astype(_BF16)
                                             for i in range(len(_B_LEVELS))] for j in chunks]
        q_in = [(qc[j] * piece(decay[j], 0)).astype(_BF16) for j in chunks]
        k_out = [(kc[j] * piece(decay[j], 1)).astype(_BF16) for j in chunks]
        parts = [[lax.dot_general(a, b, _NT, preferred_element_type=_F32)
                  for a, b in zip(scaled_q[j], scaled_k[j])] for j in chunks]
        attn = []
        for j in chunks:
            total = jnp.where(level == 0, parts[j][0], 0.0)
            for i, h in enumerate(_B_LEVELS):
                total = total + jnp.where(level == h, parts[j][1 + i], 0.0)
            attn.append(total.astype(_BF16))
        intra = [jnp.dot(attn[j], vc[j], preferred_element_type=_F32) for j in chunks]
        update = [lax.dot_general(vc[j], k_out[j], _TN, preferred_element_type=_F32) for j in chunks]

        outs = []
        for j in chunks:
            outs.append(intra[j] + lax.dot_general(q_in[j], state_t.astype(_BF16), _NT,
                                                   preferred_element_type=_F32))
            state_t = state_t * decay[j][c - 1:c] + update[j]

        y = _rmsnorm_rows(jnp.concatenate(outs, axis=0), gain)
        graw = g_ref[0, rows, :]
        y = y * (graw * _sigmoid(graw))
        o_ref[0, rows, :] = y.astype(o_ref.dtype)
        return state_t

    lax.fori_loop(0, seq // span, step, jnp.zeros((B_DIM, B_DIM), _F32))


def _hgrn2(proj, lower_bound, gain, sums, level, group):
    bsz, seq, _ = proj.shape
    assert group % 2 == 0 and seq % (group * B_CHUNK) == 0
    head = lambda off: pl.BlockSpec((1, seq, B_DIM), lambda b, h, off=off: (b, 0, off + h))
    blocks = 4 * _nbytes((seq, B_DIM), _F32) + _nbytes((seq, B_DIM), _BF16) \
        + _nbytes(sums.shape, _BF16) + _nbytes(level.shape, jnp.int32) + 2 * _nbytes((8, LANES), _F32)
    return pl.pallas_call(
        functools.partial(_hgrn_kernel, group=group),
        grid=(bsz, B_HEADS),
        in_specs=[head(0), head(B_HEADS), head(2 * B_HEADS), head(3 * B_HEADS),
                  pl.BlockSpec((1, 1, B_DIM), lambda b, h: (h, 0, 0)),
                  pl.BlockSpec((1, B_DIM), lambda b, h: (0, 0)),
                  pl.BlockSpec(sums.shape, lambda b, h: (0, 0)),
                  pl.BlockSpec(level.shape, lambda b, h: (0, 0))],
        out_specs=pl.BlockSpec((1, seq, B_DIM), lambda b, h: (b, 0, h)),
        out_shape=jax.ShapeDtypeStruct((bsz, seq, B_WIDTH), _BF16),
        compiler_params=pltpu.CompilerParams(
            dimension_semantics=("arbitrary", "arbitrary"), vmem_limit_bytes=_vmem_limit(blocks)),
        name="hgrn2",
    )(proj, proj, proj, proj, lower_bound.reshape(B_HEADS, 1, B_DIM), gain.reshape(1, B_DIM), sums, level)


def _swa_kernel(q_ref, k_ref, v_ref, sink_ref, bias_ref, o_ref, klo_ref, khi_ref, vlo_ref, vhi_ref):
    seq = q_ref.shape[1]
    kv_head = pl.program_id(1)
    lane_half = lax.broadcasted_iota(jnp.int32, (seq, LANES), 1) // C_DIM
    odd_head = (kv_head % 2) == 1

    def place(x):
        own = jnp.where(lane_half == kv_head % 2, x, 0.0)
        other = pltpu.roll(own, C_DIM, 1)
        lo = jnp.where(odd_head, other, own)
        hi = jnp.where(odd_head, own, other)
        return lo.astype(_BF16), hi.astype(_BF16)

    klo_ref[...], khi_ref[...] = place(k_ref[0])
    vlo_ref[...], vhi_ref[...] = place(v_ref[0])
    pairs = C_GROUP // 2

    heads = [(j, side) for side in range(2) for j in range(pairs)]
    sinks = [sink_ref[kv_head * C_GROUP + 2 * j + side] * LOG2E for j, side in heads]
    low_q = lax.broadcasted_iota(jnp.int32, (BAND, LANES), 1) < C_DIM
    k_refs, v_refs = (klo_ref, khi_ref), (vlo_ref, vhi_ref)

    def block(n, with_prev):
        cur = pl.ds(pl.multiple_of(n * BAND, BAND), BAND)
        if with_prev:
            keys = pl.ds(pl.multiple_of((n - 1) * BAND, BAND), 2 * BAND)
            bias = bias_ref[...]
        else:
            keys = cur
            bias = bias_ref[:, BAND:]
        q4 = jnp.concatenate([q_ref[0, cur, j * LANES:(j + 1) * LANES] for j in range(pairs)],
                             axis=0).astype(_BF16)
        s_side = [lax.dot_general(q4, k_refs[side][keys, :], _NT, preferred_element_type=_F32)
                  for side in range(2)]
        s = [s_side[side][j * BAND:(j + 1) * BAND] + bias for j, side in heads]
        m = [jnp.maximum(jnp.max(s_h, axis=-1, keepdims=True), sink) for s_h, sink in zip(s, sinks)]
        p = [jnp.exp2(s_h - m_h) for s_h, m_h in zip(s, m)]
        l = [jnp.sum(p_h, axis=-1, keepdims=True) + jnp.exp2(sink - m_h) for p_h, m_h, sink in zip(p, m, sinks)]
        num = [jnp.dot(jnp.concatenate(p[side * pairs:(side + 1) * pairs], axis=0).astype(_BF16),
                       v_refs[side][keys, :], preferred_element_type=_F32) for side in range(2)]
        inv = [1.0 / l_h for l_h in l]
        for j in range(pairs):
            rows = slice(j * BAND, (j + 1) * BAND)
            out = jnp.where(low_q, num[0][rows] * inv[j], num[1][rows] * inv[pairs + j])
            o_ref[0, cur, j * LANES:(j + 1) * LANES] = out.astype(o_ref.dtype)

    block(0, False)

    def later(n, carry):
        block(n, True)
        return carry

    lax.fori_loop(1, seq // BAND, later, 0)


def _swa(proj, sinks, bias):
    bsz, seq, _ = proj.shape
    qw = C_GROUP * C_DIM
    k_base = C_Q_HEADS * C_DIM // LANES
    v_base = k_base + C_KV_HEADS * C_DIM // LANES
    blocks = _nbytes((seq, qw), _F32) + 2 * _nbytes((seq, LANES), _F32) + _nbytes((seq, qw), _BF16) \
        + _nbytes(bias.shape, _F32)
    scratch = 4 * _nbytes((seq, LANES), _BF16)
    return pl.pallas_call(
        _swa_kernel,
        grid=(bsz, C_KV_HEADS),
        in_specs=[pl.BlockSpec((1, seq, qw), lambda b, h: (b, 0, h)),
                  pl.BlockSpec((1, seq, LANES), lambda b, h: (b, 0, k_base + h // 2)),
                  pl.BlockSpec((1, seq, LANES), lambda b, h: (b, 0, v_base + h // 2)),
                  pl.BlockSpec(memory_space=pltpu.SMEM),
                  pl.BlockSpec(bias.shape, lambda b, h: (0, 0))],
        out_specs=pl.BlockSpec((1, seq, qw), lambda b, h: (b, 0, h)),
        out_shape=jax.ShapeDtypeStruct((bsz, seq, C_Q_HEADS * C_DIM), _BF16),
        scratch_shapes=[pltpu.VMEM((seq, LANES), _BF16) for _ in range(4)],
        compiler_params=pltpu.CompilerParams(
            dimension_semantics=("arbitrary", "arbitrary"),
            vmem_limit_bytes=_vmem_limit(blocks, scratch)),
        name="swa",
    )(proj, proj, proj, sinks.astype(_F32), bias)


def kernel(x, norm_mix_g, norm_mlp_g, final_norm_g, even_w_in, even_w_out, hgrn_lb_raw, hgrn_norm_g,
           odd_w_qkv, odd_b_qkv, odd_sinks, odd_w_o, odd_b_o, mlp_w1, mlp_w2):
    bsz, seq, d = x.shape
    depth = norm_mix_g.shape[0]
    t = _tiles()
    m = bsz * seq

    tables_a = _rope_tables(seq, A_DIM)
    tables_c = _rope_tables(seq, C_DIM)
    bias_a = jnp.asarray(_band_bias_window(BAND))
    bias_c = jnp.asarray(_band_bias_window(C_WINDOW - 1))
    sums_np, level_np = _hgrn_constants()
    sums, level = jnp.asarray(sums_np, _BF16), jnp.asarray(level_np)

    lb_soft = jax.nn.softmax(hgrn_lb_raw.astype(_F32), axis=0)
    lower_bounds = jnp.cumsum(lb_soft, axis=0) - lb_soft[0:1]

    w_in_a = even_w_in[:, :, :A_IN].astype(_BF16)
    w_in_b = even_w_in[:, :, A_IN:].astype(_BF16)
    w_out = even_w_out.astype(_BF16)
    w_qkv = odd_w_qkv.astype(_BF16)
    w_o = odd_w_o.astype(_BF16)
    w1 = mlp_w1.astype(_BF16)
    w2 = mlp_w2.astype(_BF16)

    group_a = A_WIDTH // LANES
    rope_a = [A_DIM ** -0.5 * LOG2E] * group_a + [1.0] * group_a + [None] * group_a
    q_groups = C_Q_HEADS * C_DIM // LANES
    kv_groups = C_KV_HEADS * C_DIM // LANES
    rope_c = [C_DIM ** -0.5 * LOG2E] * q_groups + [1.0] * kv_groups + [None] * kv_groups
    zeros_d = jnp.zeros((d,), _F32)

    x2 = x.reshape(m, d)
    for layer in range(depth):
        if layer % 2 == 0:
            e = layer // 2
            proj_a = _norm_proj(x2, norm_mix_g[layer], w_in_a, e, 0, A_IN, jnp.zeros((A_IN,), _F32),
                                tables_a, rope_a, seq, t["proj_tm"]).reshape(bsz, seq, A_IN)
            proj_b = _norm_proj(x2, norm_mix_g[layer], w_in_b, e, 0, B_IN, jnp.zeros((B_IN,), _F32),
                                None, [None] * (B_IN // LANES), seq, t["proj_tm"]).reshape(bsz, seq, B_IN)
            oa = _dilated_attention(proj_a, bias_a, t["dilated_group"]).reshape(m, A_WIDTH)
            ob = _hgrn2(proj_b, lower_bounds[e], hgrn_norm_g[e], sums, level, t["hgrn_group"]).reshape(m, B_WIDTH)
            x2, h2 = _res_matmul([oa, ob], w_out, e, zeros_d, x2, norm_mlp_g[layer], t["out_tm"])
        else:
            o = layer // 2
            proj = _norm_proj(x2, norm_mix_g[layer], w_qkv, o, 0, C_QKV, odd_b_qkv[o],
                              tables_c, rope_c, seq, t["proj_tm"]).reshape(bsz, seq, C_QKV)
            attn = _swa(proj, odd_sinks[o], bias_c).reshape(m, C_Q_HEADS * C_DIM)
            x2, h2 = _res_matmul([attn], w_o, o, odd_b_o[o], x2, norm_mlp_g[layer], t["out_tm"])
        last = layer == depth - 1
        x2 = _mlp(x2, h2, w1, w2, layer, final_norm_g if last else None, t["mlp_tm"], t["mlp_tf"])
    return x2.reshape(bsz, seq, d)
```

```python
import functools
import math

import numpy as np
import jax
import jax.numpy as jnp
from jax import lax
from jax.experimental import pallas as pl
from jax.experimental.pallas import tpu as pltpu

D_MODEL = 2048
NORM_EPS = 1e-5
ROPE_THETA = 500000.0
ROPE_FRACTION = 4
BAND = 128

A_DIM = 128
A_HEADS = 8
A_BRANCHES = ((128, 1), (512, 4), (2048, 16))
A_WIDTH = A_HEADS * A_DIM
B_DIM = 128
B_HEADS = 8
B_WIDTH = B_HEADS * B_DIM
B_CHUNK = 64
A_IN = 3 * A_WIDTH
B_IN = 4 * B_WIDTH

C_DIM = 64
C_Q_HEADS = 32
C_KV_HEADS = 4
C_GROUP = C_Q_HEADS // C_KV_HEADS
C_WINDOW = 128
C_QKV = (C_Q_HEADS + 2 * C_KV_HEADS) * C_DIM
D_FF = 4 * D_MODEL

LANES = 128
V7X_VMEM_BYTES = 64 * 1024 * 1024
COMPILER_TEMP_BYTES = 6 * 1024 * 1024

MASKED = -1e30
LOG2E = math.log2(math.e)

_BF16 = jnp.bfloat16
_F32 = jnp.float32
_NT = (((1,), (1,)), ((), ()))
_TN = (((0,), (0,)), ((), ()))


def _tiles():
    return dict(
        proj_tm=512,
        out_tm=512,
        mlp_tm=1024, mlp_tf=1024,
        dilated_group=16,
        hgrn_group=16,
        swa_joint=3,
    )


def _vmem_limit(pipelined_bytes, single_bytes=0):
    need = 2 * pipelined_bytes + single_bytes + COMPILER_TEMP_BYTES
    assert need <= V7X_VMEM_BYTES, need
    return int(need)


def _nbytes(shape, dtype):
    return int(np.prod(shape)) * jnp.dtype(dtype).itemsize


def _rmsnorm_rows(x, g):
    ms = jnp.mean(x * x, axis=-1, keepdims=True)
    return x * lax.rsqrt(ms + NORM_EPS) * g


def _sigmoid(x):
    return 1.0 / (1.0 + jnp.exp2(x * (-LOG2E)))


def _rope_tables(seq, head_dim):
    rot = head_dim // ROPE_FRACTION
    half = rot // 2
    inv_freq = 1.0 / (ROPE_THETA ** (jnp.arange(0, rot, 2, dtype=_F32) / rot))
    ang = jnp.arange(seq, dtype=_F32)[:, None] * inv_freq[None, :]
    cos, sin = jnp.cos(ang), jnp.sin(ang)
    pad = head_dim - 2 * half
    ones = jnp.ones((seq, pad), _F32)
    zeros = jnp.zeros((seq, pad), _F32)
    zh = jnp.zeros((seq, half), _F32)
    reps = LANES // head_dim
    cos_full = jnp.tile(jnp.concatenate([cos, cos, ones], axis=1), (1, reps))
    sin_lo = jnp.tile(jnp.concatenate([-sin, zh, zeros], axis=1), (1, reps))
    sin_hi = jnp.tile(jnp.concatenate([zh, sin, zeros], axis=1), (1, reps))
    return cos_full, sin_lo, sin_hi, half


def _rope(x, cos_full, sin_lo, sin_hi, half):
    up = pltpu.roll(x, LANES - half, 1)
    down = pltpu.roll(x, half, 1)
    return x * cos_full + up * sin_lo + down * sin_hi


def _band_bias_window(window):
    i = np.arange(BAND)[:, None]
    j = np.arange(2 * BAND)[None, :]
    dist = i + BAND - j
    return np.where((dist >= 0) & (dist <= window), 0.0, MASKED).astype(np.float32)


def _norm_proj_kernel(*refs, rope_scales, half):
    if half is None:
        x_ref, g_ref, w_ref, b_ref, o_ref = refs
    else:
        x_ref, g_ref, w_ref, b_ref, cos_ref, slo_ref, shi_ref, o_ref = refs
        cos_full, sin_lo, sin_hi = cos_ref[...], slo_ref[...], shi_ref[...]
    h = _rmsnorm_rows(x_ref[...], g_ref[...]).astype(_BF16)
    acc = jnp.dot(h, w_ref[0], preferred_element_type=_F32) + b_ref[...]
    for gi, scale in enumerate(rope_scales):
        sl = slice(gi * LANES, (gi + 1) * LANES)
        if scale is None:
            o_ref[:, sl] = acc[:, sl]
        else:
            o_ref[:, sl] = _rope(acc[:, sl], cos_full, sin_lo, sin_hi, half) * scale


def _norm_proj(x, g, w_stack, layer, col_block, n, b, tables, rope_scales, seq, tm):
    m, d = x.shape
    assert len(rope_scales) * LANES == n
    in_specs = [
        pl.BlockSpec((tm, d), lambda i: (i, 0)),
        pl.BlockSpec((1, d), lambda i: (0, 0)),
        pl.BlockSpec((1, d, n), lambda i: (layer, 0, col_block), pipeline_mode=pl.Buffered(1)),
        pl.BlockSpec((1, n), lambda i: (0, 0)),
    ]
    args = [x, g.reshape(1, d), w_stack, b.reshape(1, n)]
    blocks = _nbytes((tm, d), _F32) + _nbytes((tm, n), _F32) + _nbytes((1, d), _F32) + _nbytes((1, n), _F32)
    half = None
    if tables is not None:
        cos_full, sin_lo, sin_hi, half = tables
        per_seq = seq // tm
        in_specs += [pl.BlockSpec((tm, LANES), lambda i: (i % per_seq, 0))] * 3
        args += [cos_full, sin_lo, sin_hi]
        blocks += 3 * _nbytes((tm, LANES), _F32)
    single = _nbytes((d, n), _BF16) + _nbytes((tm, d), _BF16) + _nbytes((tm, n), _F32)
    return pl.pallas_call(
        functools.partial(_norm_proj_kernel, rope_scales=tuple(rope_scales), half=half),
        grid=(m // tm,),
        in_specs=in_specs,
        out_specs=pl.BlockSpec((tm, n), lambda i: (i, 0)),
        out_shape=jax.ShapeDtypeStruct((m, n), _F32),
        compiler_params=pltpu.CompilerParams(
            dimension_semantics=("arbitrary",), vmem_limit_bytes=_vmem_limit(blocks, single)),
        name="norm_proj",
    )(*args)


def _res_matmul_kernel(*refs, n_in):
    a_refs, w_refs = refs[:n_in], refs[n_in:2 * n_in]
    b_ref, r_ref, g_ref, o_ref, h_ref = refs[2 * n_in:]
    acc = r_ref[...] + b_ref[...]
    for a_ref, w_ref in zip(a_refs, w_refs):
        acc = acc + jnp.dot(a_ref[...], w_ref[0], preferred_element_type=_F32)
    o_ref[...] = acc
    h_ref[...] = _rmsnorm_rows(acc, g_ref[...]).astype(_BF16)


def _res_matmul(a_list, w_stack, layer, b, res, g_next, tm):
    m, n = res.shape
    k = a_list[0].shape[1]
    blocks = (_nbytes((tm, n), _F32) * 2 + _nbytes((tm, n), _BF16) + 2 * _nbytes((1, n), _F32)
              + len(a_list) * _nbytes((tm, k), _BF16))
    in_specs = [pl.BlockSpec((tm, k), lambda i: (i, 0)) for _ in a_list]
    in_specs += [pl.BlockSpec((1, k, n), lambda i, r=r: (layer, r, 0), pipeline_mode=pl.Buffered(1))
                 for r in range(len(a_list))]
    in_specs += [pl.BlockSpec((1, n), lambda i: (0, 0)), pl.BlockSpec((tm, n), lambda i: (i, 0)),
                 pl.BlockSpec((1, n), lambda i: (0, 0))]
    return pl.pallas_call(
        functools.partial(_res_matmul_kernel, n_in=len(a_list)),
        grid=(m // tm,),
        in_specs=in_specs,
        out_specs=[pl.BlockSpec((tm, n), lambda i: (i, 0)), pl.BlockSpec((tm, n), lambda i: (i, 0))],
        out_shape=[jax.ShapeDtypeStruct((m, n), _F32), jax.ShapeDtypeStruct((m, n), _BF16)],
        compiler_params=pltpu.CompilerParams(
            dimension_semantics=("arbitrary",),
            vmem_limit_bytes=_vmem_limit(blocks, len(a_list) * _nbytes((k, n), _BF16))),
        name="res_matmul",
    )(*a_list, *([w_stack] * len(a_list)), b.reshape(1, n), res, g_next.reshape(1, n))


def _mlp_kernel(*refs, final):
    if final:
        x_ref, h_ref, w1_ref, w2_ref, gf_ref, o_ref = refs
    else:
        x_ref, h_ref, w1_ref, w2_ref, o_ref = refs
    j = pl.program_id(1)
    a = jnp.dot(h_ref[...], w1_ref[0], preferred_element_type=_F32)
    a = jnp.square(jnp.maximum(a, 0.0)).astype(_BF16)
    so_far = jnp.where(j == 0, 0.0, o_ref[...])
    o_ref[...] = so_far + jnp.dot(a, w2_ref[0], preferred_element_type=_F32)
    piece = x_ref.shape[0]
    rows = pl.ds(pl.multiple_of(j * piece, piece), piece)
    o_ref[rows, :] += x_ref[...]
    if final:
        @pl.when(j == pl.num_programs(1) - 1)
        def _():
            o_ref[...] = _rmsnorm_rows(o_ref[...], gf_ref[...])


def _mlp(x, h, w1_stack, w2_stack, layer, final_gain, tm, tf):
    m, d = x.shape
    f = w1_stack.shape[2]
    n_tiles = f // tf
    piece = tm // n_tiles
    blocks = (_nbytes((tm, d), _F32) + _nbytes((piece, d), _F32) + _nbytes((tm, d), _BF16) + _nbytes((1, d), _F32)
              + _nbytes((d, tf), _BF16) + _nbytes((tf, d), _BF16))
    temporaries = _nbytes((tm, tf), _F32) + _nbytes((tm, tf), _BF16)
    in_specs = [
        pl.BlockSpec((piece, d), lambda i, j: (i * n_tiles + j, 0)),
        pl.BlockSpec((tm, d), lambda i, j: (i, 0)),
        pl.BlockSpec((1, d, tf), lambda i, j: (layer, 0, j)),
        pl.BlockSpec((1, tf, d), lambda i, j: (layer, j, 0)),
    ]
    args = [x, h, w1_stack, w2_stack]
    if final_gain is not None:
        in_specs.append(pl.BlockSpec((1, d), lambda i, j: (0, 0)))
        args.append(final_gain.reshape(1, d))
    return pl.pallas_call(
        functools.partial(_mlp_kernel, final=final_gain is not None),
        grid=(m // tm, f // tf),
        in_specs=in_specs,
        out_specs=pl.BlockSpec((tm, d), lambda i, j: (i, 0)),
        out_shape=jax.ShapeDtypeStruct((m, d), _F32),
        compiler_params=pltpu.CompilerParams(
            dimension_semantics=("arbitrary", "arbitrary"),
            vmem_limit_bytes=_vmem_limit(blocks, temporaries)),
        name="mlp",
    )(*args)


def _dilated_kernel(q_ref, k_ref, v_ref, bias_ref, o_ref,
                    q4_ref, k4_ref, v4_ref, qp_ref, kp_ref, vp_ref,
                    acc_ref, m_ref, l_ref, acc4_ref, m4_ref, l4_ref, *, group):
    seq = acc_ref.shape[0]
    n_all = seq // BAND
    quarter = seq // 4
    bias_full = bias_ref[...]
    col = lax.broadcasted_iota(jnp.int32, (BAND, 2 * BAND), 1)
    bias_first = jnp.where(col < BAND, MASKED, bias_full)
    batch_nt = (((2,), (2,)), ((0,), (0,)))
    batch_nn = (((2,), (1,)), ((0,), (0,)))

    def by4_rows(c):
        return pl.ds(c, quarter, stride=4)

    for c in range(4):
        seg = slice(c * quarter, (c + 1) * quarter)
        q4_ref[seg, :] = q_ref[0, by4_rows(c), :]
        k4_ref[seg, :] = k_ref[0, by4_rows(c), :]
        v4_ref[seg, :] = v_ref[0, by4_rows(c), :]

    def fold(refs, rows, num_c, m_c, l_c, first):
        acc_r, m_r, l_r = refs
        if first:
            acc_r[rows, :] = num_c
            m_r[rows, :] = m_c
            l_r[rows, :] = l_c
        else:
            m_old = m_r[rows, :]
            m_new = jnp.maximum(m_old, m_c)
            a = jnp.exp2(m_old - m_new)
            b = jnp.exp2(m_c - m_new)
            acc_r[rows, :] = acc_r[rows, :] * a + num_c * b
            l_r[rows, :] = l_r[rows, :] * a + l_c * b
            m_r[rows, :] = m_new

    for window, dil in A_BRANCHES:
        assert window // dil == BAND and dil in (1, 4, 16)
        length = seq // dil
        nb = length // BAND
        if dil == 1:
            sources = (q_ref.at[0], k_ref.at[0], v_ref.at[0])
            runs = [(slice(None), slice(None))]
        elif dil == 4:
            sources = (q4_ref, k4_ref, v4_ref)
            runs = [(slice(None), slice(None))]
        else:
            sources = (q4_ref, k4_ref, v4_ref)
            runs = [(pl.ds((r % 4) * quarter + r // 4, length, stride=4), slice(r * length, (r + 1) * length))
                    for r in range(dil)]
        for src_rows, dst_rows in runs:
            for src, dst in zip(sources, (qp_ref, kp_ref, vp_ref)):
                dst[dst_rows, :] = src[src_rows, :].astype(_BF16)

        for g0 in range(0, n_all, group):
            cur = slice(g0 * BAND, (g0 + group) * BAND)
            q = qp_ref[cur, :].reshape(group, BAND, A_DIM)
            k = kp_ref[cur, :].reshape(group, BAND, A_DIM)
            v = vp_ref[cur, :].reshape(group, BAND, A_DIM)
            if nb > 1:
                if g0 == 0:
                    prev_of = lambda ref: jnp.concatenate(
                        [ref[0:BAND, :], ref[0:(group - 1) * BAND, :]], axis=0)
                else:
                    prev_of = lambda ref: ref[(g0 - 1) * BAND:(g0 + group - 1) * BAND, :]
                k = jnp.concatenate([prev_of(kp_ref).reshape(group, BAND, A_DIM), k], axis=1)
                v = jnp.concatenate([prev_of(vp_ref).reshape(group, BAND, A_DIM), v], axis=1)
                biases = [bias_first if (g0 + g) % nb == 0 else bias_full for g in range(group)]
            else:
                biases = [bias_full[:, BAND:]] * group
            s = lax.dot_general(q, k, batch_nt, preferred_element_type=_F32)
            s = jnp.stack([s[g] + biases[g] for g in range(group)])
            m = jnp.max(s, axis=-1, keepdims=True)
            p = jnp.exp2(s - m)
            l = jnp.sum(p, axis=-1, keepdims=True)
            num = lax.dot_general(p.astype(_BF16), v, batch_nn, preferred_element_type=_F32)

            def stats(sel):
                rows_n = (sel.stop - sel.start) * BAND
                return (num[sel].reshape(rows_n, A_DIM),
                        jnp.broadcast_to(m[sel].reshape(rows_n, 1), (rows_n, LANES)),
                        jnp.broadcast_to(l[sel].reshape(rows_n, 1), (rows_n, LANES)))

            if dil == 1:
                fold((acc_ref, m_ref, l_ref), cur, *stats(slice(0, group)), first=True)
            elif dil == 4:
                fold((acc4_ref, m4_ref, l4_ref), cur, *stats(slice(0, group)), first=True)
            else:
                for g in range(group):
                    r = g0 + g
                    rows = pl.ds((r % 4) * quarter + r // 4, length, stride=4)
                    fold((acc4_ref, m4_ref, l4_ref), rows, *stats(slice(g, g + 1)), first=False)

    for c in range(4):
        seg = slice(c * quarter, (c + 1) * quarter)
        fold((acc_ref, m_ref, l_ref), by4_rows(c), acc4_ref[seg, :], m4_ref[seg, :], l4_ref[seg, :], first=False)
    o_ref[0] = (acc_ref[...] / l_ref[...]).astype(o_ref.dtype)


def _dilated_attention(proj, bias, group):
    bsz, seq, _ = proj.shape
    head = lambda off: pl.BlockSpec((1, seq, A_DIM), lambda b, h, off=off: (b, 0, off + h))
    blocks = 3 * _nbytes((seq, A_DIM), _F32) + _nbytes((seq, A_DIM), _BF16) + _nbytes(bias.shape, _F32)
    scratch = 9 * _nbytes((seq, LANES), _F32) + 3 * _nbytes((seq, LANES), _BF16)
    temporaries = 2 * group * _nbytes((BAND, 2 * BAND), _F32)
    return pl.pallas_call(
        functools.partial(_dilated_kernel, group=group),
        grid=(bsz, A_HEADS),
        in_specs=[head(0), head(A_HEADS), head(2 * A_HEADS),
                  pl.BlockSpec(bias.shape, lambda b, h: (0, 0))],
        out_specs=pl.BlockSpec((1, seq, A_DIM), lambda b, h: (b, 0, h)),
        out_shape=jax.ShapeDtypeStruct((bsz, seq, A_WIDTH), _BF16),
        scratch_shapes=[pltpu.VMEM((seq, A_DIM), _F32) for _ in range(3)]
        + [pltpu.VMEM((seq, A_DIM), _BF16) for _ in range(3)]
        + [pltpu.VMEM((seq, LANES), _F32) for _ in range(6)],
        compiler_params=pltpu.CompilerParams(
            dimension_semantics=("arbitrary", "arbitrary"),
            vmem_limit_bytes=_vmem_limit(blocks, scratch + temporaries)),
        name="dilated",
    )(proj, proj, proj, bias)


_B_LEVELS = (32, 16, 8, 4, 2, 1)


def _hgrn_constants():
    c = B_CHUNK
    t = np.arange(c)[:, None]
    r = np.arange(c)[None, :]
    mats = [(r <= t), (r > t)]
    for h in _B_LEVELS:
        base = (t // (2 * h)) * (2 * h)
        mid = base + h - 1
        upper = (t & h) != 0
        mats.append(np.where(upper, (r > mid) & (r <= t), (r > t) & (r <= mid)))
    sums = np.concatenate(mats, axis=0).astype(np.float32)
    s = np.arange(c)[None, :]
    x = t ^ s
    level = np.where(s > t, -1, np.where(s == t, 0, 1 << (np.floor(np.log2(np.maximum(x, 1))).astype(np.int64))))
    return np.concatenate([sums, sums], axis=1), level.astype(np.int32)


def _hgrn_kernel(q_ref, f_ref, i_ref, g_ref, lb_ref, gain_ref, sums_ref, level_ref, o_ref, *, group):
    seq = q_ref.shape[1]
    c = B_CHUNK
    span = group * c
    lb = lb_ref[0]
    gain = gain_ref[...]
    sums2 = sums_ref[...]
    level = level_ref[...]
    chunks = range(group)

    def piece(x, j):
        return x[j * c:(j + 1) * c]

    def step(n, state_t):
        rows = pl.ds(pl.multiple_of(n * span, span), span)
        gate = lb + (1.0 - lb) * _sigmoid(f_ref[0, rows, :])
        log_gate = jnp.log2(gate)
        k = 1.0 - gate
        qraw = q_ref[0, rows, :]
        q = qraw * _sigmoid(qraw) * (B_DIM ** -0.5)
        v = i_ref[0, rows, :].astype(_BF16)

        g_hi = log_gate.astype(_BF16)
        g_lo = (log_gate - g_hi.astype(_F32)).astype(_BF16)
        hi_lo = [jnp.concatenate([piece(g_hi, j), piece(g_lo, j)], axis=0) for j in chunks]
        expo = []
        for j in range(0, group, 2):
            both = jnp.dot(sums2, jnp.concatenate([hi_lo[j], hi_lo[j + 1]], axis=1),
                           preferred_element_type=_F32)
            expo += [both[:, :B_DIM], both[:, B_DIM:]]
        decay = [jnp.exp2(e) for e in expo]

        qc = [piece(q, j) for j in chunks]
        kc = [piece(k, j) for j in chunks]
        vc = [piece(v, j) for j in chunks]
        q_bf = [qc[j].astype(_BF16) for j in chunks]
        k_bf = [kc[j].astype(_BF16) for j in chunks]
        level_decay = [[piece(decay[j], 2 + i).astype(_BF16) for i in range(len(_B_LEVELS))] for j in chunks]
        scaled_q = [[q_bf[j]] + [q_bf[j] * d for d in level_decay[j]] for j in chunks]
        scaled_k = [[k_bf[j]] + [k_bf[j] * d for d in level_decay[j]] for j in chunks]
        q_in = [(qc[j] * piece(decay[j], 0)).astype(_BF16) for j in chunks]
        k_out = [(kc[j] * piece(decay[j], 1)).astype(_BF16) for j in chunks]
        parts = [[lax.dot_general(a, b, _NT, preferred_element_type=_F32)
                  for a, b in zip(scaled_q[j], scaled_k[j])] for j in chunks]
        attn = []
        for j in chunks:
            total = jnp.where(level == 0, parts[j][0], 0.0)
            for i, h in enumerate(_B_LEVELS):
                total = total + jnp.where(level == h, parts[j][1 + i], 0.0)
            attn.append(total.astype(_BF16))
        intra = [jnp.dot(attn[j], vc[j], preferred_element_type=_F32) for j in chunks]
        update = [lax.dot_general(vc[j], k_out[j], _TN, preferred_element_type=_F32) for j in chunks]

        outs = []
        for j in chunks:
            outs.append(intra[j] + lax.dot_general(q_in[j], state_t.astype(_BF16), _NT,
                                                   preferred_element_type=_F32))
            state_t = state_t * decay[j][c - 1:c] + update[j]

        y = _rmsnorm_rows(jnp.concatenate(outs, axis=0), gain)
        graw = g_ref[0, rows, :]
        y = y * (graw * _sigmoid(graw))
        o_ref[0, rows, :] = y.astype(o_ref.dtype)
        return state_t

    lax.fori_loop(0, seq // span, step, jnp.zeros((B_DIM, B_DIM), _F32))


def _hgrn2(proj, lower_bound, gain, sums, level, group):
    bsz, seq, _ = proj.shape
    assert group % 2 == 0 and seq % (group * B_CHUNK) == 0
    head = lambda off: pl.BlockSpec((1, seq, B_DIM), lambda b, h, off=off: (b, 0, off + h))
    blocks = 4 * _nbytes((seq, B_DIM), _F32) + _nbytes((seq, B_DIM), _BF16) \
        + _nbytes(sums.shape, _BF16) + _nbytes(level.shape, jnp.int32) + 2 * _nbytes((8, LANES), _F32)
    return pl.pallas_call(
        functools.partial(_hgrn_kernel, group=group),
        grid=(bsz, B_HEADS),
        in_specs=[head(0), head(B_HEADS), head(2 * B_HEADS), head(3 * B_HEADS),
                  pl.BlockSpec((1, 1, B_DIM), lambda b, h: (h, 0, 0)),
                  pl.BlockSpec((1, B_DIM), lambda b, h: (0, 0)),
                  pl.BlockSpec(sums.shape, lambda b, h: (0, 0)),
                  pl.BlockSpec(level.shape, lambda b, h: (0, 0))],
        out_specs=pl.BlockSpec((1, seq, B_DIM), lambda b, h: (b, 0, h)),
        out_shape=jax.ShapeDtypeStruct((bsz, seq, B_WIDTH), _BF16),
        compiler_params=pltpu.CompilerParams(
            dimension_semantics=("arbitrary", "arbitrary"), vmem_limit_bytes=_vmem_limit(blocks)),
        name="hgrn2",
    )(proj, proj, proj, proj, lower_bound.reshape(B_HEADS, 1, B_DIM), gain.reshape(1, B_DIM), sums, level)


def _swa_kernel(q_ref, k_ref, v_ref, sink_ref, bias_ref, o_ref, klo_ref, khi_ref, vlo_ref, vhi_ref, *, joint):
    seq = q_ref.shape[1]
    kv_head = pl.program_id(1)
    lane_half = lax.broadcasted_iota(jnp.int32, (seq, LANES), 1) // C_DIM
    odd_head = (kv_head % 2) == 1

    def place(x):
        own = jnp.where(lane_half == kv_head % 2, x, 0.0)
        other = pltpu.roll(own, C_DIM, 1)
        lo = jnp.where(odd_head, other, own)
        hi = jnp.where(odd_head, own, other)
        return lo.astype(_BF16), hi.astype(_BF16)

    klo_ref[...], khi_ref[...] = place(k_ref[0])
    vlo_ref[...], vhi_ref[...] = place(v_ref[0])
    pairs = C_GROUP // 2

    heads = [(j, side) for side in range(2) for j in range(pairs)]
    sinks = [sink_ref[kv_head * C_GROUP + 2 * j + side] * LOG2E for j, side in heads]
    low_q = lax.broadcasted_iota(jnp.int32, (BAND, LANES), 1) < C_DIM
    k_refs, v_refs = (klo_ref, khi_ref), (vlo_ref, vhi_ref)

    def attend(block_ids, with_prev):
        curs = [pl.ds(pl.multiple_of(n * BAND, BAND), BAND) for n in block_ids]
        if with_prev:
            keys = [pl.ds(pl.multiple_of((n - 1) * BAND, BAND), 2 * BAND) for n in block_ids]
            bias = bias_ref[...]
        else:
            keys = curs
            bias = bias_ref[:, BAND:]
        count = range(len(block_ids))
        q4 = [jnp.concatenate([q_ref[0, curs[b], j * LANES:(j + 1) * LANES] for j in range(pairs)],
                              axis=0).astype(_BF16) for b in count]
        s_side = [[lax.dot_general(q4[b], k_refs[side][keys[b], :], _NT, preferred_element_type=_F32)
                   for side in range(2)] for b in count]
        s = [[s_side[b][side][j * BAND:(j + 1) * BAND] + bias for j, side in heads] for b in count]
        m = [[jnp.maximum(jnp.max(s_h, axis=-1, keepdims=True), sink) for s_h, sink in zip(s[b], sinks)]
             for b in count]
        p = [[jnp.exp2(s_h - m_h) for s_h, m_h in zip(s[b], m[b])] for b in count]
        l = [[jnp.sum(p_h, axis=-1, keepdims=True) + jnp.exp2(sink - m_h)
              for p_h, m_h, sink in zip(p[b], m[b], sinks)] for b in count]
        num = [[jnp.dot(jnp.concatenate(p[b][side * pairs:(side + 1) * pairs], axis=0).astype(_BF16),
                        v_refs[side][keys[b], :], preferred_element_type=_F32) for side in range(2)]
               for b in count]
        inv = [[1.0 / l_h for l_h in l[b]] for b in count]
        for b in count:
            for j in range(pairs):
                rows = slice(j * BAND, (j + 1) * BAND)
                out = jnp.where(low_q, num[b][0][rows] * inv[b][j], num[b][1][rows] * inv[b][pairs + j])
                o_ref[0, curs[b], j * LANES:(j + 1) * LANES] = out.astype(o_ref.dtype)

    attend([0], False)
    later = seq // BAND - 1
    rounds = later // joint

    def round_(t, carry):
        attend([1 + t * joint + b for b in range(joint)], True)
        return carry

    lax.fori_loop(0, rounds, round_, 0)
    if later % joint:
        attend(list(range(1 + rounds * joint, later + 1)), True)


def _swa(proj, sinks, bias, joint):
    bsz, seq, _ = proj.shape
    qw = C_GROUP * C_DIM
    k_base = C_Q_HEADS * C_DIM // LANES
    v_base = k_base + C_KV_HEADS * C_DIM // LANES
    blocks = _nbytes((seq, qw), _F32) + 2 * _nbytes((seq, LANES), _F32) + _nbytes((seq, qw), _BF16) \
        + _nbytes(bias.shape, _F32)
    scratch = 4 * _nbytes((seq, LANES), _BF16)
    return pl.pallas_call(
        functools.partial(_swa_kernel, joint=joint),
        grid=(bsz, C_KV_HEADS),
        in_specs=[pl.BlockSpec((1, seq, qw), lambda b, h: (b, 0, h)),
                  pl.BlockSpec((1, seq, LANES), lambda b, h: (b, 0, k_base + h // 2)),
                  pl.BlockSpec((1, seq, LANES), lambda b, h: (b, 0, v_base + h // 2)),
                  pl.BlockSpec(memory_space=pltpu.SMEM),
                  pl.BlockSpec(bias.shape, lambda b, h: (0, 0))],
        out_specs=pl.BlockSpec((1, seq, qw), lambda b, h: (b, 0, h)),
        out_shape=jax.ShapeDtypeStruct((bsz, seq, C_Q_HEADS * C_DIM), _BF16),
        scratch_shapes=[pltpu.VMEM((seq, LANES), _BF16) for _ in range(4)],
        compiler_params=pltpu.CompilerParams(
            dimension_semantics=("arbitrary", "arbitrary"),
            vmem_limit_bytes=_vmem_limit(blocks, scratch)),
        name="swa",
    )(proj, proj, proj, sinks.astype(_F32), bias)


def kernel(x, norm_mix_g, norm_mlp_g, final_norm_g, even_w_in, even_w_out, hgrn_lb_raw, hgrn_norm_g,
           odd_w_qkv, odd_b_qkv, odd_sinks, odd_w_o, odd_b_o, mlp_w1, mlp_w2):
    bsz, seq, d = x.shape
    depth = norm_mix_g.shape[0]
    t = _tiles()
    m = bsz * seq

    tables_a = _rope_tables(seq, A_DIM)
    tables_c = _rope_tables(seq, C_DIM)
    bias_a = jnp.asarray(_band_bias_window(BAND))
    bias_c = jnp.asarray(_band_bias_window(C_WINDOW - 1))
    sums_np, level_np = _hgrn_constants()
    sums, level = jnp.asarray(sums_np, _BF16), jnp.asarray(level_np)

    lb_soft = jax.nn.softmax(hgrn_lb_raw.astype(_F32), axis=0)
    lower_bounds = jnp.cumsum(lb_soft, axis=0) - lb_soft[0:1]

    w_in_a = even_w_in[:, :, :A_IN].astype(_BF16)
    w_in_b = even_w_in[:, :, A_IN:].astype(_BF16)
    w_out = even_w_out.astype(_BF16)
    w_qkv = odd_w_qkv.astype(_BF16)
    w_o = odd_w_o.astype(_BF16)
    w1 = mlp_w1.astype(_BF16)
    w2 = mlp_w2.astype(_BF16)

    group_a = A_WIDTH // LANES
    rope_a = [A_DIM ** -0.5 * LOG2E] * group_a + [1.0] * group_a + [None] * group_a
    q_groups = C_Q_HEADS * C_DIM // LANES
    kv_groups = C_KV_HEADS * C_DIM // LANES
    rope_c = [C_DIM ** -0.5 * LOG2E] * q_groups + [1.0] * kv_groups + [None] * kv_groups
    zeros_d = jnp.zeros((d,), _F32)

    x2 = x.reshape(m, d)
    for layer in range(depth):
        if layer % 2 == 0:
            e = layer // 2
            proj_a = _norm_proj(x2, norm_mix_g[layer], w_in_a, e, 0, A_IN, jnp.zeros((A_IN,), _F32),
                                tables_a, rope_a, seq, t["proj_tm"]).reshape(bsz, seq, A_IN)
            proj_b = _norm_proj(x2, norm_mix_g[layer], w_in_b, e, 0, B_IN, jnp.zeros((B_IN,), _F32),
                                None, [None] * (B_IN // LANES), seq, t["proj_tm"]).reshape(bsz, seq, B_IN)
            oa = _dilated_attention(proj_a, bias_a, t["dilated_group"]).reshape(m, A_WIDTH)
            ob = _hgrn2(proj_b, lower_bounds[e], hgrn_norm_g[e], sums, level, t["hgrn_group"]).reshape(m, B_WIDTH)
            x2, h2 = _res_matmul([oa, ob], w_out, e, zeros_d, x2, norm_mlp_g[layer], t["out_tm"])
        else:
            o = layer // 2
            proj = _norm_proj(x2, norm_mix_g[layer], w_qkv, o, 0, C_QKV, odd_b_qkv[o],
                              tables_c, rope_c, seq, t["proj_tm"]).reshape(bsz, seq, C_QKV)
            attn = _swa(proj, odd_sinks[o], bias_c, t["swa_joint"]).reshape(m, C_Q_HEADS * C_DIM)
            x2, h2 = _res_matmul([attn], w_o, o, odd_b_o[o], x2, norm_mlp_g[layer], t["out_tm"])
        last = layer == depth - 1
        x2 = _mlp(x2, h2, w1, w2, layer, final_norm_g if last else None, t["mlp_tm"], t["mlp_tf"])
    return x2.reshape(bsz, seq, d)
```

```python
import functools
import math

import numpy as np
import jax
import jax.numpy as jnp
from jax import lax
from jax.experimental import pallas as pl
from jax.experimental.pallas import tpu as pltpu

D_MODEL = 2048
NORM_EPS = 1e-5
ROPE_THETA = 500000.0
ROPE_FRACTION = 4
BAND = 128

A_DIM = 128
A_HEADS = 8
A_BRANCHES = ((128, 1), (512, 4), (2048, 16))
A_WIDTH = A_HEADS * A_DIM
B_DIM = 128
B_HEADS = 8
B_WIDTH = B_HEADS * B_DIM
B_CHUNK = 64
A_IN = 3 * A_WIDTH
B_IN = 4 * B_WIDTH

C_DIM = 64
C_Q_HEADS = 32
C_KV_HEADS = 4
C_GROUP = C_Q_HEADS // C_KV_HEADS
C_WINDOW = 128
C_QKV = (C_Q_HEADS + 2 * C_KV_HEADS) * C_DIM
D_FF = 4 * D_MODEL

LANES = 128
V7X_VMEM_BYTES = 64 * 1024 * 1024
COMPILER_TEMP_BYTES = 6 * 1024 * 1024

MASKED = -1e30
LOG2E = math.log2(math.e)

_BF16 = jnp.bfloat16
_F32 = jnp.float32
_NT = (((1,), (1,)), ((), ()))
_TN = (((0,), (0,)), ((), ()))


def _tiles():
    return dict(
        proj_tm=512,
        out_tm=512,
        mlp_tm=1024, mlp_tf=1024,
        dilated_group=16,
        hgrn_group=16,
        swa_joint=3,
    )


def _vmem_limit(pipelined_bytes, single_bytes=0):
    need = 2 * pipelined_bytes + single_bytes + COMPILER_TEMP_BYTES
    assert need <= V7X_VMEM_BYTES, need
    return int(need)


def _nbytes(shape, dtype):
    return int(np.prod(shape)) * jnp.dtype(dtype).itemsize


def _rmsnorm_rows(x, g):
    ms = jnp.mean(x * x, axis=-1, keepdims=True)
    return x * lax.rsqrt(ms + NORM_EPS) * g


def _sigmoid(x):
    return 1.0 / (1.0 + jnp.exp2(x * (-LOG2E)))


def _rope_tables(seq, head_dim):
    rot = head_dim // ROPE_FRACTION
    half = rot // 2
    inv_freq = 1.0 / (ROPE_THETA ** (jnp.arange(0, rot, 2, dtype=_F32) / rot))
    ang = jnp.arange(seq, dtype=_F32)[:, None] * inv_freq[None, :]
    cos, sin = jnp.cos(ang), jnp.sin(ang)
    pad = head_dim - 2 * half
    ones = jnp.ones((seq, pad), _F32)
    zeros = jnp.zeros((seq, pad), _F32)
    zh = jnp.zeros((seq, half), _F32)
    reps = LANES // head_dim
    cos_full = jnp.tile(jnp.concatenate([cos, cos, ones], axis=1), (1, reps))
    sin_lo = jnp.tile(jnp.concatenate([-sin, zh, zeros], axis=1), (1, reps))
    sin_hi = jnp.tile(jnp.concatenate([zh, sin, zeros], axis=1), (1, reps))
    return cos_full, sin_lo, sin_hi, half


def _rope(x, cos_full, sin_lo, sin_hi, half):
    up = pltpu.roll(x, LANES - half, 1)
    down = pltpu.roll(x, half, 1)
    return x * cos_full + up * sin_lo + down * sin_hi


def _band_bias_window(window):
    i = np.arange(BAND)[:, None]
    j = np.arange(2 * BAND)[None, :]
    dist = i + BAND - j
    return np.where((dist >= 0) & (dist <= window), 0.0, MASKED).astype(np.float32)


def _norm_proj_kernel(*refs, rope_scales, half):
    if half is None:
        x_ref, g_ref, w_ref, b_ref, o_ref = refs
    else:
        x_ref, g_ref, w_ref, b_ref, cos_ref, slo_ref, shi_ref, o_ref = refs
        cos_full, sin_lo, sin_hi = cos_ref[...], slo_ref[...], shi_ref[...]
    h = _rmsnorm_rows(x_ref[...], g_ref[...]).astype(_BF16)
    acc = jnp.dot(h, w_ref[0], preferred_element_type=_F32) + b_ref[...]
    for gi, scale in enumerate(rope_scales):
        sl = slice(gi * LANES, (gi + 1) * LANES)
        if scale is None:
            o_ref[:, sl] = acc[:, sl]
        else:
            o_ref[:, sl] = _rope(acc[:, sl], cos_full, sin_lo, sin_hi, half) * scale


def _norm_proj(x, g, w_stack, layer, col_block, n, b, tables, rope_scales, seq, tm):
    m, d = x.shape
    assert len(rope_scales) * LANES == n
    in_specs = [
        pl.BlockSpec((tm, d), lambda i: (i, 0)),
        pl.BlockSpec((1, d), lambda i: (0, 0)),
        pl.BlockSpec((1, d, n), lambda i: (layer, 0, col_block), pipeline_mode=pl.Buffered(1)),
        pl.BlockSpec((1, n), lambda i: (0, 0)),
    ]
    args = [x, g.reshape(1, d), w_stack, b.reshape(1, n)]
    blocks = _nbytes((tm, d), _F32) + _nbytes((tm, n), _F32) + _nbytes((1, d), _F32) + _nbytes((1, n), _F32)
    half = None
    if tables is not None:
        cos_full, sin_lo, sin_hi, half = tables
        per_seq = seq // tm
        in_specs += [pl.BlockSpec((tm, LANES), lambda i: (i % per_seq, 0))] * 3
        args += [cos_full, sin_lo, sin_hi]
        blocks += 3 * _nbytes((tm, LANES), _F32)
    single = _nbytes((d, n), _BF16) + _nbytes((tm, d), _BF16) + _nbytes((tm, n), _F32)
    return pl.pallas_call(
        functools.partial(_norm_proj_kernel, rope_scales=tuple(rope_scales), half=half),
        grid=(m // tm,),
        in_specs=in_specs,
        out_specs=pl.BlockSpec((tm, n), lambda i: (i, 0)),
        out_shape=jax.ShapeDtypeStruct((m, n), _F32),
        compiler_params=pltpu.CompilerParams(
            dimension_semantics=("arbitrary",), vmem_limit_bytes=_vmem_limit(blocks, single)),
        name="norm_proj",
    )(*args)


def _res_matmul_kernel(*refs, n_in):
    a_refs, w_refs = refs[:n_in], refs[n_in:2 * n_in]
    b_ref, r_ref, g_ref, o_ref, h_ref = refs[2 * n_in:]
    acc = r_ref[...] + b_ref[...]
    for a_ref, w_ref in zip(a_refs, w_refs):
        acc = acc + jnp.dot(a_ref[...], w_ref[0], preferred_element_type=_F32)
    o_ref[...] = acc
    h_ref[...] = _rmsnorm_rows(acc, g_ref[...]).astype(_BF16)


def _res_matmul(a_list, w_stack, layer, b, res, g_next, tm):
    m, n = res.shape
    k = a_list[0].shape[1]
    blocks = (_nbytes((tm, n), _F32) * 2 + _nbytes((tm, n), _BF16) + 2 * _nbytes((1, n), _F32)
              + len(a_list) * _nbytes((tm, k), _BF16))
    in_specs = [pl.BlockSpec((tm, k), lambda i: (i, 0)) for _ in a_list]
    in_specs += [pl.BlockSpec((1, k, n), lambda i, r=r: (layer, r, 0), pipeline_mode=pl.Buffered(1))
                 for r in range(len(a_list))]
    in_specs += [pl.BlockSpec((1, n), lambda i: (0, 0)), pl.BlockSpec((tm, n), lambda i: (i, 0)),
                 pl.BlockSpec((1, n), lambda i: (0, 0))]
    return pl.pallas_call(
        functools.partial(_res_matmul_kernel, n_in=len(a_list)),
        grid=(m // tm,),
        in_specs=in_specs,
        out_specs=[pl.BlockSpec((tm, n), lambda i: (i, 0)), pl.BlockSpec((tm, n), lambda i: (i, 0))],
        out_shape=[jax.ShapeDtypeStruct((m, n), _F32), jax.ShapeDtypeStruct((m, n), _BF16)],
        compiler_params=pltpu.CompilerParams(
            dimension_semantics=("arbitrary",),
            vmem_limit_bytes=_vmem_limit(blocks, len(a_list) * _nbytes((k, n), _BF16))),
        name="res_matmul",
    )(*a_list, *([w_stack] * len(a_list)), b.reshape(1, n), res, g_next.reshape(1, n))


def _mlp_kernel(*refs, final):
    if final:
        x_ref, h_ref, w1_ref, w2_ref, gf_ref, o_ref = refs
    else:
        x_ref, h_ref, w1_ref, w2_ref, o_ref = refs
    j = pl.program_id(1)
    a = jnp.dot(h_ref[...], w1_ref[0], preferred_element_type=_F32)
    a = jnp.square(jnp.maximum(a, 0.0)).astype(_BF16)
    so_far = jnp.where(j == 0, 0.0, o_ref[...])
    o_ref[...] = so_far + jnp.dot(a, w2_ref[0], preferred_element_type=_F32)
    piece = x_ref.shape[0]
    rows = pl.ds(pl.multiple_of(j * piece, piece), piece)
    o_ref[rows, :] += x_ref[...]
    if final:
        @pl.when(j == pl.num_programs(1) - 1)
        def _():
            o_ref[...] = _rmsnorm_rows(o_ref[...], gf_ref[...])


def _mlp(x, h, w1_stack, w2_stack, layer, final_gain, tm, tf):
    m, d = x.shape
    f = w1_stack.shape[2]
    n_tiles = f // tf
    piece = tm // n_tiles
    blocks = (_nbytes((tm, d), _F32) + _nbytes((piece, d), _F32) + _nbytes((tm, d), _BF16) + _nbytes((1, d), _F32)
              + _nbytes((d, tf), _BF16) + _nbytes((tf, d), _BF16))
    temporaries = _nbytes((tm, tf), _F32) + _nbytes((tm, tf), _BF16)
    in_specs = [
        pl.BlockSpec((piece, d), lambda i, j: (i * n_tiles + j, 0)),
        pl.BlockSpec((tm, d), lambda i, j: (i, 0)),
        pl.BlockSpec((1, d, tf), lambda i, j: (layer, 0, j)),
        pl.BlockSpec((1, tf, d), lambda i, j: (layer, j, 0)),
    ]
    args = [x, h, w1_stack, w2_stack]
    if final_gain is not None:
        in_specs.append(pl.BlockSpec((1, d), lambda i, j: (0, 0)))
        args.append(final_gain.reshape(1, d))
    return pl.pallas_call(
        functools.partial(_mlp_kernel, final=final_gain is not None),
        grid=(m // tm, f // tf),
        in_specs=in_specs,
        out_specs=pl.BlockSpec((tm, d), lambda i, j: (i, 0)),
        out_shape=jax.ShapeDtypeStruct((m, d), _F32),
        compiler_params=pltpu.CompilerParams(
            dimension_semantics=("arbitrary", "arbitrary"),
            vmem_limit_bytes=_vmem_limit(blocks, temporaries)),
        name="mlp",
    )(*args)


def _by4_bias(class_len):
    assert A_BRANCHES == ((128, 1), (512, 4), (2048, 16))
    blocks = class_len // BAND
    i = np.arange(BAND)[:, None]
    j = np.arange(BAND)[None, :]
    tiles = []
    for off in range(blocks - 1, -1, -1):
        dist = off * BAND + i - j
        in_d4 = (dist >= 0) & (dist <= A_BRANCHES[1][0] // A_BRANCHES[1][1])
        in_d16 = (dist >= 0) & (dist % 4 == 0) & (dist // 4 <= A_BRANCHES[2][0] // A_BRANCHES[2][1])
        mult = in_d4.astype(np.int64) + in_d16.astype(np.int64)
        tiles.append(np.where(mult > 0, np.log2(np.maximum(mult, 1)), MASKED))
    return np.concatenate(tiles, axis=1).astype(np.float32)


def _dilated_kernel(q_ref, k_ref, v_ref, bias_ref, bias4_ref, o_ref,
                    qp_ref, kp_ref, vp_ref, acc_ref, m_ref, l_ref, acc4_ref, m4_ref, l4_ref, *, group):
    seq = acc_ref.shape[0]
    n_all = seq // BAND
    quarter = seq // 4
    n_class = quarter // BAND
    bias_full = bias_ref[...]
    col = lax.broadcasted_iota(jnp.int32, (BAND, 2 * BAND), 1)
    bias_first = jnp.where(col < BAND, MASKED, bias_full)
    batch_nt = (((2,), (2,)), ((0,), (0,)))
    batch_nn = (((2,), (1,)), ((0,), (0,)))

    def by4_rows(c):
        return pl.ds(c, quarter, stride=4)

    def softmax_parts(q, k, v, biases):
        s = lax.dot_general(q, k, batch_nt, preferred_element_type=_F32)
        s = jnp.stack([s[g] + biases[g] for g in range(len(biases))])
        m = jnp.max(s, axis=-1, keepdims=True)
        p = jnp.exp2(s - m)
        l = jnp.sum(p, axis=-1, keepdims=True)
        num = lax.dot_general(p.astype(_BF16), v, batch_nn, preferred_element_type=_F32)
        shape = num.shape
        return num, jnp.broadcast_to(m, shape), jnp.broadcast_to(l, shape)

    qp_ref[...] = q_ref[0].astype(_BF16)
    kp_ref[...] = k_ref[0].astype(_BF16)
    vp_ref[...] = v_ref[0].astype(_BF16)
    for g0 in range(0, n_all, group):
        cur = slice(g0 * BAND, (g0 + group) * BAND)
        if g0 == 0:
            prev_of = lambda ref: jnp.concatenate([ref[0:BAND, :], ref[0:(group - 1) * BAND, :]], axis=0)
        else:
            prev_of = lambda ref: ref[(g0 - 1) * BAND:(g0 + group - 1) * BAND, :]
        shape = (group, BAND, A_DIM)
        k = jnp.concatenate([prev_of(kp_ref).reshape(shape), kp_ref[cur, :].reshape(shape)], axis=1)
        v = jnp.concatenate([prev_of(vp_ref).reshape(shape), vp_ref[cur, :].reshape(shape)], axis=1)
        biases = [bias_first if g0 + g == 0 else bias_full for g in range(group)]
        num, m, l = softmax_parts(qp_ref[cur, :].reshape(shape), k, v, biases)
        acc_ref[cur, :] = num.reshape(group * BAND, A_DIM)
        m_ref[cur, :] = m.reshape(group * BAND, A_DIM)
        l_ref[cur, :] = l.reshape(group * BAND, A_DIM)

    for c in range(4):
        seg = slice(c * quarter, (c + 1) * quarter)
        qp_ref[seg, :] = q_ref[0, by4_rows(c), :].astype(_BF16)
        kp_ref[seg, :] = k_ref[0, by4_rows(c), :].astype(_BF16)
        vp_ref[seg, :] = v_ref[0, by4_rows(c), :].astype(_BF16)
    bias4 = bias4_ref[...]
    parts = []
    for n in range(n_class):
        q = jnp.stack([qp_ref[c * quarter + n * BAND:c * quarter + (n + 1) * BAND, :] for c in range(4)])
        k = jnp.stack([kp_ref[c * quarter:c * quarter + (n + 1) * BAND, :] for c in range(4)])
        v = jnp.stack([vp_ref[c * quarter:c * quarter + (n + 1) * BAND, :] for c in range(4)])
        parts.append(softmax_parts(q, k, v, [bias4[:, (n_class - 1 - n) * BAND:]] * 4))
    for n, (num, m, l) in enumerate(parts):
        for c in range(4):
            rows = slice(c * quarter + n * BAND, c * quarter + (n + 1) * BAND)
            acc4_ref[rows, :] = num[c]
            m4_ref[rows, :] = m[c]
            l4_ref[rows, :] = l[c]

    for c in range(4):
        seg = slice(c * quarter, (c + 1) * quarter)
        rows = by4_rows(c)
        m_a, m_b = m_ref[rows, :], m4_ref[seg, :]
        m_new = jnp.maximum(m_a, m_b)
        a = jnp.exp2(m_a - m_new)
        b = jnp.exp2(m_b - m_new)
        num = acc_ref[rows, :] * a + acc4_ref[seg, :] * b
        den = l_ref[rows, :] * a + l4_ref[seg, :] * b
        acc_ref[rows, :] = num / den
    o_ref[0] = acc_ref[...].astype(o_ref.dtype)


def _dilated_attention(proj, bias, bias4, group):
    bsz, seq, _ = proj.shape
    head = lambda off: pl.BlockSpec((1, seq, A_DIM), lambda b, h, off=off: (b, 0, off + h))
    blocks = (3 * _nbytes((seq, A_DIM), _F32) + _nbytes((seq, A_DIM), _BF16)
              + _nbytes(bias.shape, _F32) + _nbytes(bias4.shape, _F32))
    scratch = 6 * _nbytes((seq, LANES), _F32) + 3 * _nbytes((seq, LANES), _BF16)
    temporaries = 2 * group * _nbytes((BAND, 2 * BAND), _F32)
    return pl.pallas_call(
        functools.partial(_dilated_kernel, group=group),
        grid=(bsz, A_HEADS),
        in_specs=[head(0), head(A_HEADS), head(2 * A_HEADS),
                  pl.BlockSpec(bias.shape, lambda b, h: (0, 0)),
                  pl.BlockSpec(bias4.shape, lambda b, h: (0, 0))],
        out_specs=pl.BlockSpec((1, seq, A_DIM), lambda b, h: (b, 0, h)),
        out_shape=jax.ShapeDtypeStruct((bsz, seq, A_WIDTH), _BF16),
        scratch_shapes=[pltpu.VMEM((seq, A_DIM), _BF16) for _ in range(3)]
        + [pltpu.VMEM((seq, LANES), _F32) for _ in range(6)],
        compiler_params=pltpu.CompilerParams(
            dimension_semantics=("arbitrary", "arbitrary"),
            vmem_limit_bytes=_vmem_limit(blocks, scratch + temporaries)),
        name="dilated",
    )(proj, proj, proj, bias, bias4)


_B_LEVELS = (32, 16, 8, 4, 2, 1)


def _hgrn_constants():
    c = B_CHUNK
    t = np.arange(c)[:, None]
    r = np.arange(c)[None, :]
    mats = [(r <= t), (r > t)]
    for h in _B_LEVELS:
        base = (t // (2 * h)) * (2 * h)
        mid = base + h - 1
        upper = (t & h) != 0
        mats.append(np.where(upper, (r > mid) & (r <= t), (r > t) & (r <= mid)))
    sums = np.concatenate(mats, axis=0).astype(np.float32)
    s = np.arange(c)[None, :]
    x = t ^ s
    level = np.where(s > t, -1, np.where(s == t, 0, 1 << (np.floor(np.log2(np.maximum(x, 1))).astype(np.int64))))
    return np.concatenate([sums, sums], axis=1), level.astype(np.int32)


def _hgrn_kernel(q_ref, f_ref, i_ref, g_ref, lb_ref, gain_ref, sums_ref, level_ref, o_ref, *, group):
    seq = q_ref.shape[1]
    c = B_CHUNK
    span = group * c
    lb = lb_ref[0]
    gain = gain_ref[...]
    sums2 = sums_ref[...]
    level = level_ref[...]
    chunks = range(group)

    def piece(x, j):
        return x[j * c:(j + 1) * c]

    def step(n, state_t):
        rows = pl.ds(pl.multiple_of(n * span, span), span)
        gate = lb + (1.0 - lb) * _sigmoid(f_ref[0, rows, :])
        log_gate = jnp.log2(gate)
        k = 1.0 - gate
        qraw = q_ref[0, rows, :]
        q = qraw * _sigmoid(qraw) * (B_DIM ** -0.5)
        v = i_ref[0, rows, :].astype(_BF16)

        g_hi = log_gate.astype(_BF16)
        g_lo = (log_gate - g_hi.astype(_F32)).astype(_BF16)
        hi_lo = [jnp.concatenate([piece(g_hi, j), piece(g_lo, j)], axis=0) for j in chunks]
        expo = []
        for j in range(0, group, 2):
            both = jnp.dot(sums2, jnp.concatenate([hi_lo[j], hi_lo[j + 1]], axis=1),
                           preferred_element_type=_F32)
            expo += [both[:, :B_DIM], both[:, B_DIM:]]
        decay = [jnp.exp2(e) for e in expo]

        qc = [piece(q, j) for j in chunks]
        kc = [piece(k, j) for j in chunks]
        vc = [piece(v, j) for j in chunks]
        q_bf = [qc[j].astype(_BF16) for j in chunks]
        k_bf = [kc[j].astype(_BF16) for j in chunks]
        level_decay = [[piece(decay[j], 2 + i).astype(_BF16) for i in range(len(_B_LEVELS))] for j in chunks]
        scaled_q = [[q_bf[j]] + [q_bf[j] * d for d in level_decay[j]] for j in chunks]
        scaled_k = [[k_bf[j]] + [k_bf[j] * d for d in level_decay[j]] for j in chunks]
        q_in = [(qc[j] * piece(decay[j], 0)).astype(_BF16) for j in chunks]
        k_out = [(kc[j] * piece(decay[j], 1)).astype(_BF16) for j in chunks]
        parts = [[lax.dot_general(a, b, _NT, preferred_element_type=_F32)
                  for a, b in zip(scaled_q[j], scaled_k[j])] for j in chunks]
        attn = []
        for j in chunks:
            total = jnp.where(level == 0, parts[j][0], 0.0)
            for i, h in enumerate(_B_LEVELS):
                total = total + jnp.where(level == h, parts[j][1 + i], 0.0)
            attn.append(total.astype(_BF16))
        intra = [jnp.dot(attn[j], vc[j], preferred_element_type=_F32) for j in chunks]
        update = [lax.dot_general(vc[j], k_out[j], _TN, preferred_element_type=_F32) for j in chunks]

        outs = []
        for j in chunks:
            outs.append(intra[j] + lax.dot_general(q_in[j], state_t.astype(_BF16), _NT,
                                                   preferred_element_type=_F32))
            state_t = state_t * decay[j][c - 1:c] + update[j]

        y = _rmsnorm_rows(jnp.concatenate(outs, axis=0), gain)
        graw = g_ref[0, rows, :]
        y = y * (graw * _sigmoid(graw))
        o_ref[0, rows, :] = y.astype(o_ref.dtype)
        return state_t

    lax.fori_loop(0, seq // span, step, jnp.zeros((B_DIM, B_DIM), _F32))


def _hgrn2(proj, lower_bound, gain, sums, level, group):
    bsz, seq, _ = proj.shape
    assert group % 2 == 0 and seq % (group * B_CHUNK) == 0
    head = lambda off: pl.BlockSpec((1, seq, B_DIM), lambda b, h, off=off: (b, 0, off + h))
    blocks = 4 * _nbytes((seq, B_DIM), _F32) + _nbytes((seq, B_DIM), _BF16) \
        + _nbytes(sums.shape, _BF16) + _nbytes(level.shape, jnp.int32) + 2 * _nbytes((8, LANES), _F32)
    return pl.pallas_call(
        functools.partial(_hgrn_kernel, group=group),
        grid=(bsz, B_HEADS),
        in_specs=[head(0), head(B_HEADS), head(2 * B_HEADS), head(3 * B_HEADS),
                  pl.BlockSpec((1, 1, B_DIM), lambda b, h: (h, 0, 0)),
                  pl.BlockSpec((1, B_DIM), lambda b, h: (0, 0)),
                  pl.BlockSpec(sums.shape, lambda b, h: (0, 0)),
                  pl.BlockSpec(level.shape, lambda b, h: (0, 0))],
        out_specs=pl.BlockSpec((1, seq, B_DIM), lambda b, h: (b, 0, h)),
        out_shape=jax.ShapeDtypeStruct((bsz, seq, B_WIDTH), _BF16),
        compiler_params=pltpu.CompilerParams(
            dimension_semantics=("arbitrary", "arbitrary"), vmem_limit_bytes=_vmem_limit(blocks)),
        name="hgrn2",
    )(proj, proj, proj, proj, lower_bound.reshape(B_HEADS, 1, B_DIM), gain.reshape(1, B_DIM), sums, level)


def _swa_kernel(q_ref, k_ref, v_ref, sink_ref, bias_ref, o_ref, klo_ref, khi_ref, vlo_ref, vhi_ref, *, joint):
    seq = q_ref.shape[1]
    kv_head = pl.program_id(1)
    lane_half = lax.broadcasted_iota(jnp.int32, (seq, LANES), 1) // C_DIM
    odd_head = (kv_head % 2) == 1

    def place(x):
        own = jnp.where(lane_half == kv_head % 2, x, 0.0)
        other = pltpu.roll(own, C_DIM, 1)
        lo = jnp.where(odd_head, other, own)
        hi = jnp.where(odd_head, own, other)
        return lo.astype(_BF16), hi.astype(_BF16)

    klo_ref[...], khi_ref[...] = place(k_ref[0])
    vlo_ref[...], vhi_ref[...] = place(v_ref[0])
    pairs = C_GROUP // 2

    heads = [(j, side) for side in range(2) for j in range(pairs)]
    sinks = [sink_ref[kv_head * C_GROUP + 2 * j + side] * LOG2E for j, side in heads]
    low_q = lax.broadcasted_iota(jnp.int32, (BAND, LANES), 1) < C_DIM
    k_refs, v_refs = (klo_ref, khi_ref), (vlo_ref, vhi_ref)

    def attend(block_ids, with_prev):
        curs = [pl.ds(pl.multiple_of(n * BAND, BAND), BAND) for n in block_ids]
        if with_prev:
            keys = [pl.ds(pl.multiple_of((n - 1) * BAND, BAND), 2 * BAND) for n in block_ids]
            bias = bias_ref[...]
        else:
            keys = curs
            bias = bias_ref[:, BAND:]
        count = range(len(block_ids))
        q4 = [jnp.concatenate([q_ref[0, curs[b], j * LANES:(j + 1) * LANES] for j in range(pairs)],
                              axis=0).astype(_BF16) for b in count]
        s_side = [[lax.dot_general(q4[b], k_refs[side][keys[b], :], _NT, preferred_element_type=_F32)
                   for side in range(2)] for b in count]
        s = [[s_side[b][side][j * BAND:(j + 1) * BAND] + bias for j, side in heads] for b in count]
        m = [[jnp.maximum(jnp.max(s_h, axis=-1, keepdims=True), sink) for s_h, sink in zip(s[b], sinks)]
             for b in count]
        p = [[jnp.exp2(s_h - m_h) for s_h, m_h in zip(s[b], m[b])] for b in count]
        l = [[jnp.sum(p_h, axis=-1, keepdims=True) + jnp.exp2(sink - m_h)
              for p_h, m_h, sink in zip(p[b], m[b], sinks)] for b in count]
        num = [[jnp.dot(jnp.concatenate(p[b][side * pairs:(side + 1) * pairs], axis=0).astype(_BF16),
                        v_refs[side][keys[b], :], preferred_element_type=_F32) for side in range(2)]
               for b in count]
        inv = [[1.0 / l_h for l_h in l[b]] for b in count]
        for b in count:
            for j in range(pairs):
                rows = slice(j * BAND, (j + 1) * BAND)
                out = jnp.where(low_q, num[b][0][rows] * inv[b][j], num[b][1][rows] * inv[b][pairs + j])
                o_ref[0, curs[b], j * LANES:(j + 1) * LANES] = out.astype(o_ref.dtype)

    attend([0], False)
    later = seq // BAND - 1
    rounds = later // joint

    def round_(t, carry):
        attend([1 + t * joint + b for b in range(joint)], True)
        return carry

    lax.fori_loop(0, rounds, round_, 0)
    if later % joint:
        attend(list(range(1 + rounds * joint, later + 1)), True)


def _swa(proj, sinks, bias, joint):
    bsz, seq, _ = proj.shape
    qw = C_GROUP * C_DIM
    k_base = C_Q_HEADS * C_DIM // LANES
    v_base = k_base + C_KV_HEADS * C_DIM // LANES
    blocks = _nbytes((seq, qw), _F32) + 2 * _nbytes((seq, LANES), _F32) + _nbytes((seq, qw), _BF16) \
        + _nbytes(bias.shape, _F32)
    scratch = 4 * _nbytes((seq, LANES), _BF16)
    return pl.pallas_call(
        functools.partial(_swa_kernel, joint=joint),
        grid=(bsz, C_KV_HEADS),
        in_specs=[pl.BlockSpec((1, seq, qw), lambda b, h: (b, 0, h)),
                  pl.BlockSpec((1, seq, LANES), lambda b, h: (b, 0, k_base + h // 2)),
                  pl.BlockSpec((1, seq, LANES), lambda b, h: (b, 0, v_base + h // 2)),
                  pl.BlockSpec(memory_space=pltpu.SMEM),
                  pl.BlockSpec(bias.shape, lambda b, h: (0, 0))],
        out_specs=pl.BlockSpec((1, seq, qw), lambda b, h: (b, 0, h)),
        out_shape=jax.ShapeDtypeStruct((bsz, seq, C_Q_HEADS * C_DIM), _BF16),
        scratch_shapes=[pltpu.VMEM((seq, LANES), _BF16) for _ in range(4)],
        compiler_params=pltpu.CompilerParams(
            dimension_semantics=("arbitrary", "arbitrary"),
            vmem_limit_bytes=_vmem_limit(blocks, scratch)),
        name="swa",
    )(proj, proj, proj, sinks.astype(_F32), bias)


def kernel(x, norm_mix_g, norm_mlp_g, final_norm_g, even_w_in, even_w_out, hgrn_lb_raw, hgrn_norm_g,
           odd_w_qkv, odd_b_qkv, odd_sinks, odd_w_o, odd_b_o, mlp_w1, mlp_w2):
    bsz, seq, d = x.shape
    depth = norm_mix_g.shape[0]
    t = _tiles()
    m = bsz * seq

    tables_a = _rope_tables(seq, A_DIM)
    tables_c = _rope_tables(seq, C_DIM)
    bias_a = jnp.asarray(_band_bias_window(BAND))
    bias_a4 = jnp.asarray(_by4_bias(seq // 4))
    bias_c = jnp.asarray(_band_bias_window(C_WINDOW - 1))
    sums_np, level_np = _hgrn_constants()
    sums, level = jnp.asarray(sums_np, _BF16), jnp.asarray(level_np)

    lb_soft = jax.nn.softmax(hgrn_lb_raw.astype(_F32), axis=0)
    lower_bounds = jnp.cumsum(lb_soft, axis=0) - lb_soft[0:1]

    w_in_a = even_w_in[:, :, :A_IN].astype(_BF16)
    w_in_b = even_w_in[:, :, A_IN:].astype(_BF16)
    w_out = even_w_out.astype(_BF16)
    w_qkv = odd_w_qkv.astype(_BF16)
    w_o = odd_w_o.astype(_BF16)
    w1 = mlp_w1.astype(_BF16)
    w2 = mlp_w2.astype(_BF16)

    group_a = A_WIDTH // LANES
    rope_a = [A_DIM ** -0.5 * LOG2E] * group_a + [1.0] * group_a + [None] * group_a
    q_groups = C_Q_HEADS * C_DIM // LANES
    kv_groups = C_KV_HEADS * C_DIM // LANES
    rope_c = [C_DIM ** -0.5 * LOG2E] * q_groups + [1.0] * kv_groups + [None] * kv_groups
    zeros_d = jnp.zeros((d,), _F32)

    x2 = x.reshape(m, d)
    for layer in range(depth):
        if layer % 2 == 0:
            e = layer // 2
            proj_a = _norm_proj(x2, norm_mix_g[layer], w_in_a, e, 0, A_IN, jnp.zeros((A_IN,), _F32),
                                tables_a, rope_a, seq, t["proj_tm"]).reshape(bsz, seq, A_IN)
            proj_b = _norm_proj(x2, norm_mix_g[layer], w_in_b, e, 0, B_IN, jnp.zeros((B_IN,), _F32),
                                None, [None] * (B_IN // LANES), seq, t["proj_tm"]).reshape(bsz, seq, B_IN)
            oa = _dilated_attention(proj_a, bias_a, bias_a4, t["dilated_group"]).reshape(m, A_WIDTH)
            ob = _hgrn2(proj_b, lower_bounds[e], hgrn_norm_g[e], sums, level, t["hgrn_group"]).reshape(m, B_WIDTH)
            x2, h2 = _res_matmul([oa, ob], w_out, e, zeros_d, x2, norm_mlp_g[layer], t["out_tm"])
        else:
            o = layer // 2
            proj = _norm_proj(x2, norm_mix_g[layer], w_qkv, o, 0, C_QKV, odd_b_qkv[o],
                              tables_c, rope_c, seq, t["proj_tm"]).reshape(bsz, seq, C_QKV)
            attn = _swa(proj, odd_sinks[o], bias_c, t["swa_joint"]).reshape(m, C_Q_HEADS * C_DIM)
            x2, h2 = _res_matmul([attn], w_o, o, odd_b_o[o], x2, norm_mlp_g[layer], t["out_tm"])
        last = layer == depth - 1
        x2 = _mlp(x2, h2, w1, w2, layer, final_norm_g if last else None, t["mlp_tm"], t["mlp_tf"])
    return x2.reshape(bsz, seq, d)
```

```python
import functools
import math

import numpy as np
import jax
import jax.numpy as jnp
from jax import lax
from jax.experimental import pallas as pl
from jax.experimental.pallas import tpu as pltpu

D_MODEL = 2048
NORM_EPS = 1e-5
ROPE_THETA = 500000.0
ROPE_FRACTION = 4
BAND = 128

A_DIM = 128
A_HEADS = 8
A_BRANCHES = ((128, 1), (512, 4), (2048, 16))
A_WIDTH = A_HEADS * A_DIM
B_DIM = 128
B_HEADS = 8
B_WIDTH = B_HEADS * B_DIM
B_CHUNK = 64
A_IN = 3 * A_WIDTH
B_IN = 4 * B_WIDTH

C_DIM = 64
C_Q_HEADS = 32
C_KV_HEADS = 4
C_GROUP = C_Q_HEADS // C_KV_HEADS
C_WINDOW = 128
C_QKV = (C_Q_HEADS + 2 * C_KV_HEADS) * C_DIM
D_FF = 4 * D_MODEL

LANES = 128
V7X_VMEM_BYTES = 64 * 1024 * 1024
COMPILER_TEMP_BYTES = 6 * 1024 * 1024

MASKED = -1e30
LOG2E = math.log2(math.e)

_BF16 = jnp.bfloat16
_F32 = jnp.float32
_NT = (((1,), (1,)), ((), ()))
_TN = (((0,), (0,)), ((), ()))


def _tiles():
    return dict(
        proj_tm=512,
        out_tm=512,
        mlp_tm=1024, mlp_tf=1024,
        dilated_group=16,
        hgrn_group=16,
        swa_joint=3,
    )


def _vmem_limit(pipelined_bytes, single_bytes=0):
    need = 2 * pipelined_bytes + single_bytes + COMPILER_TEMP_BYTES
    assert need <= V7X_VMEM_BYTES, need
    return int(need)


def _nbytes(shape, dtype):
    return int(np.prod(shape)) * jnp.dtype(dtype).itemsize


def _rmsnorm_rows(x, g):
    ms = jnp.mean(x * x, axis=-1, keepdims=True)
    return x * lax.rsqrt(ms + NORM_EPS) * g


def _sigmoid(x):
    return 1.0 / (1.0 + jnp.exp2(x * (-LOG2E)))


def _rope_tables(seq, head_dim):
    rot = head_dim // ROPE_FRACTION
    half = rot // 2
    inv_freq = 1.0 / (ROPE_THETA ** (jnp.arange(0, rot, 2, dtype=_F32) / rot))
    ang = jnp.arange(seq, dtype=_F32)[:, None] * inv_freq[None, :]
    cos, sin = jnp.cos(ang), jnp.sin(ang)
    pad = head_dim - 2 * half
    ones = jnp.ones((seq, pad), _F32)
    zeros = jnp.zeros((seq, pad), _F32)
    zh = jnp.zeros((seq, half), _F32)
    reps = LANES // head_dim
    cos_full = jnp.tile(jnp.concatenate([cos, cos, ones], axis=1), (1, reps))
    sin_lo = jnp.tile(jnp.concatenate([-sin, zh, zeros], axis=1), (1, reps))
    sin_hi = jnp.tile(jnp.concatenate([zh, sin, zeros], axis=1), (1, reps))
    return cos_full, sin_lo, sin_hi, half


def _rope(x, cos_full, sin_lo, sin_hi, half):
    up = pltpu.roll(x, LANES - half, 1)
    down = pltpu.roll(x, half, 1)
    return x * cos_full + up * sin_lo + down * sin_hi


def _band_bias_window(window):
    i = np.arange(BAND)[:, None]
    j = np.arange(2 * BAND)[None, :]
    dist = i + BAND - j
    return np.where((dist >= 0) & (dist <= window), 0.0, MASKED).astype(np.float32)


def _norm_proj_kernel(*refs, rope_scales, half, emit_h):
    refs = list(refs)
    h_ref = refs.pop() if emit_h else None
    x_ref, g_ref, w_ref, b_ref, cos_ref, slo_ref, shi_ref, o_ref = refs
    cos_full, sin_lo, sin_hi = cos_ref[...], slo_ref[...], shi_ref[...]
    h = _rmsnorm_rows(x_ref[...], g_ref[...]).astype(_BF16)
    if emit_h:
        h_ref[...] = h
    acc = jnp.dot(h, w_ref[0], preferred_element_type=_F32) + b_ref[...]
    for gi, scale in enumerate(rope_scales):
        sl = slice(gi * LANES, (gi + 1) * LANES)
        if scale is None:
            o_ref[:, sl] = acc[:, sl]
        else:
            o_ref[:, sl] = _rope(acc[:, sl], cos_full, sin_lo, sin_hi, half) * scale


def _norm_proj(x, g, w_stack, layer, n, b, tables, rope_scales, seq, tm, emit_h):
    m, d = x.shape
    assert len(rope_scales) * LANES == n
    cos_full, sin_lo, sin_hi, half = tables
    per_seq = seq // tm
    in_specs = [
        pl.BlockSpec((tm, d), lambda i: (i, 0)),
        pl.BlockSpec((1, d), lambda i: (0, 0)),
        pl.BlockSpec((1, d, n), lambda i: (layer, 0, 0), pipeline_mode=pl.Buffered(1)),
        pl.BlockSpec((1, n), lambda i: (0, 0)),
    ] + [pl.BlockSpec((tm, LANES), lambda i: (i % per_seq, 0))] * 3
    blocks = (_nbytes((tm, d), _F32) + _nbytes((tm, n), _F32) + _nbytes((1, d), _F32) + _nbytes((1, n), _F32)
              + 3 * _nbytes((tm, LANES), _F32))
    out_specs = [pl.BlockSpec((tm, n), lambda i: (i, 0))]
    out_shape = [jax.ShapeDtypeStruct((m, n), _F32)]
    if emit_h:
        out_specs.append(pl.BlockSpec((tm, d), lambda i: (i, 0)))
        out_shape.append(jax.ShapeDtypeStruct((m, d), _BF16))
        blocks += _nbytes((tm, d), _BF16)
    single = _nbytes((d, n), _BF16) + _nbytes((tm, d), _BF16) + _nbytes((tm, n), _F32)
    outs = pl.pallas_call(
        functools.partial(_norm_proj_kernel, rope_scales=tuple(rope_scales), half=half, emit_h=emit_h),
        grid=(m // tm,),
        in_specs=in_specs,
        out_specs=out_specs,
        out_shape=out_shape,
        compiler_params=pltpu.CompilerParams(
            dimension_semantics=("arbitrary",), vmem_limit_bytes=_vmem_limit(blocks, single)),
        name="norm_proj",
    )(x, g.reshape(1, d), w_stack, b.reshape(1, n), cos_full, sin_lo, sin_hi)
    return outs if emit_h else outs[0]


def _proj_kernel(h_ref, w_ref, o_ref):
    o_ref[...] = jnp.dot(h_ref[...], w_ref[0], preferred_element_type=_F32)


def _proj(h, w_stack, layer, tm):
    m, d = h.shape
    n = w_stack.shape[2]
    blocks = _nbytes((tm, d), _BF16) + _nbytes((tm, n), _F32)
    return pl.pallas_call(
        _proj_kernel,
        grid=(m // tm,),
        in_specs=[pl.BlockSpec((tm, d), lambda i: (i, 0)),
                  pl.BlockSpec((1, d, n), lambda i: (layer, 0, 0), pipeline_mode=pl.Buffered(1))],
        out_specs=pl.BlockSpec((tm, n), lambda i: (i, 0)),
        out_shape=jax.ShapeDtypeStruct((m, n), _F32),
        compiler_params=pltpu.CompilerParams(
            dimension_semantics=("arbitrary",),
            vmem_limit_bytes=_vmem_limit(blocks, _nbytes((d, n), _BF16) + _nbytes((tm, n), _F32))),
        name="proj",
    )(h, w_stack)


def _res_matmul_kernel(*refs, n_in):
    a_refs, w_refs = refs[:n_in], refs[n_in:2 * n_in]
    b_ref, r_ref, g_ref, o_ref, h_ref = refs[2 * n_in:]
    acc = r_ref[...] + b_ref[...]
    for a_ref, w_ref in zip(a_refs, w_refs):
        acc = acc + jnp.dot(a_ref[...], w_ref[0], preferred_element_type=_F32)
    o_ref[...] = acc
    h_ref[...] = _rmsnorm_rows(acc, g_ref[...]).astype(_BF16)


def _res_matmul(a_list, w_stack, layer, b, res, g_next, tm):
    m, n = res.shape
    k = a_list[0].shape[1]
    blocks = (_nbytes((tm, n), _F32) * 2 + _nbytes((tm, n), _BF16) + 2 * _nbytes((1, n), _F32)
              + len(a_list) * _nbytes((tm, k), _BF16))
    in_specs = [pl.BlockSpec((tm, k), lambda i: (i, 0)) for _ in a_list]
    in_specs += [pl.BlockSpec((1, k, n), lambda i, r=r: (layer, r, 0), pipeline_mode=pl.Buffered(1))
                 for r in range(len(a_list))]
    in_specs += [pl.BlockSpec((1, n), lambda i: (0, 0)), pl.BlockSpec((tm, n), lambda i: (i, 0)),
                 pl.BlockSpec((1, n), lambda i: (0, 0))]
    return pl.pallas_call(
        functools.partial(_res_matmul_kernel, n_in=len(a_list)),
        grid=(m // tm,),
        in_specs=in_specs,
        out_specs=[pl.BlockSpec((tm, n), lambda i: (i, 0)), pl.BlockSpec((tm, n), lambda i: (i, 0))],
        out_shape=[jax.ShapeDtypeStruct((m, n), _F32), jax.ShapeDtypeStruct((m, n), _BF16)],
        compiler_params=pltpu.CompilerParams(
            dimension_semantics=("arbitrary",),
            vmem_limit_bytes=_vmem_limit(blocks, len(a_list) * _nbytes((k, n), _BF16))),
        name="res_matmul",
    )(*a_list, *([w_stack] * len(a_list)), b.reshape(1, n), res, g_next.reshape(1, n))


def _mlp_kernel(*refs, final):
    if final:
        x_ref, h_ref, w1_ref, w2_ref, gf_ref, o_ref = refs
    else:
        x_ref, h_ref, w1_ref, w2_ref, o_ref = refs
    j = pl.program_id(1)
    a = jnp.dot(h_ref[...], w1_ref[0], preferred_element_type=_F32)
    a = jnp.square(jnp.maximum(a, 0.0)).astype(_BF16)
    so_far = jnp.where(j == 0, 0.0, o_ref[...])
    o_ref[...] = so_far + jnp.dot(a, w2_ref[0], preferred_element_type=_F32)
    piece = x_ref.shape[0]
    rows = pl.ds(pl.multiple_of(j * piece, piece), piece)
    o_ref[rows, :] += x_ref[...]
    if final:
        @pl.when(j == pl.num_programs(1) - 1)
        def _():
            o_ref[...] = _rmsnorm_rows(o_ref[...], gf_ref[...])


def _mlp(x, h, w1_stack, w2_stack, layer, final_gain, tm, tf):
    m, d = x.shape
    f = w1_stack.shape[2]
    n_tiles = f // tf
    piece = tm // n_tiles
    blocks = (_nbytes((tm, d), _F32) + _nbytes((piece, d), _F32) + _nbytes((tm, d), _BF16) + _nbytes((1, d), _F32)
              + _nbytes((d, tf), _BF16) + _nbytes((tf, d), _BF16))
    temporaries = _nbytes((tm, tf), _F32) + _nbytes((tm, tf), _BF16)
    in_specs = [
        pl.BlockSpec((piece, d), lambda i, j: (i * n_tiles + j, 0)),
        pl.BlockSpec((tm, d), lambda i, j: (i, 0)),
        pl.BlockSpec((1, d, tf), lambda i, j: (layer, 0, j)),
        pl.BlockSpec((1, tf, d), lambda i, j: (layer, j, 0)),
    ]
    args = [x, h, w1_stack, w2_stack]
    if final_gain is not None:
        in_specs.append(pl.BlockSpec((1, d), lambda i, j: (0, 0)))
        args.append(final_gain.reshape(1, d))
    return pl.pallas_call(
        functools.partial(_mlp_kernel, final=final_gain is not None),
        grid=(m // tm, f // tf),
        in_specs=in_specs,
        out_specs=pl.BlockSpec((tm, d), lambda i, j: (i, 0)),
        out_shape=jax.ShapeDtypeStruct((m, d), _F32),
        compiler_params=pltpu.CompilerParams(
            dimension_semantics=("arbitrary", "arbitrary"),
            vmem_limit_bytes=_vmem_limit(blocks, temporaries)),
        name="mlp",
    )(*args)


def _by4_bias(class_len):
    assert A_BRANCHES == ((128, 1), (512, 4), (2048, 16))
    blocks = class_len // BAND
    i = np.arange(BAND)[:, None]
    j = np.arange(BAND)[None, :]
    tiles = []
    for off in range(blocks - 1, -1, -1):
        dist = off * BAND + i - j
        in_d4 = (dist >= 0) & (dist <= A_BRANCHES[1][0] // A_BRANCHES[1][1])
        in_d16 = (dist >= 0) & (dist % 4 == 0) & (dist // 4 <= A_BRANCHES[2][0] // A_BRANCHES[2][1])
        mult = in_d4.astype(np.int64) + in_d16.astype(np.int64)
        tiles.append(np.where(mult > 0, np.log2(np.maximum(mult, 1)), MASKED))
    return np.concatenate(tiles, axis=1).astype(np.float32)


def _dilated_kernel(q_ref, k_ref, v_ref, bias_ref, bias4_ref, o_ref,
                    qp_ref, kp_ref, vp_ref, acc_ref, m_ref, l_ref, acc4_ref, m4_ref, l4_ref, *, group):
    seq = acc_ref.shape[0]
    n_all = seq // BAND
    quarter = seq // 4
    n_class = quarter // BAND
    bias_full = bias_ref[...]
    col = lax.broadcasted_iota(jnp.int32, (BAND, 2 * BAND), 1)
    bias_first = jnp.where(col < BAND, MASKED, bias_full)
    batch_nt = (((2,), (2,)), ((0,), (0,)))
    batch_nn = (((2,), (1,)), ((0,), (0,)))

    def by4_rows(c):
        return pl.ds(c, quarter, stride=4)

    def softmax_parts(q, k, v, biases):
        s = lax.dot_general(q, k, batch_nt, preferred_element_type=_F32)
        s = jnp.stack([s[g] + biases[g] for g in range(len(biases))])
        m = jnp.max(s, axis=-1, keepdims=True)
        p = jnp.exp2(s - m)
        l = jnp.sum(p, axis=-1, keepdims=True)
        num = lax.dot_general(p.astype(_BF16), v, batch_nn, preferred_element_type=_F32)
        shape = num.shape
        return num, jnp.broadcast_to(m, shape), jnp.broadcast_to(l, shape)

    qp_ref[...] = q_ref[0].astype(_BF16)
    kp_ref[...] = k_ref[0].astype(_BF16)
    vp_ref[...] = v_ref[0].astype(_BF16)
    for g0 in range(0, n_all, group):
        cur = slice(g0 * BAND, (g0 + group) * BAND)
        if g0 == 0:
            prev_of = lambda ref: jnp.concatenate([ref[0:BAND, :], ref[0:(group - 1) * BAND, :]], axis=0)
        else:
            prev_of = lambda ref: ref[(g0 - 1) * BAND:(g0 + group - 1) * BAND, :]
        shape = (group, BAND, A_DIM)
        k = jnp.concatenate([prev_of(kp_ref).reshape(shape), kp_ref[cur, :].reshape(shape)], axis=1)
        v = jnp.concatenate([prev_of(vp_ref).reshape(shape), vp_ref[cur, :].reshape(shape)], axis=1)
        biases = [bias_first if g0 + g == 0 else bias_full for g in range(group)]
        num, m, l = softmax_parts(qp_ref[cur, :].reshape(shape), k, v, biases)
        acc_ref[cur, :] = num.reshape(group * BAND, A_DIM)
        m_ref[cur, :] = m.reshape(group * BAND, A_DIM)
        l_ref[cur, :] = l.reshape(group * BAND, A_DIM)

    for c in range(4):
        seg = slice(c * quarter, (c + 1) * quarter)
        qp_ref[seg, :] = q_ref[0, by4_rows(c), :].astype(_BF16)
        kp_ref[seg, :] = k_ref[0, by4_rows(c), :].astype(_BF16)
        vp_ref[seg, :] = v_ref[0, by4_rows(c), :].astype(_BF16)
    bias4 = bias4_ref[...]
    parts = []
    for n in range(n_class):
        q = jnp.stack([qp_ref[c * quarter + n * BAND:c * quarter + (n + 1) * BAND, :] for c in range(4)])
        k = jnp.stack([kp_ref[c * quarter:c * quarter + (n + 1) * BAND, :] for c in range(4)])
        v = jnp.stack([vp_ref[c * quarter:c * quarter + (n + 1) * BAND, :] for c in range(4)])
        parts.append(softmax_parts(q, k, v, [bias4[:, (n_class - 1 - n) * BAND:]] * 4))
    for n, (num, m, l) in enumerate(parts):
        for c in range(4):
            rows = slice(c * quarter + n * BAND, c * quarter + (n + 1) * BAND)
            acc4_ref[rows, :] = num[c]
            m4_ref[rows, :] = m[c]
            l4_ref[rows, :] = l[c]

    for c in range(4):
        seg = slice(c * quarter, (c + 1) * quarter)
        rows = by4_rows(c)
        m_a, m_b = m_ref[rows, :], m4_ref[seg, :]
        m_new = jnp.maximum(m_a, m_b)
        a = jnp.exp2(m_a - m_new)
        b = jnp.exp2(m_b - m_new)
        num = acc_ref[rows, :] * a + acc4_ref[seg, :] * b
        den = l_ref[rows, :] * a + l4_ref[seg, :] * b
        acc_ref[rows, :] = num / den
    o_ref[0] = acc_ref[...].astype(o_ref.dtype)


def _dilated_attention(proj, bias, bias4, group):
    bsz, seq, _ = proj.shape
    head = lambda off: pl.BlockSpec((1, seq, A_DIM), lambda b, h, off=off: (b, 0, off + h))
    blocks = (3 * _nbytes((seq, A_DIM), _F32) + _nbytes((seq, A_DIM), _BF16)
              + _nbytes(bias.shape, _F32) + _nbytes(bias4.shape, _F32))
    scratch = 6 * _nbytes((seq, LANES), _F32) + 3 * _nbytes((seq, LANES), _BF16)
    temporaries = 2 * group * _nbytes((BAND, 2 * BAND), _F32)
    return pl.pallas_call(
        functools.partial(_dilated_kernel, group=group),
        grid=(bsz, A_HEADS),
        in_specs=[head(0), head(A_HEADS), head(2 * A_HEADS),
                  pl.BlockSpec(bias.shape, lambda b, h: (0, 0)),
                  pl.BlockSpec(bias4.shape, lambda b, h: (0, 0))],
        out_specs=pl.BlockSpec((1, seq, A_DIM), lambda b, h: (b, 0, h)),
        out_shape=jax.ShapeDtypeStruct((bsz, seq, A_WIDTH), _BF16),
        scratch_shapes=[pltpu.VMEM((seq, A_DIM), _BF16) for _ in range(3)]
        + [pltpu.VMEM((seq, LANES), _F32) for _ in range(6)],
        compiler_params=pltpu.CompilerParams(
            dimension_semantics=("arbitrary", "arbitrary"),
            vmem_limit_bytes=_vmem_limit(blocks, scratch + temporaries)),
        name="dilated",
    )(proj, proj, proj, bias, bias4)


_B_LEVELS = (32, 16, 8, 4, 2, 1)


def _hgrn_constants():
    c = B_CHUNK
    t = np.arange(c)[:, None]
    r = np.arange(c)[None, :]
    mats = [(r <= t), (r > t)]
    for h in _B_LEVELS:
        base = (t // (2 * h)) * (2 * h)
        mid = base + h - 1
        upper = (t & h) != 0
        mats.append(np.where(upper, (r > mid) & (r <= t), (r > t) & (r <= mid)))
    sums = np.concatenate(mats, axis=0).astype(np.float32)
    s = np.arange(c)[None, :]
    x = t ^ s
    level = np.where(s > t, -1, np.where(s == t, 0, 1 << (np.floor(np.log2(np.maximum(x, 1))).astype(np.int64))))
    return np.concatenate([sums, sums], axis=1), level.astype(np.int32)


def _hgrn_kernel(q_ref, f_ref, i_ref, g_ref, lb_ref, gain_ref, sums_ref, level_ref, o_ref, *, group):
    seq = q_ref.shape[1]
    c = B_CHUNK
    span = group * c
    lb = lb_ref[0]
    gain = gain_ref[...]
    sums2 = sums_ref[...]
    level = level_ref[...]
    chunks = range(group)

    def piece(x, j):
        return x[j * c:(j + 1) * c]

    def step(n, state_t):
        rows = pl.ds(pl.multiple_of(n * span, span), span)
        gate = lb + (1.0 - lb) * _sigmoid(f_ref[0, rows, :])
        log_gate = jnp.log2(gate)
        k = 1.0 - gate
        qraw = q_ref[0, rows, :]
        q = qraw * _sigmoid(qraw) * (B_DIM ** -0.5)
        v = i_ref[0, rows, :].astype(_BF16)

        g_hi = log_gate.astype(_BF16)
        g_lo = (log_gate - g_hi.astype(_F32)).astype(_BF16)
        hi_lo = [jnp.concatenate([piece(g_hi, j), piece(g_lo, j)], axis=0) for j in chunks]
        expo = []
        for j in range(0, group, 2):
            both = jnp.dot(sums2, jnp.concatenate([hi_lo[j], hi_lo[j + 1]], axis=1),
                           preferred_element_type=_F32)
            expo += [both[:, :B_DIM], both[:, B_DIM:]]
        decay = [jnp.exp2(e) for e in expo]

        qc = [piece(q, j) for j in chunks]
        kc = [piece(k, j) for j in chunks]
        vc = [piece(v, j) for j in chunks]
        q_bf = [qc[j].astype(_BF16) for j in chunks]
        k_bf = [kc[j].astype(_BF16) for j in chunks]
        level_decay = [[piece(decay[j], 2 + i).astype(_BF16) for i in range(len(_B_LEVELS))] for j in chunks]
        scaled_q = [[q_bf[j]] + [q_bf[j] * d for d in level_decay[j]] for j in chunks]
        scaled_k = [[k_bf[j]] + [k_bf[j] * d for d in level_decay[j]] for j in chunks]
        q_in = [(qc[j] * piece(decay[j], 0)).astype(_BF16) for j in chunks]
        k_out = [(kc[j] * piece(decay[j], 1)).astype(_BF16) for j in chunks]
        parts = [[lax.dot_general(a, b, _NT, preferred_element_type=_F32)
                  for a, b in zip(scaled_q[j], scaled_k[j])] for j in chunks]
        attn = []
        for j in chunks:
            total = jnp.where(level == 0, parts[j][0], 0.0)
            for i, h in enumerate(_B_LEVELS):
                total = total + jnp.where(level == h, parts[j][1 + i], 0.0)
            attn.append(total.astype(_BF16))
        intra = [jnp.dot(attn[j], vc[j], preferred_element_type=_F32) for j in chunks]
        update = [lax.dot_general(vc[j], k_out[j], _TN, preferred_element_type=_F32) for j in chunks]

        outs = []
        for j in chunks:
            outs.append(intra[j] + lax.dot_general(q_in[j], state_t.astype(_BF16), _NT,
                                                   preferred_element_type=_F32))
            state_t = state_t * decay[j][c - 1:c] + update[j]

        y = _rmsnorm_rows(jnp.concatenate(outs, axis=0), gain)
        graw = g_ref[0, rows, :]
        y = y * (graw * _sigmoid(graw))
        o_ref[0, rows, :] = y.astype(o_ref.dtype)
        return state_t

    lax.fori_loop(0, seq // span, step, jnp.zeros((B_DIM, B_DIM), _F32))


def _hgrn2(proj, lower_bound, gain, sums, level, group):
    bsz, seq, _ = proj.shape
    assert group % 2 == 0 and seq % (group * B_CHUNK) == 0
    head = lambda off: pl.BlockSpec((1, seq, B_DIM), lambda b, h, off=off: (b, 0, off + h))
    blocks = 4 * _nbytes((seq, B_DIM), _F32) + _nbytes((seq, B_DIM), _BF16) \
        + _nbytes(sums.shape, _BF16) + _nbytes(level.shape, jnp.int32) + 2 * _nbytes((8, LANES), _F32)
    return pl.pallas_call(
        functools.partial(_hgrn_kernel, group=group),
        grid=(bsz, B_HEADS),
        in_specs=[head(0), head(B_HEADS), head(2 * B_HEADS), head(3 * B_HEADS),
                  pl.BlockSpec((1, 1, B_DIM), lambda b, h: (h, 0, 0)),
                  pl.BlockSpec((1, B_DIM), lambda b, h: (0, 0)),
                  pl.BlockSpec(sums.shape, lambda b, h: (0, 0)),
                  pl.BlockSpec(level.shape, lambda b, h: (0, 0))],
        out_specs=pl.BlockSpec((1, seq, B_DIM), lambda b, h: (b, 0, h)),
        out_shape=jax.ShapeDtypeStruct((bsz, seq, B_WIDTH), _BF16),
        compiler_params=pltpu.CompilerParams(
            dimension_semantics=("arbitrary", "arbitrary"), vmem_limit_bytes=_vmem_limit(blocks)),
        name="hgrn2",
    )(proj, proj, proj, proj, lower_bound.reshape(B_HEADS, 1, B_DIM), gain.reshape(1, B_DIM), sums, level)


def _swa_kernel(q_ref, k_ref, v_ref, sink_ref, bias_ref, o_ref, klo_ref, khi_ref, vlo_ref, vhi_ref, *, joint):
    seq = q_ref.shape[1]
    kv_head = pl.program_id(1)
    lane_half = lax.broadcasted_iota(jnp.int32, (seq, LANES), 1) // C_DIM
    odd_head = (kv_head % 2) == 1

    def place(x):
        own = jnp.where(lane_half == kv_head % 2, x, 0.0)
        other = pltpu.roll(own, C_DIM, 1)
        lo = jnp.where(odd_head, other, own)
        hi = jnp.where(odd_head, own, other)
        return lo.astype(_BF16), hi.astype(_BF16)

    klo_ref[...], khi_ref[...] = place(k_ref[0])
    vlo_ref[...], vhi_ref[...] = place(v_ref[0])
    pairs = C_GROUP // 2

    heads = [(j, side) for side in range(2) for j in range(pairs)]
    sinks = [sink_ref[kv_head * C_GROUP + 2 * j + side] * LOG2E for j, side in heads]
    low_q = lax.broadcasted_iota(jnp.int32, (BAND, LANES), 1) < C_DIM
    k_refs, v_refs = (klo_ref, khi_ref), (vlo_ref, vhi_ref)

    def attend(block_ids, with_prev):
        curs = [pl.ds(pl.multiple_of(n * BAND, BAND), BAND) for n in block_ids]
        if with_prev:
            keys = [pl.ds(pl.multiple_of((n - 1) * BAND, BAND), 2 * BAND) for n in block_ids]
            bias = bias_ref[...]
        else:
            keys = curs
            bias = bias_ref[:, BAND:]
        count = range(len(block_ids))
        q4 = [jnp.concatenate([q_ref[0, curs[b], j * LANES:(j + 1) * LANES] for j in range(pairs)],
                              axis=0).astype(_BF16) for b in count]
        s_side = [[lax.dot_general(q4[b], k_refs[side][keys[b], :], _NT, preferred_element_type=_F32)
                   for side in range(2)] for b in count]
        s = [[s_side[b][side][j * BAND:(j + 1) * BAND] + bias for j, side in heads] for b in count]
        m = [[jnp.maximum(jnp.max(s_h, axis=-1, keepdims=True), sink) for s_h, sink in zip(s[b], sinks)]
             for b in count]
        p = [[jnp.exp2(s_h - m_h) for s_h, m_h in zip(s[b], m[b])] for b in count]
        sums = [[jnp.sum(p_h, axis=-1, keepdims=True) for p_h in p[b]] for b in count]
        num = [[jnp.dot(jnp.concatenate(p[b][side * pairs:(side + 1) * pairs], axis=0).astype(_BF16),
                        v_refs[side][keys[b], :], preferred_element_type=_F32) for side in range(2)]
               for b in count]
        both = lambda lo, hi: jnp.where(low_q, lo, hi)
        den = [[both(sums[b][j], sums[b][pairs + j])
                + jnp.exp2(both(sinks[j], sinks[pairs + j]) - both(m[b][j], m[b][pairs + j]))
                for j in range(pairs)] for b in count]
        for b in count:
            for j in range(pairs):
                rows = slice(j * BAND, (j + 1) * BAND)
                out = both(num[b][0][rows], num[b][1][rows]) / den[b][j]
                o_ref[0, curs[b], j * LANES:(j + 1) * LANES] = out.astype(o_ref.dtype)

    attend([0], False)
    later = seq // BAND - 1
    rounds = later // joint

    def round_(t, carry):
        attend([1 + t * joint + b for b in range(joint)], True)
        return carry

    lax.fori_loop(0, rounds, round_, 0)
    if later % joint:
        attend(list(range(1 + rounds * joint, later + 1)), True)


def _swa(proj, sinks, bias, joint):
    bsz, seq, _ = proj.shape
    qw = C_GROUP * C_DIM
    k_base = C_Q_HEADS * C_DIM // LANES
    v_base = k_base + C_KV_HEADS * C_DIM // LANES
    blocks = _nbytes((seq, qw), _F32) + 2 * _nbytes((seq, LANES), _F32) + _nbytes((seq, qw), _BF16) \
        + _nbytes(bias.shape, _F32)
    scratch = 4 * _nbytes((seq, LANES), _BF16)
    return pl.pallas_call(
        functools.partial(_swa_kernel, joint=joint),
        grid=(bsz, C_KV_HEADS),
        in_specs=[pl.BlockSpec((1, seq, qw), lambda b, h: (b, 0, h)),
                  pl.BlockSpec((1, seq, LANES), lambda b, h: (b, 0, k_base + h // 2)),
                  pl.BlockSpec((1, seq, LANES), lambda b, h: (b, 0, v_base + h // 2)),
                  pl.BlockSpec(memory_space=pltpu.SMEM),
                  pl.BlockSpec(bias.shape, lambda b, h: (0, 0))],
        out_specs=pl.BlockSpec((1, seq, qw), lambda b, h: (b, 0, h)),
        out_shape=jax.ShapeDtypeStruct((bsz, seq, C_Q_HEADS * C_DIM), _BF16),
        scratch_shapes=[pltpu.VMEM((seq, LANES), _BF16) for _ in range(4)],
        compiler_params=pltpu.CompilerParams(
            dimension_semantics=("arbitrary", "arbitrary"),
            vmem_limit_bytes=_vmem_limit(blocks, scratch)),
        name="swa",
    )(proj, proj, proj, sinks.astype(_F32), bias)


def kernel(x, norm_mix_g, norm_mlp_g, final_norm_g, even_w_in, even_w_out, hgrn_lb_raw, hgrn_norm_g,
           odd_w_qkv, odd_b_qkv, odd_sinks, odd_w_o, odd_b_o, mlp_w1, mlp_w2):
    bsz, seq, d = x.shape
    depth = norm_mix_g.shape[0]
    t = _tiles()
    m = bsz * seq

    tables_a = _rope_tables(seq, A_DIM)
    tables_c = _rope_tables(seq, C_DIM)
    bias_a = jnp.asarray(_band_bias_window(BAND))
    bias_a4 = jnp.asarray(_by4_bias(seq // 4))
    bias_c = jnp.asarray(_band_bias_window(C_WINDOW - 1))
    sums_np, level_np = _hgrn_constants()
    sums, level = jnp.asarray(sums_np, _BF16), jnp.asarray(level_np)

    lb_soft = jax.nn.softmax(hgrn_lb_raw.astype(_F32), axis=0)
    lower_bounds = jnp.cumsum(lb_soft, axis=0) - lb_soft[0:1]

    w_in_a = even_w_in[:, :, :A_IN].astype(_BF16)
    w_in_b = even_w_in[:, :, A_IN:].astype(_BF16)
    w_out = even_w_out.astype(_BF16)
    w_qkv = odd_w_qkv.astype(_BF16)
    w_o = odd_w_o.astype(_BF16)
    w1 = mlp_w1.astype(_BF16)
    w2 = mlp_w2.astype(_BF16)

    group_a = A_WIDTH // LANES
    rope_a = [A_DIM ** -0.5 * LOG2E] * group_a + [1.0] * group_a + [None] * group_a
    q_groups = C_Q_HEADS * C_DIM // LANES
    kv_groups = C_KV_HEADS * C_DIM // LANES
    rope_c = [C_DIM ** -0.5 * LOG2E] * q_groups + [1.0] * kv_groups + [None] * kv_groups
    zeros_d = jnp.zeros((d,), _F32)

    x2 = x.reshape(m, d)
    for layer in range(depth):
        if layer % 2 == 0:
            e = layer // 2
            proj_a, h_mix = _norm_proj(x2, norm_mix_g[layer], w_in_a, e, A_IN, jnp.zeros((A_IN,), _F32),
                                       tables_a, rope_a, seq, t["proj_tm"], emit_h=True)
            proj_a = proj_a.reshape(bsz, seq, A_IN)
            proj_b = _proj(h_mix, w_in_b, e, t["proj_tm"]).reshape(bsz, seq, B_IN)
            oa = _dilated_attention(proj_a, bias_a, bias_a4, t["dilated_group"]).reshape(m, A_WIDTH)
            ob = _hgrn2(proj_b, lower_bounds[e], hgrn_norm_g[e], sums, level, t["hgrn_group"]).reshape(m, B_WIDTH)
            x2, h2 = _res_matmul([oa, ob], w_out, e, zeros_d, x2, norm_mlp_g[layer], t["out_tm"])
        else:
            o = layer // 2
            proj = _norm_proj(x2, norm_mix_g[layer], w_qkv, o, C_QKV, odd_b_qkv[o],
                              tables_c, rope_c, seq, t["proj_tm"], emit_h=False).reshape(bsz, seq, C_QKV)
            attn = _swa(proj, odd_sinks[o], bias_c, t["swa_joint"]).reshape(m, C_Q_HEADS * C_DIM)
            x2, h2 = _res_matmul([attn], w_o, o, odd_b_o[o], x2, norm_mlp_g[layer], t["out_tm"])
        last = layer == depth - 1
        x2 = _mlp(x2, h2, w1, w2, layer, final_norm_g if last else None, t["mlp_tm"], t["mlp_tf"])
    return x2.reshape(bsz, seq, d)
```

```python
import functools
import math

import numpy as np
import jax
import jax.numpy as jnp
from jax import lax
from jax.experimental import pallas as pl
from jax.experimental.pallas import tpu as pltpu

NORM_EPS = 1e-5
ROPE_THETA = 500000.0
ROPE_FRACTION = 4
BAND = 128

A_DIM = 128
A_HEADS = 8
A_BRANCHES = ((128, 1), (512, 4), (2048, 16))
A_WIDTH = A_HEADS * A_DIM
B_DIM = 128
B_HEADS = 8
B_WIDTH = B_HEADS * B_DIM
B_CHUNK = 64
A_IN = 3 * A_WIDTH
B_IN = 4 * B_WIDTH

C_DIM = 64
C_Q_HEADS = 32
C_KV_HEADS = 4
C_GROUP = C_Q_HEADS // C_KV_HEADS
C_WINDOW = 128
C_QKV = (C_Q_HEADS + 2 * C_KV_HEADS) * C_DIM

LANES = 128
V7X_VMEM_BYTES = 64 * 1024 * 1024
COMPILER_TEMP_BYTES = 6 * 1024 * 1024

MASKED = -1e30
LOG2E = math.log2(math.e)

_BF16 = jnp.bfloat16
_F32 = jnp.float32
_NT = (((1,), (1,)), ((), ()))
_TN = (((0,), (0,)), ((), ()))


def _tiles():
    return dict(
        proj_tm=512,
        out_tm=512,
        mlp_tm=1024, mlp_tf=1024,
        dilated_group=16,
        hgrn_group=32,
        swa_joint=3,
    )


def _vmem_limit(pipelined_bytes, single_bytes=0):
    need = 2 * pipelined_bytes + single_bytes + COMPILER_TEMP_BYTES
    assert need <= V7X_VMEM_BYTES, need
    return int(need)


def _nbytes(shape, dtype):
    return int(np.prod(shape)) * jnp.dtype(dtype).itemsize


def _rmsnorm_rows(x, g):
    ms = jnp.mean(x * x, axis=-1, keepdims=True)
    return x * lax.rsqrt(ms + NORM_EPS) * g


def _sigmoid(x):
    return 1.0 / (1.0 + jnp.exp2(x * (-LOG2E)))


def _rope_tables(seq, head_dim):
    rot = head_dim // ROPE_FRACTION
    half = rot // 2
    inv_freq = 1.0 / (ROPE_THETA ** (jnp.arange(0, rot, 2, dtype=_F32) / rot))
    ang = jnp.arange(seq, dtype=_F32)[:, None] * inv_freq[None, :]
    cos, sin = jnp.cos(ang), jnp.sin(ang)
    pad = head_dim - 2 * half
    ones = jnp.ones((seq, pad), _F32)
    zeros = jnp.zeros((seq, pad), _F32)
    zh = jnp.zeros((seq, half), _F32)
    reps = LANES // head_dim
    cos_full = jnp.tile(jnp.concatenate([cos, cos, ones], axis=1), (1, reps))
    sin_lo = jnp.tile(jnp.concatenate([-sin, zh, zeros], axis=1), (1, reps))
    sin_hi = jnp.tile(jnp.concatenate([zh, sin, zeros], axis=1), (1, reps))
    return cos_full, sin_lo, sin_hi, half


def _rope(x, cos_full, sin_lo, sin_hi, half):
    up = pltpu.roll(x, LANES - half, 1)
    down = pltpu.roll(x, half, 1)
    return x * cos_full + up * sin_lo + down * sin_hi


def _band_bias_window(window):
    i = np.arange(BAND)[:, None]
    j = np.arange(2 * BAND)[None, :]
    dist = i + BAND - j
    return np.where((dist >= 0) & (dist <= window), 0.0, MASKED).astype(np.float32)


def _norm_proj_kernel(*refs, rope_scales, half, emit_h):
    refs = list(refs)
    h_ref = refs.pop() if emit_h else None
    x_ref, g_ref, w_ref, b_ref, cos_ref, slo_ref, shi_ref, o_ref = refs
    cos_full, sin_lo, sin_hi = cos_ref[...], slo_ref[...], shi_ref[...]
    h = _rmsnorm_rows(x_ref[...], g_ref[...]).astype(_BF16)
    if emit_h:
        h_ref[...] = h
    acc = jnp.dot(h, w_ref[0], preferred_element_type=_F32) + b_ref[...]
    for gi, scale in enumerate(rope_scales):
        sl = slice(gi * LANES, (gi + 1) * LANES)
        if scale is None:
            o_ref[:, sl] = acc[:, sl]
        else:
            o_ref[:, sl] = _rope(acc[:, sl], cos_full, sin_lo, sin_hi, half) * scale


def _norm_proj(x, g, w_stack, layer, n, b, tables, rope_scales, seq, tm, emit_h):
    m, d = x.shape
    assert len(rope_scales) * LANES == n
    cos_full, sin_lo, sin_hi, half = tables
    per_seq = seq // tm
    in_specs = [
        pl.BlockSpec((tm, d), lambda i: (i, 0)),
        pl.BlockSpec((1, d), lambda i: (0, 0)),
        pl.BlockSpec((1, d, n), lambda i: (layer, 0, 0), pipeline_mode=pl.Buffered(1)),
        pl.BlockSpec((1, n), lambda i: (0, 0)),
    ] + [pl.BlockSpec((tm, LANES), lambda i: (i % per_seq, 0))] * 3
    blocks = (_nbytes((tm, d), _F32) + _nbytes((tm, n), _F32) + _nbytes((1, d), _F32) + _nbytes((1, n), _F32)
              + 3 * _nbytes((tm, LANES), _F32))
    out_specs = [pl.BlockSpec((tm, n), lambda i: (i, 0))]
    out_shape = [jax.ShapeDtypeStruct((m, n), _F32)]
    if emit_h:
        out_specs.append(pl.BlockSpec((tm, d), lambda i: (i, 0)))
        out_shape.append(jax.ShapeDtypeStruct((m, d), _BF16))
        blocks += _nbytes((tm, d), _BF16)
    single = _nbytes((d, n), _BF16) + _nbytes((tm, d), _BF16) + _nbytes((tm, n), _F32)
    outs = pl.pallas_call(
        functools.partial(_norm_proj_kernel, rope_scales=tuple(rope_scales), half=half, emit_h=emit_h),
        grid=(m // tm,),
        in_specs=in_specs,
        out_specs=out_specs,
        out_shape=out_shape,
        compiler_params=pltpu.CompilerParams(
            dimension_semantics=("arbitrary",), vmem_limit_bytes=_vmem_limit(blocks, single)),
        name="norm_proj",
    )(x, g.reshape(1, d), w_stack, b.reshape(1, n), cos_full, sin_lo, sin_hi)
    return outs if emit_h else outs[0]


def _proj_kernel(h_ref, w_ref, o_ref):
    o_ref[...] = jnp.dot(h_ref[...], w_ref[0], preferred_element_type=_F32)


def _proj(h, w_stack, layer, tm):
    m, d = h.shape
    n = w_stack.shape[2]
    blocks = _nbytes((tm, d), _BF16) + _nbytes((tm, n), _F32)
    return pl.pallas_call(
        _proj_kernel,
        grid=(m // tm,),
        in_specs=[pl.BlockSpec((tm, d), lambda i: (i, 0)),
                  pl.BlockSpec((1, d, n), lambda i: (layer, 0, 0), pipeline_mode=pl.Buffered(1))],
        out_specs=pl.BlockSpec((tm, n), lambda i: (i, 0)),
        out_shape=jax.ShapeDtypeStruct((m, n), _F32),
        compiler_params=pltpu.CompilerParams(
            dimension_semantics=("arbitrary",),
            vmem_limit_bytes=_vmem_limit(blocks, _nbytes((d, n), _BF16) + _nbytes((tm, n), _F32))),
        name="proj",
    )(h, w_stack)


def _res_matmul_kernel(*refs, n_in):
    a_refs, w_refs = refs[:n_in], refs[n_in:2 * n_in]
    b_ref, r_ref, g_ref, o_ref, h_ref = refs[2 * n_in:]
    acc = r_ref[...] + b_ref[...]
    for a_ref, w_ref in zip(a_refs, w_refs):
        acc = acc + jnp.dot(a_ref[...], w_ref[0], preferred_element_type=_F32)
    o_ref[...] = acc
    h_ref[...] = _rmsnorm_rows(acc, g_ref[...]).astype(_BF16)


def _res_matmul(a_list, w_stack, layer, b, res, g_next, tm):
    m, n = res.shape
    k = a_list[0].shape[1]
    blocks = (_nbytes((tm, n), _F32) * 2 + _nbytes((tm, n), _BF16) + 2 * _nbytes((1, n), _F32)
              + len(a_list) * _nbytes((tm, k), _BF16))
    in_specs = [pl.BlockSpec((tm, k), lambda i: (i, 0)) for _ in a_list]
    in_specs += [pl.BlockSpec((1, k, n), lambda i, r=r: (layer, r, 0), pipeline_mode=pl.Buffered(1))
                 for r in range(len(a_list))]
    in_specs += [pl.BlockSpec((1, n), lambda i: (0, 0)), pl.BlockSpec((tm, n), lambda i: (i, 0)),
                 pl.BlockSpec((1, n), lambda i: (0, 0))]
    return pl.pallas_call(
        functools.partial(_res_matmul_kernel, n_in=len(a_list)),
        grid=(m // tm,),
        in_specs=in_specs,
        out_specs=[pl.BlockSpec((tm, n), lambda i: (i, 0)), pl.BlockSpec((tm, n), lambda i: (i, 0))],
        out_shape=[jax.ShapeDtypeStruct((m, n), _F32), jax.ShapeDtypeStruct((m, n), _BF16)],
        compiler_params=pltpu.CompilerParams(
            dimension_semantics=("arbitrary",),
            vmem_limit_bytes=_vmem_limit(blocks, len(a_list) * _nbytes((k, n), _BF16))),
        name="res_matmul",
    )(*a_list, *([w_stack] * len(a_list)), b.reshape(1, n), res, g_next.reshape(1, n))


def _mlp_kernel(*refs, final):
    if final:
        x_ref, h_ref, w1_ref, w2_ref, gf_ref, o_ref = refs
    else:
        x_ref, h_ref, w1_ref, w2_ref, o_ref = refs
    j = pl.program_id(1)
    a = jnp.dot(h_ref[...], w1_ref[0], preferred_element_type=_F32)
    a = jnp.square(jnp.maximum(a, 0.0)).astype(_BF16)
    so_far = jnp.where(j == 0, 0.0, o_ref[...])
    o_ref[...] = so_far + jnp.dot(a, w2_ref[0], preferred_element_type=_F32)
    piece = x_ref.shape[0]
    rows = pl.ds(pl.multiple_of(j * piece, piece), piece)
    o_ref[rows, :] += x_ref[...]
    if final:
        @pl.when(j == pl.num_programs(1) - 1)
        def _():
            o_ref[...] = _rmsnorm_rows(o_ref[...], gf_ref[...])


def _mlp(x, h, w1_stack, w2_stack, layer, final_gain, tm, tf):
    m, d = x.shape
    f = w1_stack.shape[2]
    n_tiles = f // tf
    piece = tm // n_tiles
    blocks = (_nbytes((tm, d), _F32) + _nbytes((piece, d), _F32) + _nbytes((tm, d), _BF16) + _nbytes((1, d), _F32)
              + _nbytes((d, tf), _BF16) + _nbytes((tf, d), _BF16))
    temporaries = _nbytes((tm, tf), _F32) + _nbytes((tm, tf), _BF16)
    in_specs = [
        pl.BlockSpec((piece, d), lambda i, j: (i * n_tiles + j, 0)),
        pl.BlockSpec((tm, d), lambda i, j: (i, 0)),
        pl.BlockSpec((1, d, tf), lambda i, j: (layer, 0, j)),
        pl.BlockSpec((1, tf, d), lambda i, j: (layer, j, 0)),
    ]
    args = [x, h, w1_stack, w2_stack]
    if final_gain is not None:
        in_specs.append(pl.BlockSpec((1, d), lambda i, j: (0, 0)))
        args.append(final_gain.reshape(1, d))
    return pl.pallas_call(
        functools.partial(_mlp_kernel, final=final_gain is not None),
        grid=(m // tm, f // tf),
        in_specs=in_specs,
        out_specs=pl.BlockSpec((tm, d), lambda i, j: (i, 0)),
        out_shape=jax.ShapeDtypeStruct((m, d), _F32),
        compiler_params=pltpu.CompilerParams(
            dimension_semantics=("arbitrary", "arbitrary"),
            vmem_limit_bytes=_vmem_limit(blocks, temporaries)),
        name="mlp",
    )(*args)


def _by4_bias(class_len):
    assert A_BRANCHES == ((128, 1), (512, 4), (2048, 16))
    blocks = class_len // BAND
    i = np.arange(BAND)[:, None]
    j = np.arange(BAND)[None, :]
    tiles = []
    for off in range(blocks - 1, -1, -1):
        dist = off * BAND + i - j
        in_d4 = (dist >= 0) & (dist <= A_BRANCHES[1][0] // A_BRANCHES[1][1])
        in_d16 = (dist >= 0) & (dist % 4 == 0) & (dist // 4 <= A_BRANCHES[2][0] // A_BRANCHES[2][1])
        mult = in_d4.astype(np.int64) + in_d16.astype(np.int64)
        tiles.append(np.where(mult > 0, np.log2(np.maximum(mult, 1)), MASKED))
    return np.concatenate(tiles, axis=1).astype(np.float32)


def _dilated_kernel(q_ref, k_ref, v_ref, bias_ref, bias4_ref, o_ref,
                    qp_ref, kp_ref, vp_ref, acc_ref, m_ref, l_ref, acc4_ref, m4_ref, l4_ref, *, group):
    seq = acc_ref.shape[0]
    n_all = seq // BAND
    quarter = seq // 4
    n_class = quarter // BAND
    bias_full = bias_ref[...]
    col = lax.broadcasted_iota(jnp.int32, (BAND, 2 * BAND), 1)
    bias_first = jnp.where(col < BAND, MASKED, bias_full)
    batch_nt = (((2,), (2,)), ((0,), (0,)))
    batch_nn = (((2,), (1,)), ((0,), (0,)))

    def by4_rows(c):
        return pl.ds(c, quarter, stride=4)

    def softmax_parts(q, k, v, biases):
        s = lax.dot_general(q, k, batch_nt, preferred_element_type=_F32)
        s = jnp.stack([s[g] + biases[g] for g in range(len(biases))])
        m = jnp.max(s, axis=-1, keepdims=True)
        p = jnp.exp2(s - m)
        l = jnp.sum(p, axis=-1, keepdims=True)
        num = lax.dot_general(p.astype(_BF16), v, batch_nn, preferred_element_type=_F32)
        shape = num.shape
        return num, jnp.broadcast_to(m, shape), jnp.broadcast_to(l, shape)

    qp_ref[...] = q_ref[0].astype(_BF16)
    kp_ref[...] = k_ref[0].astype(_BF16)
    vp_ref[...] = v_ref[0].astype(_BF16)
    for g0 in range(0, n_all, group):
        cur = slice(g0 * BAND, (g0 + group) * BAND)
        if g0 == 0:
            prev_of = lambda ref: jnp.concatenate([ref[0:BAND, :], ref[0:(group - 1) * BAND, :]], axis=0)
        else:
            prev_of = lambda ref: ref[(g0 - 1) * BAND:(g0 + group - 1) * BAND, :]
        shape = (group, BAND, A_DIM)
        k = jnp.concatenate([prev_of(kp_ref).reshape(shape), kp_ref[cur, :].reshape(shape)], axis=1)
        v = jnp.concatenate([prev_of(vp_ref).reshape(shape), vp_ref[cur, :].reshape(shape)], axis=1)
        biases = [bias_first if g0 + g == 0 else bias_full for g in range(group)]
        num, m, l = softmax_parts(qp_ref[cur, :].reshape(shape), k, v, biases)
        acc_ref[cur, :] = num.reshape(group * BAND, A_DIM)
        m_ref[cur, :] = m.reshape(group * BAND, A_DIM)
        l_ref[cur, :] = l.reshape(group * BAND, A_DIM)

    for c in range(4):
        seg = slice(c * quarter, (c + 1) * quarter)
        qp_ref[seg, :] = q_ref[0, by4_rows(c), :].astype(_BF16)
        kp_ref[seg, :] = k_ref[0, by4_rows(c), :].astype(_BF16)
        vp_ref[seg, :] = v_ref[0, by4_rows(c), :].astype(_BF16)
    bias4 = bias4_ref[...]
    parts = []
    for n in range(n_class):
        q = jnp.stack([qp_ref[c * quarter + n * BAND:c * quarter + (n + 1) * BAND, :] for c in range(4)])
        k = jnp.stack([kp_ref[c * quarter:c * quarter + (n + 1) * BAND, :] for c in range(4)])
        v = jnp.stack([vp_ref[c * quarter:c * quarter + (n + 1) * BAND, :] for c in range(4)])
        parts.append(softmax_parts(q, k, v, [bias4[:, (n_class - 1 - n) * BAND:]] * 4))
    for n, (num, m, l) in enumerate(parts):
        for c in range(4):
            rows = slice(c * quarter + n * BAND, c * quarter + (n + 1) * BAND)
            acc4_ref[rows, :] = num[c]
            m4_ref[rows, :] = m[c]
            l4_ref[rows, :] = l[c]

    for c in range(4):
        seg = slice(c * quarter, (c + 1) * quarter)
        rows = by4_rows(c)
        m_a, m_b = m_ref[rows, :], m4_ref[seg, :]
        m_new = jnp.maximum(m_a, m_b)
        a = jnp.exp2(m_a - m_new)
        b = jnp.exp2(m_b - m_new)
        num = acc_ref[rows, :] * a + acc4_ref[seg, :] * b
        den = l_ref[rows, :] * a + l4_ref[seg, :] * b
        acc_ref[rows, :] = num / den
    o_ref[0] = acc_ref[...].astype(o_ref.dtype)


def _dilated_attention(proj, bias, bias4, group):
    bsz, seq, _ = proj.shape
    head = lambda off: pl.BlockSpec((1, seq, A_DIM), lambda b, h, off=off: (b, 0, off + h))
    blocks = (3 * _nbytes((seq, A_DIM), _F32) + _nbytes((seq, A_DIM), _BF16)
              + _nbytes(bias.shape, _F32) + _nbytes(bias4.shape, _F32))
    scratch = 6 * _nbytes((seq, LANES), _F32) + 3 * _nbytes((seq, LANES), _BF16)
    temporaries = 2 * group * _nbytes((BAND, 2 * BAND), _F32)
    return pl.pallas_call(
        functools.partial(_dilated_kernel, group=group),
        grid=(bsz, A_HEADS),
        in_specs=[head(0), head(A_HEADS), head(2 * A_HEADS),
                  pl.BlockSpec(bias.shape, lambda b, h: (0, 0)),
                  pl.BlockSpec(bias4.shape, lambda b, h: (0, 0))],
        out_specs=pl.BlockSpec((1, seq, A_DIM), lambda b, h: (b, 0, h)),
        out_shape=jax.ShapeDtypeStruct((bsz, seq, A_WIDTH), _BF16),
        scratch_shapes=[pltpu.VMEM((seq, A_DIM), _BF16) for _ in range(3)]
        + [pltpu.VMEM((seq, LANES), _F32) for _ in range(6)],
        compiler_params=pltpu.CompilerParams(
            dimension_semantics=("arbitrary", "arbitrary"),
            vmem_limit_bytes=_vmem_limit(blocks, scratch + temporaries)),
        name="dilated",
    )(proj, proj, proj, bias, bias4)


_B_LEVELS = (32, 16, 8, 4, 2, 1)


def _hgrn_constants():
    c = B_CHUNK
    t = np.arange(c)[:, None]
    r = np.arange(c)[None, :]
    mats = [(r <= t), (r > t)]
    for h in _B_LEVELS:
        base = (t // (2 * h)) * (2 * h)
        mid = base + h - 1
        upper = (t & h) != 0
        mats.append(np.where(upper, (r > mid) & (r <= t), (r > t) & (r <= mid)))
    sums = np.concatenate(mats, axis=0).astype(np.float32)
    s = np.arange(c)[None, :]
    x = t ^ s
    level = np.where(s > t, -1, np.where(s == t, 0, 1 << (np.floor(np.log2(np.maximum(x, 1))).astype(np.int64))))
    return np.concatenate([sums, sums], axis=1), level.astype(np.int32)


def _hgrn_kernel(q_ref, f_ref, i_ref, g_ref, lb_ref, gain_ref, sums_ref, level_ref, o_ref, *, group):
    seq = q_ref.shape[1]
    c = B_CHUNK
    span = group * c
    lb = lb_ref[0]
    gain = gain_ref[...]
    sums2 = sums_ref[...]
    level = level_ref[...]
    chunks = range(group)

    def piece(x, j):
        return x[j * c:(j + 1) * c]

    def step(n, state_t):
        rows = pl.ds(pl.multiple_of(n * span, span), span)
        gate = lb + (1.0 - lb) * _sigmoid(f_ref[0, rows, :])
        log_gate = jnp.log2(gate)
        k = 1.0 - gate
        qraw = q_ref[0, rows, :]
        q = qraw * _sigmoid(qraw) * (B_DIM ** -0.5)
        v = i_ref[0, rows, :].astype(_BF16)

        g_hi = log_gate.astype(_BF16)
        g_lo = (log_gate - g_hi.astype(_F32)).astype(_BF16)
        hi_lo = [jnp.concatenate([piece(g_hi, j), piece(g_lo, j)], axis=0) for j in chunks]
        expo = []
        for j in range(0, group, 2):
            both = jnp.dot(sums2, jnp.concatenate([hi_lo[j], hi_lo[j + 1]], axis=1),
                           preferred_element_type=_F32)
            expo += [both[:, :B_DIM], both[:, B_DIM:]]
        decay = [jnp.exp2(e) for e in expo]

        qc = [piece(q, j) for j in chunks]
        kc = [piece(k, j) for j in chunks]
        vc = [piece(v, j) for j in chunks]
        q_bf = [qc[j].astype(_BF16) for j in chunks]
        k_bf = [kc[j].astype(_BF16) for j in chunks]
        level_decay = [[piece(decay[j], 2 + i).astype(_BF16) for i in range(len(_B_LEVELS))] for j in chunks]
        scaled_q = [[q_bf[j]] + [q_bf[j] * d for d in level_decay[j]] for j in chunks]
        scaled_k = [[k_bf[j]] + [k_bf[j] * d for d in level_decay[j]] for j in chunks]
        q_in = [(qc[j] * piece(decay[j], 0)).astype(_BF16) for j in chunks]
        k_out = [(kc[j] * piece(decay[j], 1)).astype(_BF16) for j in chunks]
        parts = [[lax.dot_general(a, b, _NT, preferred_element_type=_F32)
                  for a, b in zip(scaled_q[j], scaled_k[j])] for j in chunks]
        attn = []
        for j in chunks:
            total = jnp.where(level == 0, parts[j][0], 0.0)
            for i, h in enumerate(_B_LEVELS):
                total = total + jnp.where(level == h, parts[j][1 + i], 0.0)
            attn.append(total.astype(_BF16))
        intra = [jnp.dot(attn[j], vc[j], preferred_element_type=_F32) for j in chunks]
        update = [lax.dot_general(vc[j], k_out[j], _TN, preferred_element_type=_F32) for j in chunks]

        outs = []
        for j in chunks:
            outs.append(intra[j] + lax.dot_general(q_in[j], state_t.astype(_BF16), _NT,
                                                   preferred_element_type=_F32))
            state_t = state_t * decay[j][c - 1:c] + update[j]

        y = _rmsnorm_rows(jnp.concatenate(outs, axis=0), gain)
        graw = g_ref[0, rows, :]
        y = y * (graw * _sigmoid(graw))
        o_ref[0, rows, :] = y.astype(o_ref.dtype)
        return state_t

    lax.fori_loop(0, seq // span, step, jnp.zeros((B_DIM, B_DIM), _F32))


def _hgrn2(proj, lower_bound, gain, sums, level, group):
    bsz, seq, _ = proj.shape
    assert group % 2 == 0 and seq % (group * B_CHUNK) == 0
    head = lambda off: pl.BlockSpec((1, seq, B_DIM), lambda b, h, off=off: (b, 0, off + h))
    blocks = 4 * _nbytes((seq, B_DIM), _F32) + _nbytes((seq, B_DIM), _BF16) \
        + _nbytes(sums.shape, _BF16) + _nbytes(level.shape, jnp.int32) + 2 * _nbytes((8, LANES), _F32)
    temporaries = 3 * group * _nbytes((sums.shape[0], B_DIM), _F32)
    return pl.pallas_call(
        functools.partial(_hgrn_kernel, group=group),
        grid=(bsz, B_HEADS),
        in_specs=[head(0), head(B_HEADS), head(2 * B_HEADS), head(3 * B_HEADS),
                  pl.BlockSpec((1, 1, B_DIM), lambda b, h: (h, 0, 0)),
                  pl.BlockSpec((1, B_DIM), lambda b, h: (0, 0)),
                  pl.BlockSpec(sums.shape, lambda b, h: (0, 0)),
                  pl.BlockSpec(level.shape, lambda b, h: (0, 0))],
        out_specs=pl.BlockSpec((1, seq, B_DIM), lambda b, h: (b, 0, h)),
        out_shape=jax.ShapeDtypeStruct((bsz, seq, B_WIDTH), _BF16),
        compiler_params=pltpu.CompilerParams(
            dimension_semantics=("arbitrary", "arbitrary"),
            vmem_limit_bytes=_vmem_limit(blocks, temporaries)),
        name="hgrn2",
    )(proj, proj, proj, proj, lower_bound.reshape(B_HEADS, 1, B_DIM), gain.reshape(1, B_DIM), sums, level)


def _swa_kernel(q_ref, k_ref, v_ref, sink_ref, bias_ref, o_ref, klo_ref, khi_ref, vlo_ref, vhi_ref, *, joint):
    seq = q_ref.shape[1]
    kv_head = pl.program_id(1)
    lane_half = lax.broadcasted_iota(jnp.int32, (seq, LANES), 1) // C_DIM
    odd_head = (kv_head % 2) == 1

    def place(x):
        own = jnp.where(lane_half == kv_head % 2, x, 0.0)
        other = pltpu.roll(own, C_DIM, 1)
        lo = jnp.where(odd_head, other, own)
        hi = jnp.where(odd_head, own, other)
        return lo.astype(_BF16), hi.astype(_BF16)

    klo_ref[...], khi_ref[...] = place(k_ref[0])
    vlo_ref[...], vhi_ref[...] = place(v_ref[0])
    pairs = C_GROUP // 2

    heads = [(j, side) for side in range(2) for j in range(pairs)]
    sinks = [sink_ref[kv_head * C_GROUP + 2 * j + side] * LOG2E for j, side in heads]
    low_q = lax.broadcasted_iota(jnp.int32, (BAND, LANES), 1) < C_DIM
    k_refs, v_refs = (klo_ref, khi_ref), (vlo_ref, vhi_ref)

    def attend(block_ids, with_prev):
        curs = [pl.ds(pl.multiple_of(n * BAND, BAND), BAND) for n in block_ids]
        if with_prev:
            keys = [pl.ds(pl.multiple_of((n - 1) * BAND, BAND), 2 * BAND) for n in block_ids]
            bias = bias_ref[...]
        else:
            keys = curs
            bias = bias_ref[:, BAND:]
        count = range(len(block_ids))
        q4 = [jnp.concatenate([q_ref[0, curs[b], j * LANES:(j + 1) * LANES] for j in range(pairs)],
                              axis=0).astype(_BF16) for b in count]
        s_side = [[lax.dot_general(q4[b], k_refs[side][keys[b], :], _NT, preferred_element_type=_F32)
                   for side in range(2)] for b in count]
        s = [[s_side[b][side][j * BAND:(j + 1) * BAND] + bias for j, side in heads] for b in count]
        m = [[jnp.maximum(jnp.max(s_h, axis=-1, keepdims=True), sink) for s_h, sink in zip(s[b], sinks)]
             for b in count]
        p = [[jnp.exp2(s_h - m_h) for s_h, m_h in zip(s[b], m[b])] for b in count]
        sums = [[jnp.sum(p_h, axis=-1, keepdims=True) for p_h in p[b]] for b in count]
        num = [[jnp.dot(jnp.concatenate(p[b][side * pairs:(side + 1) * pairs], axis=0).astype(_BF16),
                        v_refs[side][keys[b], :], preferred_element_type=_F32) for side in range(2)]
               for b in count]
        both = lambda lo, hi: jnp.where(low_q, lo, hi)
        den = [[both(sums[b][j], sums[b][pairs + j])
                + jnp.exp2(both(sinks[j], sinks[pairs + j]) - both(m[b][j], m[b][pairs + j]))
                for j in range(pairs)] for b in count]
        for b in count:
            for j in range(pairs):
                rows = slice(j * BAND, (j + 1) * BAND)
                out = both(num[b][0][rows], num[b][1][rows]) / den[b][j]
                o_ref[0, curs[b], j * LANES:(j + 1) * LANES] = out.astype(o_ref.dtype)

    attend([0], False)
    later = seq // BAND - 1
    rounds = later // joint

    def round_(t, carry):
        attend([1 + t * joint + b for b in range(joint)], True)
        return carry

    lax.fori_loop(0, rounds, round_, 0)
    if later % joint:
        attend(list(range(1 + rounds * joint, later + 1)), True)


def _swa(proj, sinks, bias, joint):
    bsz, seq, _ = proj.shape
    qw = C_GROUP * C_DIM
    k_base = C_Q_HEADS * C_DIM // LANES
    v_base = k_base + C_KV_HEADS * C_DIM // LANES
    blocks = _nbytes((seq, qw), _F32) + 2 * _nbytes((seq, LANES), _F32) + _nbytes((seq, qw), _BF16) \
        + _nbytes(bias.shape, _F32)
    scratch = 4 * _nbytes((seq, LANES), _BF16)
    return pl.pallas_call(
        functools.partial(_swa_kernel, joint=joint),
        grid=(bsz, C_KV_HEADS),
        in_specs=[pl.BlockSpec((1, seq, qw), lambda b, h: (b, 0, h)),
                  pl.BlockSpec((1, seq, LANES), lambda b, h: (b, 0, k_base + h // 2)),
                  pl.BlockSpec((1, seq, LANES), lambda b, h: (b, 0, v_base + h // 2)),
                  pl.BlockSpec(memory_space=pltpu.SMEM),
                  pl.BlockSpec(bias.shape, lambda b, h: (0, 0))],
        out_specs=pl.BlockSpec((1, seq, qw), lambda b, h: (b, 0, h)),
        out_shape=jax.ShapeDtypeStruct((bsz, seq, C_Q_HEADS * C_DIM), _BF16),
        scratch_shapes=[pltpu.VMEM((seq, LANES), _BF16) for _ in range(4)],
        compiler_params=pltpu.CompilerParams(
            dimension_semantics=("arbitrary", "arbitrary"),
            vmem_limit_bytes=_vmem_limit(blocks, scratch)),
        name="swa",
    )(proj, proj, proj, sinks.astype(_F32), bias)


def kernel(x, norm_mix_g, norm_mlp_g, final_norm_g, even_w_in, even_w_out, hgrn_lb_raw, hgrn_norm_g,
           odd_w_qkv, odd_b_qkv, odd_sinks, odd_w_o, odd_b_o, mlp_w1, mlp_w2):
    bsz, seq, d = x.shape
    depth = norm_mix_g.shape[0]
    t = _tiles()
    m = bsz * seq

    tables_a = _rope_tables(seq, A_DIM)
    tables_c = _rope_tables(seq, C_DIM)
    bias_a = jnp.asarray(_band_bias_window(BAND))
    bias_a4 = jnp.asarray(_by4_bias(seq // 4))
    bias_c = jnp.asarray(_band_bias_window(C_WINDOW - 1))
    sums_np, level_np = _hgrn_constants()
    sums, level = jnp.asarray(sums_np, _BF16), jnp.asarray(level_np)

    lb_soft = jax.nn.softmax(hgrn_lb_raw.astype(_F32), axis=0)
    lower_bounds = jnp.cumsum(lb_soft, axis=0) - lb_soft[0:1]

    w_in_a = even_w_in[:, :, :A_IN].astype(_BF16)
    w_in_b = even_w_in[:, :, A_IN:].astype(_BF16)
    w_out = even_w_out.astype(_BF16)
    w_qkv = odd_w_qkv.astype(_BF16)
    w_o = odd_w_o.astype(_BF16)
    w1 = mlp_w1.astype(_BF16)
    w2 = mlp_w2.astype(_BF16)

    group_a = A_WIDTH // LANES
    rope_a = [A_DIM ** -0.5 * LOG2E] * group_a + [1.0] * group_a + [None] * group_a
    q_groups = C_Q_HEADS * C_DIM // LANES
    kv_groups = C_KV_HEADS * C_DIM // LANES
    rope_c = [C_DIM ** -0.5 * LOG2E] * q_groups + [1.0] * kv_groups + [None] * kv_groups
    zeros_d = jnp.zeros((d,), _F32)

    x2 = x.reshape(m, d)
    for layer in range(depth):
        if layer % 2 == 0:
            e = layer // 2
            proj_a, h_mix = _norm_proj(x2, norm_mix_g[layer], w_in_a, e, A_IN, jnp.zeros((A_IN,), _F32),
                                       tables_a, rope_a, seq, t["proj_tm"], emit_h=True)
            proj_a = proj_a.reshape(bsz, seq, A_IN)
            proj_b = _proj(h_mix, w_in_b, e, t["proj_tm"]).reshape(bsz, seq, B_IN)
            oa = _dilated_attention(proj_a, bias_a, bias_a4, t["dilated_group"]).reshape(m, A_WIDTH)
            ob = _hgrn2(proj_b, lower_bounds[e], hgrn_norm_g[e], sums, level, t["hgrn_group"]).reshape(m, B_WIDTH)
            x2, h2 = _res_matmul([oa, ob], w_out, e, zeros_d, x2, norm_mlp_g[layer], t["out_tm"])
        else:
            o = layer // 2
            proj = _norm_proj(x2, norm_mix_g[layer], w_qkv, o, C_QKV, odd_b_qkv[o],
                              tables_c, rope_c, seq, t["proj_tm"], emit_h=False).reshape(bsz, seq, C_QKV)
            attn = _swa(proj, odd_sinks[o], bias_c, t["swa_joint"]).reshape(m, C_Q_HEADS * C_DIM)
            x2, h2 = _res_matmul([attn], w_o, o, odd_b_o[o], x2, norm_mlp_g[layer], t["out_tm"])
        last = layer == depth - 1
        x2 = _mlp(x2, h2, w1, w2, layer, final_norm_g if last else None, t["mlp_tm"], t["mlp_tf"])
    return x2.reshape(bsz, seq, d)
```

```python
import functools
import math

import numpy as np
import jax
import jax.numpy as jnp
from jax import lax
from jax.experimental import pallas as pl
from jax.experimental.pallas import tpu as pltpu

NORM_EPS = 1e-5
ROPE_THETA = 500000.0
ROPE_FRACTION = 4
BAND = 128

A_DIM = 128
A_HEADS = 8
A_BRANCHES = ((128, 1), (512, 4), (2048, 16))
A_WIDTH = A_HEADS * A_DIM
B_DIM = 128
B_HEADS = 8
B_WIDTH = B_HEADS * B_DIM
B_CHUNK = 64
A_IN = 3 * A_WIDTH
B_IN = 4 * B_WIDTH

C_DIM = 64
C_Q_HEADS = 32
C_KV_HEADS = 4
C_GROUP = C_Q_HEADS // C_KV_HEADS
C_WINDOW = 128
C_QKV = (C_Q_HEADS + 2 * C_KV_HEADS) * C_DIM

LANES = 128
V7X_VMEM_BYTES = 64 * 1024 * 1024
COMPILER_TEMP_BYTES = 6 * 1024 * 1024

MASKED = -1e30
LOG2E = math.log2(math.e)

_BF16 = jnp.bfloat16
_F32 = jnp.float32
_NT = (((1,), (1,)), ((), ()))
_TN = (((0,), (0,)), ((), ()))


def _tiles():
    return dict(
        proj_tm=512,
        out_tm=512,
        mlp_tm=1024, mlp_tf=1024,
        dilated_group=16,
        hgrn_group=32,
        swa_joint=15,
    )


def _vmem_limit(pipelined_bytes, single_bytes=0):
    need = 2 * pipelined_bytes + single_bytes + COMPILER_TEMP_BYTES
    assert need <= V7X_VMEM_BYTES, need
    return int(need)


def _nbytes(shape, dtype):
    return int(np.prod(shape)) * jnp.dtype(dtype).itemsize


def _rmsnorm_rows(x, g):
    ms = jnp.mean(x * x, axis=-1, keepdims=True)
    return x * lax.rsqrt(ms + NORM_EPS) * g


def _sigmoid(x):
    return 1.0 / (1.0 + jnp.exp2(x * (-LOG2E)))


def _rope_tables(seq, head_dim):
    rot = head_dim // ROPE_FRACTION
    half = rot // 2
    inv_freq = 1.0 / (ROPE_THETA ** (jnp.arange(0, rot, 2, dtype=_F32) / rot))
    ang = jnp.arange(seq, dtype=_F32)[:, None] * inv_freq[None, :]
    cos, sin = jnp.cos(ang), jnp.sin(ang)
    pad = head_dim - 2 * half
    ones = jnp.ones((seq, pad), _F32)
    zeros = jnp.zeros((seq, pad), _F32)
    zh = jnp.zeros((seq, half), _F32)
    reps = LANES // head_dim
    cos_full = jnp.tile(jnp.concatenate([cos, cos, ones], axis=1), (1, reps))
    sin_lo = jnp.tile(jnp.concatenate([-sin, zh, zeros], axis=1), (1, reps))
    sin_hi = jnp.tile(jnp.concatenate([zh, sin, zeros], axis=1), (1, reps))
    return cos_full, sin_lo, sin_hi, half


def _rope(x, cos_full, sin_lo, sin_hi, half):
    up = pltpu.roll(x, LANES - half, 1)
    down = pltpu.roll(x, half, 1)
    return x * cos_full + up * sin_lo + down * sin_hi


def _band_bias_window(window):
    i = np.arange(BAND)[:, None]
    j = np.arange(2 * BAND)[None, :]
    dist = i + BAND - j
    return np.where((dist >= 0) & (dist <= window), 0.0, MASKED).astype(np.float32)


def _norm_proj_kernel(*refs, rope_scales, half, emit_h):
    refs = list(refs)
    h_ref = refs.pop() if emit_h else None
    x_ref, g_ref, w_ref, b_ref, cos_ref, slo_ref, shi_ref, o_ref = refs
    cos_full, sin_lo, sin_hi = cos_ref[...], slo_ref[...], shi_ref[...]
    h = _rmsnorm_rows(x_ref[...], g_ref[...]).astype(_BF16)
    if emit_h:
        h_ref[...] = h
    acc = jnp.dot(h, w_ref[0], preferred_element_type=_F32) + b_ref[...]
    for gi, scale in enumerate(rope_scales):
        sl = slice(gi * LANES, (gi + 1) * LANES)
        if scale is None:
            o_ref[:, sl] = acc[:, sl]
        else:
            o_ref[:, sl] = _rope(acc[:, sl], cos_full, sin_lo, sin_hi, half) * scale


def _norm_proj(x, g, w_stack, layer, n, b, tables, rope_scales, seq, tm, emit_h):
    m, d = x.shape
    assert len(rope_scales) * LANES == n
    cos_full, sin_lo, sin_hi, half = tables
    per_seq = seq // tm
    in_specs = [
        pl.BlockSpec((tm, d), lambda i: (i, 0)),
        pl.BlockSpec((1, d), lambda i: (0, 0)),
        pl.BlockSpec((1, d, n), lambda i: (layer, 0, 0), pipeline_mode=pl.Buffered(1)),
        pl.BlockSpec((1, n), lambda i: (0, 0)),
    ] + [pl.BlockSpec((tm, LANES), lambda i: (i % per_seq, 0))] * 3
    blocks = (_nbytes((tm, d), _F32) + _nbytes((tm, n), _F32) + _nbytes((1, d), _F32) + _nbytes((1, n), _F32)
              + 3 * _nbytes((tm, LANES), _F32))
    out_specs = [pl.BlockSpec((tm, n), lambda i: (i, 0))]
    out_shape = [jax.ShapeDtypeStruct((m, n), _F32)]
    if emit_h:
        out_specs.append(pl.BlockSpec((tm, d), lambda i: (i, 0)))
        out_shape.append(jax.ShapeDtypeStruct((m, d), _BF16))
        blocks += _nbytes((tm, d), _BF16)
    single = _nbytes((d, n), _BF16) + _nbytes((tm, d), _BF16) + _nbytes((tm, n), _F32)
    outs = pl.pallas_call(
        functools.partial(_norm_proj_kernel, rope_scales=tuple(rope_scales), half=half, emit_h=emit_h),
        grid=(m // tm,),
        in_specs=in_specs,
        out_specs=out_specs,
        out_shape=out_shape,
        compiler_params=pltpu.CompilerParams(
            dimension_semantics=("arbitrary",), vmem_limit_bytes=_vmem_limit(blocks, single)),
        name="norm_proj",
    )(x, g.reshape(1, d), w_stack, b.reshape(1, n), cos_full, sin_lo, sin_hi)
    return outs if emit_h else outs[0]


def _proj_kernel(h_ref, w_ref, o_ref):
    o_ref[...] = jnp.dot(h_ref[...], w_ref[0], preferred_element_type=_F32)


def _proj(h, w_stack, layer, tm):
    m, d = h.shape
    n = w_stack.shape[2]
    blocks = _nbytes((tm, d), _BF16) + _nbytes((tm, n), _F32)
    return pl.pallas_call(
        _proj_kernel,
        grid=(m // tm,),
        in_specs=[pl.BlockSpec((tm, d), lambda i: (i, 0)),
                  pl.BlockSpec((1, d, n), lambda i: (layer, 0, 0), pipeline_mode=pl.Buffered(1))],
        out_specs=pl.BlockSpec((tm, n), lambda i: (i, 0)),
        out_shape=jax.ShapeDtypeStruct((m, n), _F32),
        compiler_params=pltpu.CompilerParams(
            dimension_semantics=("arbitrary",),
            vmem_limit_bytes=_vmem_limit(blocks, _nbytes((d, n), _BF16) + _nbytes((tm, n), _F32))),
        name="proj",
    )(h, w_stack)


def _res_matmul_kernel(*refs, n_in):
    a_refs, w_refs = refs[:n_in], refs[n_in:2 * n_in]
    b_ref, r_ref, g_ref, o_ref, h_ref = refs[2 * n_in:]
    acc = r_ref[...] + b_ref[...]
    for a_ref, w_ref in zip(a_refs, w_refs):
        acc = acc + jnp.dot(a_ref[...], w_ref[0], preferred_element_type=_F32)
    o_ref[...] = acc
    h_ref[...] = _rmsnorm_rows(acc, g_ref[...]).astype(_BF16)


def _res_matmul(a_list, w_stack, layer, b, res, g_next, tm):
    m, n = res.shape
    k = a_list[0].shape[1]
    blocks = (_nbytes((tm, n), _F32) * 2 + _nbytes((tm, n), _BF16) + 2 * _nbytes((1, n), _F32)
              + len(a_list) * _nbytes((tm, k), _BF16))
    in_specs = [pl.BlockSpec((tm, k), lambda i: (i, 0)) for _ in a_list]
    in_specs += [pl.BlockSpec((1, k, n), lambda i, r=r: (layer, r, 0), pipeline_mode=pl.Buffered(1))
                 for r in range(len(a_list))]
    in_specs += [pl.BlockSpec((1, n), lambda i: (0, 0)), pl.BlockSpec((tm, n), lambda i: (i, 0)),
                 pl.BlockSpec((1, n), lambda i: (0, 0))]
    return pl.pallas_call(
        functools.partial(_res_matmul_kernel, n_in=len(a_list)),
        grid=(m // tm,),
        in_specs=in_specs,
        out_specs=[pl.BlockSpec((tm, n), lambda i: (i, 0)), pl.BlockSpec((tm, n), lambda i: (i, 0))],
        out_shape=[jax.ShapeDtypeStruct((m, n), _F32), jax.ShapeDtypeStruct((m, n), _BF16)],
        compiler_params=pltpu.CompilerParams(
            dimension_semantics=("arbitrary",),
            vmem_limit_bytes=_vmem_limit(blocks, len(a_list) * _nbytes((k, n), _BF16))),
        name="res_matmul",
    )(*a_list, *([w_stack] * len(a_list)), b.reshape(1, n), res, g_next.reshape(1, n))


def _mlp_kernel(*refs, final):
    if final:
        x_ref, h_ref, w1_ref, w2_ref, gf_ref, o_ref = refs
    else:
        x_ref, h_ref, w1_ref, w2_ref, o_ref = refs
    j = pl.program_id(1)
    a = jnp.dot(h_ref[...], w1_ref[0], preferred_element_type=_F32)
    a = jnp.square(jnp.maximum(a, 0.0)).astype(_BF16)
    so_far = jnp.where(j == 0, 0.0, o_ref[...])
    o_ref[...] = so_far + jnp.dot(a, w2_ref[0], preferred_element_type=_F32)
    piece = x_ref.shape[0]
    rows = pl.ds(pl.multiple_of(j * piece, piece), piece)
    o_ref[rows, :] += x_ref[...]
    if final:
        @pl.when(j == pl.num_programs(1) - 1)
        def _():
            o_ref[...] = _rmsnorm_rows(o_ref[...], gf_ref[...])


def _mlp(x, h, w1_stack, w2_stack, layer, final_gain, tm, tf):
    m, d = x.shape
    f = w1_stack.shape[2]
    n_tiles = f // tf
    piece = tm // n_tiles
    blocks = (_nbytes((tm, d), _F32) + _nbytes((piece, d), _F32) + _nbytes((tm, d), _BF16) + _nbytes((1, d), _F32)
              + _nbytes((d, tf), _BF16) + _nbytes((tf, d), _BF16))
    temporaries = _nbytes((tm, tf), _F32) + _nbytes((tm, tf), _BF16)
    in_specs = [
        pl.BlockSpec((piece, d), lambda i, j: (i * n_tiles + j, 0)),
        pl.BlockSpec((tm, d), lambda i, j: (i, 0)),
        pl.BlockSpec((1, d, tf), lambda i, j: (layer, 0, j)),
        pl.BlockSpec((1, tf, d), lambda i, j: (layer, j, 0)),
    ]
    args = [x, h, w1_stack, w2_stack]
    if final_gain is not None:
        in_specs.append(pl.BlockSpec((1, d), lambda i, j: (0, 0)))
        args.append(final_gain.reshape(1, d))
    return pl.pallas_call(
        functools.partial(_mlp_kernel, final=final_gain is not None),
        grid=(m // tm, f // tf),
        in_specs=in_specs,
        out_specs=pl.BlockSpec((tm, d), lambda i, j: (i, 0)),
        out_shape=jax.ShapeDtypeStruct((m, d), _F32),
        compiler_params=pltpu.CompilerParams(
            dimension_semantics=("arbitrary", "arbitrary"),
            vmem_limit_bytes=_vmem_limit(blocks, temporaries)),
        name="mlp",
    )(*args)


def _by4_bias(class_len):
    assert A_BRANCHES == ((128, 1), (512, 4), (2048, 16))
    blocks = class_len // BAND
    i = np.arange(BAND)[:, None]
    j = np.arange(BAND)[None, :]
    tiles = []
    for off in range(blocks - 1, -1, -1):
        dist = off * BAND + i - j
        in_d4 = (dist >= 0) & (dist <= A_BRANCHES[1][0] // A_BRANCHES[1][1])
        in_d16 = (dist >= 0) & (dist % 4 == 0) & (dist // 4 <= A_BRANCHES[2][0] // A_BRANCHES[2][1])
        mult = in_d4.astype(np.int64) + in_d16.astype(np.int64)
        tiles.append(np.where(mult > 0, np.log2(np.maximum(mult, 1)), MASKED))
    return np.concatenate(tiles, axis=1).astype(np.float32)


def _dilated_kernel(q_ref, k_ref, v_ref, bias_ref, bias4_ref, o_ref,
                    qp_ref, kp_ref, vp_ref, acc_ref, m_ref, l_ref, acc4_ref, m4_ref, l4_ref, *, group):
    seq = acc_ref.shape[0]
    n_all = seq // BAND
    quarter = seq // 4
    n_class = quarter // BAND
    bias_full = bias_ref[...]
    col = lax.broadcasted_iota(jnp.int32, (BAND, 2 * BAND), 1)
    bias_first = jnp.where(col < BAND, MASKED, bias_full)
    batch_nt = (((2,), (2,)), ((0,), (0,)))
    batch_nn = (((2,), (1,)), ((0,), (0,)))

    def by4_rows(c):
        return pl.ds(c, quarter, stride=4)

    def softmax_parts(q, k, v, biases):
        s = lax.dot_general(q, k, batch_nt, preferred_element_type=_F32)
        s = jnp.stack([s[g] + biases[g] for g in range(len(biases))])
        m = jnp.max(s, axis=-1, keepdims=True)
        p = jnp.exp2(s - m)
        l = jnp.sum(p, axis=-1, keepdims=True)
        num = lax.dot_general(p.astype(_BF16), v, batch_nn, preferred_element_type=_F32)
        shape = num.shape
        return num, jnp.broadcast_to(m, shape), jnp.broadcast_to(l, shape)

    qp_ref[...] = q_ref[0].astype(_BF16)
    kp_ref[...] = k_ref[0].astype(_BF16)
    vp_ref[...] = v_ref[0].astype(_BF16)
    for g0 in range(0, n_all, group):
        cur = slice(g0 * BAND, (g0 + group) * BAND)
        if g0 == 0:
            prev_of = lambda ref: jnp.concatenate([ref[0:BAND, :], ref[0:(group - 1) * BAND, :]], axis=0)
        else:
            prev_of = lambda ref: ref[(g0 - 1) * BAND:(g0 + group - 1) * BAND, :]
        shape = (group, BAND, A_DIM)
        k = jnp.concatenate([prev_of(kp_ref).reshape(shape), kp_ref[cur, :].reshape(shape)], axis=1)
        v = jnp.concatenate([prev_of(vp_ref).reshape(shape), vp_ref[cur, :].reshape(shape)], axis=1)
        biases = [bias_first if g0 + g == 0 else bias_full for g in range(group)]
        num, m, l = softmax_parts(qp_ref[cur, :].reshape(shape), k, v, biases)
        acc_ref[cur, :] = num.reshape(group * BAND, A_DIM)
        m_ref[cur, :] = m.reshape(group * BAND, A_DIM)
        l_ref[cur, :] = l.reshape(group * BAND, A_DIM)

    for c in range(4):
        seg = slice(c * quarter, (c + 1) * quarter)
        qp_ref[seg, :] = q_ref[0, by4_rows(c), :].astype(_BF16)
        kp_ref[seg, :] = k_ref[0, by4_rows(c), :].astype(_BF16)
        vp_ref[seg, :] = v_ref[0, by4_rows(c), :].astype(_BF16)
    bias4 = bias4_ref[...]
    parts = []
    for n in range(n_class):
        q = jnp.stack([qp_ref[c * quarter + n * BAND:c * quarter + (n + 1) * BAND, :] for c in range(4)])
        k = jnp.stack([kp_ref[c * quarter:c * quarter + (n + 1) * BAND, :] for c in range(4)])
        v = jnp.stack([vp_ref[c * quarter:c * quarter + (n + 1) * BAND, :] for c in range(4)])
        parts.append(softmax_parts(q, k, v, [bias4[:, (n_class - 1 - n) * BAND:]] * 4))
    for n, (num, m, l) in enumerate(parts):
        for c in range(4):
            rows = slice(c * quarter + n * BAND, c * quarter + (n + 1) * BAND)
            acc4_ref[rows, :] = num[c]
            m4_ref[rows, :] = m[c]
            l4_ref[rows, :] = l[c]

    for c in range(4):
        seg = slice(c * quarter, (c + 1) * quarter)
        rows = by4_rows(c)
        m_a, m_b = m_ref[rows, :], m4_ref[seg, :]
        m_new = jnp.maximum(m_a, m_b)
        a = jnp.exp2(m_a - m_new)
        b = jnp.exp2(m_b - m_new)
        num = acc_ref[rows, :] * a + acc4_ref[seg, :] * b
        den = l_ref[rows, :] * a + l4_ref[seg, :] * b
        acc_ref[rows, :] = num / den
    o_ref[0] = acc_ref[...].astype(o_ref.dtype)


def _dilated_attention(proj, bias, bias4, group):
    bsz, seq, _ = proj.shape
    head = lambda off: pl.BlockSpec((1, seq, A_DIM), lambda b, h, off=off: (b, 0, off + h))
    blocks = (3 * _nbytes((seq, A_DIM), _F32) + _nbytes((seq, A_DIM), _BF16)
              + _nbytes(bias.shape, _F32) + _nbytes(bias4.shape, _F32))
    scratch = 6 * _nbytes((seq, LANES), _F32) + 3 * _nbytes((seq, LANES), _BF16)
    temporaries = 2 * group * _nbytes((BAND, 2 * BAND), _F32)
    return pl.pallas_call(
        functools.partial(_dilated_kernel, group=group),
        grid=(bsz, A_HEADS),
        in_specs=[head(0), head(A_HEADS), head(2 * A_HEADS),
                  pl.BlockSpec(bias.shape, lambda b, h: (0, 0)),
                  pl.BlockSpec(bias4.shape, lambda b, h: (0, 0))],
        out_specs=pl.BlockSpec((1, seq, A_DIM), lambda b, h: (b, 0, h)),
        out_shape=jax.ShapeDtypeStruct((bsz, seq, A_WIDTH), _BF16),
        scratch_shapes=[pltpu.VMEM((seq, A_DIM), _BF16) for _ in range(3)]
        + [pltpu.VMEM((seq, LANES), _F32) for _ in range(6)],
        compiler_params=pltpu.CompilerParams(
            dimension_semantics=("arbitrary", "arbitrary"),
            vmem_limit_bytes=_vmem_limit(blocks, scratch + temporaries)),
        name="dilated",
    )(proj, proj, proj, bias, bias4)


_B_LEVELS = (32, 16, 8, 4, 2, 1)


def _hgrn_constants():
    c = B_CHUNK
    t = np.arange(c)[:, None]
    r = np.arange(c)[None, :]
    mats = [(r <= t), (r > t)]
    for h in _B_LEVELS:
        base = (t // (2 * h)) * (2 * h)
        mid = base + h - 1
        upper = (t & h) != 0
        mats.append(np.where(upper, (r > mid) & (r <= t), (r > t) & (r <= mid)))
    sums = np.concatenate(mats, axis=0).astype(np.float32)
    s = np.arange(c)[None, :]
    x = t ^ s
    level = np.where(s > t, -1, np.where(s == t, 0, 1 << (np.floor(np.log2(np.maximum(x, 1))).astype(np.int64))))
    return np.concatenate([sums, sums], axis=1), level.astype(np.int32)


def _hgrn_kernel(q_ref, f_ref, i_ref, g_ref, lb_ref, gain_ref, sums_ref, level_ref, o_ref, *, group):
    seq = q_ref.shape[1]
    c = B_CHUNK
    span = group * c
    lb = lb_ref[0]
    gain = gain_ref[...]
    sums2 = sums_ref[...]
    level = level_ref[...]
    chunks = range(group)

    def piece(x, j):
        return x[j * c:(j + 1) * c]

    def step(n, state_t):
        rows = pl.ds(pl.multiple_of(n * span, span), span)
        gate = lb + (1.0 - lb) * _sigmoid(f_ref[0, rows, :])
        log_gate = jnp.log2(gate)
        k = 1.0 - gate
        qraw = q_ref[0, rows, :]
        q = qraw * _sigmoid(qraw) * (B_DIM ** -0.5)
        v = i_ref[0, rows, :].astype(_BF16)

        g_hi = log_gate.astype(_BF16)
        g_lo = (log_gate - g_hi.astype(_F32)).astype(_BF16)
        hi_lo = [jnp.concatenate([piece(g_hi, j), piece(g_lo, j)], axis=0) for j in chunks]
        expo = []
        for j in range(0, group, 2):
            both = jnp.dot(sums2, jnp.concatenate([hi_lo[j], hi_lo[j + 1]], axis=1),
                           preferred_element_type=_F32)
            expo += [both[:, :B_DIM], both[:, B_DIM:]]
        decay = [jnp.exp2(e) for e in expo]

        qc = [piece(q, j) for j in chunks]
        kc = [piece(k, j) for j in chunks]
        vc = [piece(v, j) for j in chunks]
        q_bf = [qc[j].astype(_BF16) for j in chunks]
        k_bf = [kc[j].astype(_BF16) for j in chunks]
        level_decay = [[piece(decay[j], 2 + i).astype(_BF16) for i in range(len(_B_LEVELS))] for j in chunks]
        scaled_q = [[q_bf[j]] + [q_bf[j] * d for d in level_decay[j]] for j in chunks]
        scaled_k = [[k_bf[j]] + [k_bf[j] * d for d in level_decay[j]] for j in chunks]
        q_in = [(qc[j] * piece(decay[j], 0)).astype(_BF16) for j in chunks]
        k_out = [(kc[j] * piece(decay[j], 1)).astype(_BF16) for j in chunks]
        parts = [[lax.dot_general(a, b, _NT, preferred_element_type=_F32)
                  for a, b in zip(scaled_q[j], scaled_k[j])] for j in chunks]
        attn = []
        for j in chunks:
            total = jnp.where(level == 0, parts[j][0], 0.0)
            for i, h in enumerate(_B_LEVELS):
                total = total + jnp.where(level == h, parts[j][1 + i], 0.0)
            attn.append(total.astype(_BF16))
        intra = [jnp.dot(attn[j], vc[j], preferred_element_type=_F32) for j in chunks]
        update = [lax.dot_general(vc[j], k_out[j], _TN, preferred_element_type=_F32) for j in chunks]

        outs = []
        for j in chunks:
            outs.append(intra[j] + lax.dot_general(q_in[j], state_t.astype(_BF16), _NT,
                                                   preferred_element_type=_F32))
            state_t = state_t * decay[j][c - 1:c] + update[j]

        y = _rmsnorm_rows(jnp.concatenate(outs, axis=0), gain)
        graw = g_ref[0, rows, :]
        y = y * (graw * _sigmoid(graw))
        o_ref[0, rows, :] = y.astype(o_ref.dtype)
        return state_t

    lax.fori_loop(0, seq // span, step, jnp.zeros((B_DIM, B_DIM), _F32))


def _hgrn2(proj, lower_bound, gain, sums, level, group):
    bsz, seq, _ = proj.shape
    assert group % 2 == 0 and seq % (group * B_CHUNK) == 0
    head = lambda off: pl.BlockSpec((1, seq, B_DIM), lambda b, h, off=off: (b, 0, off + h))
    blocks = 4 * _nbytes((seq, B_DIM), _F32) + _nbytes((seq, B_DIM), _BF16) \
        + _nbytes(sums.shape, _BF16) + _nbytes(level.shape, jnp.int32) + 2 * _nbytes((8, LANES), _F32)
    temporaries = 3 * group * _nbytes((sums.shape[0], B_DIM), _F32)
    return pl.pallas_call(
        functools.partial(_hgrn_kernel, group=group),
        grid=(bsz, B_HEADS),
        in_specs=[head(0), head(B_HEADS), head(2 * B_HEADS), head(3 * B_HEADS),
                  pl.BlockSpec((1, 1, B_DIM), lambda b, h: (h, 0, 0)),
                  pl.BlockSpec((1, B_DIM), lambda b, h: (0, 0)),
                  pl.BlockSpec(sums.shape, lambda b, h: (0, 0)),
                  pl.BlockSpec(level.shape, lambda b, h: (0, 0))],
        out_specs=pl.BlockSpec((1, seq, B_DIM), lambda b, h: (b, 0, h)),
        out_shape=jax.ShapeDtypeStruct((bsz, seq, B_WIDTH), _BF16),
        compiler_params=pltpu.CompilerParams(
            dimension_semantics=("arbitrary", "arbitrary"),
            vmem_limit_bytes=_vmem_limit(blocks, temporaries)),
        name="hgrn2",
    )(proj, proj, proj, proj, lower_bound.reshape(B_HEADS, 1, B_DIM), gain.reshape(1, B_DIM), sums, level)


def _swa_kernel(q_ref, k_ref, v_ref, sink_ref, bias_ref, o_ref, klo_ref, khi_ref, vlo_ref, vhi_ref, *, joint):
    seq = q_ref.shape[1]
    kv_head = pl.program_id(1)
    lane_half = lax.broadcasted_iota(jnp.int32, (seq, LANES), 1) // C_DIM
    odd_head = (kv_head % 2) == 1

    def place(x):
        own = jnp.where(lane_half == kv_head % 2, x, 0.0)
        other = pltpu.roll(own, C_DIM, 1)
        lo = jnp.where(odd_head, other, own)
        hi = jnp.where(odd_head, own, other)
        return lo.astype(_BF16), hi.astype(_BF16)

    klo_ref[...], khi_ref[...] = place(k_ref[0])
    vlo_ref[...], vhi_ref[...] = place(v_ref[0])
    pairs = C_GROUP // 2

    heads = [(j, side) for side in range(2) for j in range(pairs)]
    sinks = [sink_ref[kv_head * C_GROUP + 2 * j + side] * LOG2E for j, side in heads]
    low_q = lax.broadcasted_iota(jnp.int32, (BAND, LANES), 1) < C_DIM
    k_refs, v_refs = (klo_ref, khi_ref), (vlo_ref, vhi_ref)

    def attend(block_ids, with_prev):
        curs = [pl.ds(pl.multiple_of(n * BAND, BAND), BAND) for n in block_ids]
        if with_prev:
            keys = [pl.ds(pl.multiple_of((n - 1) * BAND, BAND), 2 * BAND) for n in block_ids]
            bias = bias_ref[...]
        else:
            keys = curs
            bias = bias_ref[:, BAND:]
        count = range(len(block_ids))
        q4 = [jnp.concatenate([q_ref[0, curs[b], j * LANES:(j + 1) * LANES] for j in range(pairs)],
                              axis=0).astype(_BF16) for b in count]
        s_side = [[lax.dot_general(q4[b], k_refs[side][keys[b], :], _NT, preferred_element_type=_F32)
                   for side in range(2)] for b in count]
        s = [[s_side[b][side][j * BAND:(j + 1) * BAND] + bias for j, side in heads] for b in count]
        m = [[jnp.maximum(jnp.max(s_h, axis=-1, keepdims=True), sink) for s_h, sink in zip(s[b], sinks)]
             for b in count]
        p = [[jnp.exp2(s_h - m_h) for s_h, m_h in zip(s[b], m[b])] for b in count]
        sums = [[jnp.sum(p_h, axis=-1, keepdims=True) for p_h in p[b]] for b in count]
        num = [[jnp.dot(jnp.concatenate(p[b][side * pairs:(side + 1) * pairs], axis=0).astype(_BF16),
                        v_refs[side][keys[b], :], preferred_element_type=_F32) for side in range(2)]
               for b in count]
        both = lambda lo, hi: jnp.where(low_q, lo, hi)
        den = [[both(sums[b][j], sums[b][pairs + j])
                + jnp.exp2(both(sinks[j], sinks[pairs + j]) - both(m[b][j], m[b][pairs + j]))
                for j in range(pairs)] for b in count]
        for b in count:
            for j in range(pairs):
                rows = slice(j * BAND, (j + 1) * BAND)
                out = both(num[b][0][rows], num[b][1][rows]) / den[b][j]
                o_ref[0, curs[b], j * LANES:(j + 1) * LANES] = out.astype(o_ref.dtype)

    attend([0], False)
    later = seq // BAND - 1
    rounds = later // joint

    def round_(t, carry):
        attend([1 + t * joint + b for b in range(joint)], True)
        return carry

    lax.fori_loop(0, rounds, round_, 0)
    if later % joint:
        attend(list(range(1 + rounds * joint, later + 1)), True)


def _swa(proj, sinks, bias, joint):
    bsz, seq, _ = proj.shape
    qw = C_GROUP * C_DIM
    k_base = C_Q_HEADS * C_DIM // LANES
    v_base = k_base + C_KV_HEADS * C_DIM // LANES
    blocks = _nbytes((seq, qw), _F32) + 2 * _nbytes((seq, LANES), _F32) + _nbytes((seq, qw), _BF16) \
        + _nbytes(bias.shape, _F32)
    scratch = 4 * _nbytes((seq, LANES), _BF16)
    temporaries = 2 * joint * C_GROUP * _nbytes((BAND, 2 * BAND), _F32)
    return pl.pallas_call(
        functools.partial(_swa_kernel, joint=joint),
        grid=(bsz, C_KV_HEADS),
        in_specs=[pl.BlockSpec((1, seq, qw), lambda b, h: (b, 0, h)),
                  pl.BlockSpec((1, seq, LANES), lambda b, h: (b, 0, k_base + h // 2)),
                  pl.BlockSpec((1, seq, LANES), lambda b, h: (b, 0, v_base + h // 2)),
                  pl.BlockSpec(memory_space=pltpu.SMEM),
                  pl.BlockSpec(bias.shape, lambda b, h: (0, 0))],
        out_specs=pl.BlockSpec((1, seq, qw), lambda b, h: (b, 0, h)),
        out_shape=jax.ShapeDtypeStruct((bsz, seq, C_Q_HEADS * C_DIM), _BF16),
        scratch_shapes=[pltpu.VMEM((seq, LANES), _BF16) for _ in range(4)],
        compiler_params=pltpu.CompilerParams(
            dimension_semantics=("arbitrary", "arbitrary"),
            vmem_limit_bytes=_vmem_limit(blocks, scratch + temporaries)),
        name="swa",
    )(proj, proj, proj, sinks.astype(_F32), bias)


def kernel(x, norm_mix_g, norm_mlp_g, final_norm_g, even_w_in, even_w_out, hgrn_lb_raw, hgrn_norm_g,
           odd_w_qkv, odd_b_qkv, odd_sinks, odd_w_o, odd_b_o, mlp_w1, mlp_w2):
    bsz, seq, d = x.shape
    depth = norm_mix_g.shape[0]
    t = _tiles()
    m = bsz * seq

    tables_a = _rope_tables(seq, A_DIM)
    tables_c = _rope_tables(seq, C_DIM)
    bias_a = jnp.asarray(_band_bias_window(BAND))
    bias_a4 = jnp.asarray(_by4_bias(seq // 4))
    bias_c = jnp.asarray(_band_bias_window(C_WINDOW - 1))
    sums_np, level_np = _hgrn_constants()
    sums, level = jnp.asarray(sums_np, _BF16), jnp.asarray(level_np)

    lb_soft = jax.nn.softmax(hgrn_lb_raw.astype(_F32), axis=0)
    lower_bounds = jnp.cumsum(lb_soft, axis=0) - lb_soft[0:1]

    w_in_a = even_w_in[:, :, :A_IN].astype(_BF16)
    w_in_b = even_w_in[:, :, A_IN:].astype(_BF16)
    w_out = even_w_out.astype(_BF16)
    w_qkv = odd_w_qkv.astype(_BF16)
    w_o = odd_w_o.astype(_BF16)
    w1 = mlp_w1.astype(_BF16)
    w2 = mlp_w2.astype(_BF16)

    group_a = A_WIDTH // LANES
    rope_a = [A_DIM ** -0.5 * LOG2E] * group_a + [1.0] * group_a + [None] * group_a
    q_groups = C_Q_HEADS * C_DIM // LANES
    kv_groups = C_KV_HEADS * C_DIM // LANES
    rope_c = [C_DIM ** -0.5 * LOG2E] * q_groups + [1.0] * kv_groups + [None] * kv_groups
    zeros_d = jnp.zeros((d,), _F32)

    x2 = x.reshape(m, d)
    for layer in range(depth):
        if layer % 2 == 0:
            e = layer // 2
            proj_a, h_mix = _norm_proj(x2, norm_mix_g[layer], w_in_a, e, A_IN, jnp.zeros((A_IN,), _F32),
                                       tables_a, rope_a, seq, t["proj_tm"], emit_h=True)
            proj_a = proj_a.reshape(bsz, seq, A_IN)
            proj_b = _proj(h_mix, w_in_b, e, t["proj_tm"]).reshape(bsz, seq, B_IN)
            oa = _dilated_attention(proj_a, bias_a, bias_a4, t["dilated_group"]).reshape(m, A_WIDTH)
            ob = _hgrn2(proj_b, lower_bounds[e], hgrn_norm_g[e], sums, level, t["hgrn_group"]).reshape(m, B_WIDTH)
            x2, h2 = _res_matmul([oa, ob], w_out, e, zeros_d, x2, norm_mlp_g[layer], t["out_tm"])
        else:
            o = layer // 2
            proj = _norm_proj(x2, norm_mix_g[layer], w_qkv, o, C_QKV, odd_b_qkv[o],
                              tables_c, rope_c, seq, t["proj_tm"], emit_h=False).reshape(bsz, seq, C_QKV)
            attn = _swa(proj, odd_sinks[o], bias_c, t["swa_joint"]).reshape(m, C_Q_HEADS * C_DIM)
            x2, h2 = _res_matmul([attn], w_o, o, odd_b_o[o], x2, norm_mlp_g[layer], t["out_tm"])
        last = layer == depth - 1
        x2 = _mlp(x2, h2, w1, w2, layer, final_norm_g if last else None, t["mlp_tm"], t["mlp_tf"])
    return x2.reshape(bsz, seq, d)
```

```python
import functools
import math

import numpy as np
import jax
import jax.numpy as jnp
from jax import lax
from jax.experimental import pallas as pl
from jax.experimental.pallas import tpu as pltpu

NORM_EPS = 1e-5
ROPE_THETA = 500000.0
ROPE_FRACTION = 4
BAND = 128

A_DIM = 128
A_HEADS = 8
A_BRANCHES = ((128, 1), (512, 4), (2048, 16))
A_WIDTH = A_HEADS * A_DIM
B_DIM = 128
B_HEADS = 8
B_WIDTH = B_HEADS * B_DIM
B_CHUNK = 64
A_IN = 3 * A_WIDTH
B_IN = 4 * B_WIDTH

C_DIM = 64
C_Q_HEADS = 32
C_KV_HEADS = 4
C_GROUP = C_Q_HEADS // C_KV_HEADS
C_WINDOW = 128
C_QKV = (C_Q_HEADS + 2 * C_KV_HEADS) * C_DIM

LANES = 128
V7X_VMEM_BYTES = 64 * 1024 * 1024
COMPILER_TEMP_BYTES = 6 * 1024 * 1024

MASKED = -1e30
LOG2E = math.log2(math.e)

_BF16 = jnp.bfloat16
_F32 = jnp.float32
_NT = (((1,), (1,)), ((), ()))
_TN = (((0,), (0,)), ((), ()))


def _tiles():
    return dict(
        proj_tm=512,
        out_tm=512,
        mlp_tm=1024, mlp_tf=1024,
        dilated_group=16,
        hgrn_group=32,
        swa_joint=15,
    )


def _vmem_limit(pipelined_bytes, single_bytes=0):
    need = 2 * pipelined_bytes + single_bytes + COMPILER_TEMP_BYTES
    assert need <= V7X_VMEM_BYTES, need
    return int(need)


def _nbytes(shape, dtype):
    return int(np.prod(shape)) * jnp.dtype(dtype).itemsize


def _rmsnorm_rows(x, g):
    ms = jnp.mean(x * x, axis=-1, keepdims=True)
    return x * lax.rsqrt(ms + NORM_EPS) * g


def _sigmoid(x):
    return 1.0 / (1.0 + jnp.exp2(x * (-LOG2E)))


def _rope_tables(seq, head_dim):
    rot = head_dim // ROPE_FRACTION
    half = rot // 2
    inv_freq = 1.0 / (ROPE_THETA ** (jnp.arange(0, rot, 2, dtype=_F32) / rot))
    ang = jnp.arange(seq, dtype=_F32)[:, None] * inv_freq[None, :]
    cos, sin = jnp.cos(ang), jnp.sin(ang)
    pad = head_dim - 2 * half
    ones = jnp.ones((seq, pad), _F32)
    zeros = jnp.zeros((seq, pad), _F32)
    zh = jnp.zeros((seq, half), _F32)
    reps = LANES // head_dim
    cos_full = jnp.tile(jnp.concatenate([cos, cos, ones], axis=1), (1, reps))
    sin_lo = jnp.tile(jnp.concatenate([-sin, zh, zeros], axis=1), (1, reps))
    sin_hi = jnp.tile(jnp.concatenate([zh, sin, zeros], axis=1), (1, reps))
    return cos_full, sin_lo, sin_hi, half


def _rope(x, cos_full, sin_lo, sin_hi, half):
    up = pltpu.roll(x, LANES - half, 1)
    down = pltpu.roll(x, half, 1)
    return x * cos_full + up * sin_lo + down * sin_hi


def _band_bias_window(window):
    i = np.arange(BAND)[:, None]
    j = np.arange(2 * BAND)[None, :]
    dist = i + BAND - j
    return np.where((dist >= 0) & (dist <= window), 0.0, MASKED).astype(np.float32)


def _norm_proj_kernel(*refs, rope_scales, half, emit_h):
    refs = list(refs)
    h_ref = refs.pop() if emit_h else None
    x_ref, g_ref, w_ref, b_ref, cos_ref, slo_ref, shi_ref, o_ref = refs
    cos_full, sin_lo, sin_hi = cos_ref[...], slo_ref[...], shi_ref[...]
    h = _rmsnorm_rows(x_ref[...], g_ref[...]).astype(_BF16)
    if emit_h:
        h_ref[...] = h
    acc = jnp.dot(h, w_ref[0], preferred_element_type=_F32) + b_ref[...]
    for gi, scale in enumerate(rope_scales):
        sl = slice(gi * LANES, (gi + 1) * LANES)
        if scale is None:
            o_ref[:, sl] = acc[:, sl]
        else:
            o_ref[:, sl] = _rope(acc[:, sl], cos_full, sin_lo, sin_hi, half) * scale


def _norm_proj(x, g, w_stack, layer, n, b, tables, rope_scales, seq, tm, emit_h):
    m, d = x.shape
    assert len(rope_scales) * LANES == n
    cos_full, sin_lo, sin_hi, half = tables
    per_seq = seq // tm
    in_specs = [
        pl.BlockSpec((tm, d), lambda i: (i, 0)),
        pl.BlockSpec((1, d), lambda i: (0, 0)),
        pl.BlockSpec((1, d, n), lambda i: (layer, 0, 0), pipeline_mode=pl.Buffered(1)),
        pl.BlockSpec((1, n), lambda i: (0, 0)),
    ] + [pl.BlockSpec((tm, LANES), lambda i: (i % per_seq, 0))] * 3
    blocks = (_nbytes((tm, d), _F32) + _nbytes((tm, n), _F32) + _nbytes((1, d), _F32) + _nbytes((1, n), _F32)
              + 3 * _nbytes((tm, LANES), _F32))
    out_specs = [pl.BlockSpec((tm, n), lambda i: (i, 0))]
    out_shape = [jax.ShapeDtypeStruct((m, n), _F32)]
    if emit_h:
        out_specs.append(pl.BlockSpec((tm, d), lambda i: (i, 0)))
        out_shape.append(jax.ShapeDtypeStruct((m, d), _BF16))
        blocks += _nbytes((tm, d), _BF16)
    single = _nbytes((d, n), _BF16) + _nbytes((tm, d), _BF16) + _nbytes((tm, n), _F32)
    outs = pl.pallas_call(
        functools.partial(_norm_proj_kernel, rope_scales=tuple(rope_scales), half=half, emit_h=emit_h),
        grid=(m // tm,),
        in_specs=in_specs,
        out_specs=out_specs,
        out_shape=out_shape,
        compiler_params=pltpu.CompilerParams(
            dimension_semantics=("arbitrary",), vmem_limit_bytes=_vmem_limit(blocks, single)),
        name="norm_proj",
    )(x, g.reshape(1, d), w_stack, b.reshape(1, n), cos_full, sin_lo, sin_hi)
    return outs if emit_h else outs[0]


def _proj_kernel(h_ref, w_ref, o_ref):
    o_ref[...] = jnp.dot(h_ref[...], w_ref[0], preferred_element_type=_F32)


def _proj(h, w_stack, layer, tm):
    m, d = h.shape
    n = w_stack.shape[2]
    blocks = _nbytes((tm, d), _BF16) + _nbytes((tm, n), _F32)
    return pl.pallas_call(
        _proj_kernel,
        grid=(m // tm,),
        in_specs=[pl.BlockSpec((tm, d), lambda i: (i, 0)),
                  pl.BlockSpec((1, d, n), lambda i: (layer, 0, 0), pipeline_mode=pl.Buffered(1))],
        out_specs=pl.BlockSpec((tm, n), lambda i: (i, 0)),
        out_shape=jax.ShapeDtypeStruct((m, n), _F32),
        compiler_params=pltpu.CompilerParams(
            dimension_semantics=("arbitrary",),
            vmem_limit_bytes=_vmem_limit(blocks, _nbytes((d, n), _BF16) + _nbytes((tm, n), _F32))),
        name="proj",
    )(h, w_stack)


def _res_matmul_kernel(*refs, n_in):
    a_refs, w_refs = refs[:n_in], refs[n_in:2 * n_in]
    b_ref, r_ref, g_ref, o_ref, h_ref = refs[2 * n_in:]
    acc = r_ref[...] + b_ref[...]
    for a_ref, w_ref in zip(a_refs, w_refs):
        acc = acc + jnp.dot(a_ref[...], w_ref[0], preferred_element_type=_F32)
    o_ref[...] = acc
    h_ref[...] = _rmsnorm_rows(acc, g_ref[...]).astype(_BF16)


def _res_matmul(a_list, w_stack, layer, b, res, g_next, tm):
    m, n = res.shape
    k = a_list[0].shape[1]
    blocks = (_nbytes((tm, n), _F32) * 2 + _nbytes((tm, n), _BF16) + 2 * _nbytes((1, n), _F32)
              + len(a_list) * _nbytes((tm, k), _BF16))
    in_specs = [pl.BlockSpec((tm, k), lambda i: (i, 0)) for _ in a_list]
    in_specs += [pl.BlockSpec((1, k, n), lambda i, r=r: (layer, r, 0), pipeline_mode=pl.Buffered(1))
                 for r in range(len(a_list))]
    in_specs += [pl.BlockSpec((1, n), lambda i: (0, 0)), pl.BlockSpec((tm, n), lambda i: (i, 0)),
                 pl.BlockSpec((1, n), lambda i: (0, 0))]
    return pl.pallas_call(
        functools.partial(_res_matmul_kernel, n_in=len(a_list)),
        grid=(m // tm,),
        in_specs=in_specs,
        out_specs=[pl.BlockSpec((tm, n), lambda i: (i, 0)), pl.BlockSpec((tm, n), lambda i: (i, 0))],
        out_shape=[jax.ShapeDtypeStruct((m, n), _F32), jax.ShapeDtypeStruct((m, n), _BF16)],
        compiler_params=pltpu.CompilerParams(
            dimension_semantics=("arbitrary",),
            vmem_limit_bytes=_vmem_limit(blocks, len(a_list) * _nbytes((k, n), _BF16))),
        name="res_matmul",
    )(*a_list, *([w_stack] * len(a_list)), b.reshape(1, n), res, g_next.reshape(1, n))


def _mlp_kernel(*refs, final):
    if final:
        x_ref, h_ref, w1_ref, w2_ref, gf_ref, o_ref = refs
    else:
        x_ref, h_ref, w1_ref, w2_ref, o_ref = refs
    j = pl.program_id(1)
    a = jnp.dot(h_ref[...], w1_ref[0], preferred_element_type=_F32)
    a = jnp.square(jnp.maximum(a, 0.0)).astype(_BF16)
    so_far = jnp.where(j == 0, 0.0, o_ref[...])
    o_ref[...] = so_far + jnp.dot(a, w2_ref[0], preferred_element_type=_F32)
    piece = x_ref.shape[0]
    rows = pl.ds(pl.multiple_of(j * piece, piece), piece)
    o_ref[rows, :] += x_ref[...]
    if final:
        @pl.when(j == pl.num_programs(1) - 1)
        def _():
            o_ref[...] = _rmsnorm_rows(o_ref[...], gf_ref[...])


def _mlp(x, h, w1_stack, w2_stack, layer, final_gain, tm, tf):
    m, d = x.shape
    f = w1_stack.shape[2]
    n_tiles = f // tf
    piece = tm // n_tiles
    blocks = (_nbytes((tm, d), _F32) + _nbytes((piece, d), _F32) + _nbytes((tm, d), _BF16) + _nbytes((1, d), _F32)
              + _nbytes((d, tf), _BF16) + _nbytes((tf, d), _BF16))
    temporaries = _nbytes((tm, tf), _F32) + _nbytes((tm, tf), _BF16)
    in_specs = [
        pl.BlockSpec((piece, d), lambda i, j: (i * n_tiles + j, 0)),
        pl.BlockSpec((tm, d), lambda i, j: (i, 0)),
        pl.BlockSpec((1, d, tf), lambda i, j: (layer, 0, j)),
        pl.BlockSpec((1, tf, d), lambda i, j: (layer, j, 0)),
    ]
    args = [x, h, w1_stack, w2_stack]
    if final_gain is not None:
        in_specs.append(pl.BlockSpec((1, d), lambda i, j: (0, 0)))
        args.append(final_gain.reshape(1, d))
    return pl.pallas_call(
        functools.partial(_mlp_kernel, final=final_gain is not None),
        grid=(m // tm, f // tf),
        in_specs=in_specs,
        out_specs=pl.BlockSpec((tm, d), lambda i, j: (i, 0)),
        out_shape=jax.ShapeDtypeStruct((m, d), _F32),
        compiler_params=pltpu.CompilerParams(
            dimension_semantics=("arbitrary", "arbitrary"),
            vmem_limit_bytes=_vmem_limit(blocks, temporaries)),
        name="mlp",
    )(*args)


def _by4_bias(class_len):
    assert A_BRANCHES == ((128, 1), (512, 4), (2048, 16))
    blocks = class_len // BAND
    i = np.arange(BAND)[:, None]
    j = np.arange(BAND)[None, :]
    tiles = []
    for off in range(blocks - 1, -1, -1):
        dist = off * BAND + i - j
        in_d4 = (dist >= 0) & (dist <= A_BRANCHES[1][0] // A_BRANCHES[1][1])
        in_d16 = (dist >= 0) & (dist % 4 == 0) & (dist // 4 <= A_BRANCHES[2][0] // A_BRANCHES[2][1])
        mult = in_d4.astype(np.int64) + in_d16.astype(np.int64)
        tiles.append(np.where(mult > 0, np.log2(np.maximum(mult, 1)), MASKED))
    return np.concatenate(tiles, axis=1).astype(np.float32)


def _dilated_kernel(q_ref, k_ref, v_ref, bias_ref, bias4_ref, o_ref,
                    qp_ref, kp_ref, vp_ref, q4_ref, k4_ref, v4_ref,
                    acc_ref, m_ref, l_ref, acc4_ref, m4_ref, l4_ref, *, group):
    seq = acc_ref.shape[0]
    n_all = seq // BAND
    quarter = seq // 4
    n_class = quarter // BAND
    bias_full = bias_ref[...]
    col = lax.broadcasted_iota(jnp.int32, (BAND, 2 * BAND), 1)
    bias_first = jnp.where(col < BAND, MASKED, bias_full)
    batch_nt = (((2,), (2,)), ((0,), (0,)))
    batch_nn = (((2,), (1,)), ((0,), (0,)))

    def by4_rows(c):
        return pl.ds(c, quarter, stride=4)

    def softmax_parts(items):
        s = [lax.dot_general(q, k, batch_nt, preferred_element_type=_F32) for q, k, _, _ in items]
        s = [jnp.stack([s_i[g] + biases[g] for g in range(len(biases))]) for s_i, (_, _, _, biases) in zip(s, items)]
        m = [jnp.max(s_i, axis=-1, keepdims=True) for s_i in s]
        p = [jnp.exp2(s_i - m_i) for s_i, m_i in zip(s, m)]
        l = [jnp.sum(p_i, axis=-1, keepdims=True) for p_i in p]
        num = [lax.dot_general(p_i.astype(_BF16), v, batch_nn, preferred_element_type=_F32)
               for p_i, (_, _, v, _) in zip(p, items)]
        return [(n_i, jnp.broadcast_to(m_i, n_i.shape), jnp.broadcast_to(l_i, n_i.shape))
                for n_i, m_i, l_i in zip(num, m, l)]

    qp_ref[...] = q_ref[0].astype(_BF16)
    kp_ref[...] = k_ref[0].astype(_BF16)
    vp_ref[...] = v_ref[0].astype(_BF16)
    for c in range(4):
        seg = slice(c * quarter, (c + 1) * quarter)
        q4_ref[seg, :] = q_ref[0, by4_rows(c), :].astype(_BF16)
        k4_ref[seg, :] = k_ref[0, by4_rows(c), :].astype(_BF16)
        v4_ref[seg, :] = v_ref[0, by4_rows(c), :].astype(_BF16)

    items = []
    for g0 in range(0, n_all, group):
        cur = slice(g0 * BAND, (g0 + group) * BAND)
        if g0 == 0:
            prev_of = lambda ref: jnp.concatenate([ref[0:BAND, :], ref[0:(group - 1) * BAND, :]], axis=0)
        else:
            prev_of = lambda ref, g0=g0: ref[(g0 - 1) * BAND:(g0 + group - 1) * BAND, :]
        shape = (group, BAND, A_DIM)
        k = jnp.concatenate([prev_of(kp_ref).reshape(shape), kp_ref[cur, :].reshape(shape)], axis=1)
        v = jnp.concatenate([prev_of(vp_ref).reshape(shape), vp_ref[cur, :].reshape(shape)], axis=1)
        biases = [bias_first if g0 + g == 0 else bias_full for g in range(group)]
        items.append((qp_ref[cur, :].reshape(shape), k, v, biases))
    n_position = len(items)
    bias4 = bias4_ref[...]
    for n in range(n_class):
        q = jnp.stack([q4_ref[c * quarter + n * BAND:c * quarter + (n + 1) * BAND, :] for c in range(4)])
        k = jnp.stack([k4_ref[c * quarter:c * quarter + (n + 1) * BAND, :] for c in range(4)])
        v = jnp.stack([v4_ref[c * quarter:c * quarter + (n + 1) * BAND, :] for c in range(4)])
        items.append((q, k, v, [bias4[:, (n_class - 1 - n) * BAND:]] * 4))

    parts = softmax_parts(items)
    for i, (num, m, l) in enumerate(parts[:n_position]):
        cur = slice(i * group * BAND, (i + 1) * group * BAND)
        acc_ref[cur, :] = num.reshape(group * BAND, A_DIM)
        m_ref[cur, :] = m.reshape(group * BAND, A_DIM)
        l_ref[cur, :] = l.reshape(group * BAND, A_DIM)
    for n, (num, m, l) in enumerate(parts[n_position:]):
        for c in range(4):
            rows = slice(c * quarter + n * BAND, c * quarter + (n + 1) * BAND)
            acc4_ref[rows, :] = num[c]
            m4_ref[rows, :] = m[c]
            l4_ref[rows, :] = l[c]

    for c in range(4):
        seg = slice(c * quarter, (c + 1) * quarter)
        rows = by4_rows(c)
        m_a, m_b = m_ref[rows, :], m4_ref[seg, :]
        m_new = jnp.maximum(m_a, m_b)
        a = jnp.exp2(m_a - m_new)
        b = jnp.exp2(m_b - m_new)
        num = acc_ref[rows, :] * a + acc4_ref[seg, :] * b
        den = l_ref[rows, :] * a + l4_ref[seg, :] * b
        acc_ref[rows, :] = num / den
    o_ref[0] = acc_ref[...].astype(o_ref.dtype)


def _dilated_attention(proj, bias, bias4, group):
    bsz, seq, _ = proj.shape
    head = lambda off: pl.BlockSpec((1, seq, A_DIM), lambda b, h, off=off: (b, 0, off + h))
    blocks = (3 * _nbytes((seq, A_DIM), _F32) + _nbytes((seq, A_DIM), _BF16)
              + _nbytes(bias.shape, _F32) + _nbytes(bias4.shape, _F32))
    scratch = 6 * _nbytes((seq, LANES), _F32) + 6 * _nbytes((seq, LANES), _BF16)
    n_blocks = seq // BAND
    temporaries = 2 * (2 * n_blocks + 10) * _nbytes((BAND, BAND), _F32)
    return pl.pallas_call(
        functools.partial(_dilated_kernel, group=group),
        grid=(bsz, A_HEADS),
        in_specs=[head(0), head(A_HEADS), head(2 * A_HEADS),
                  pl.BlockSpec(bias.shape, lambda b, h: (0, 0)),
                  pl.BlockSpec(bias4.shape, lambda b, h: (0, 0))],
        out_specs=pl.BlockSpec((1, seq, A_DIM), lambda b, h: (b, 0, h)),
        out_shape=jax.ShapeDtypeStruct((bsz, seq, A_WIDTH), _BF16),
        scratch_shapes=[pltpu.VMEM((seq, A_DIM), _BF16) for _ in range(6)]
        + [pltpu.VMEM((seq, LANES), _F32) for _ in range(6)],
        compiler_params=pltpu.CompilerParams(
            dimension_semantics=("arbitrary", "arbitrary"),
            vmem_limit_bytes=_vmem_limit(blocks, scratch + temporaries)),
        name="dilated",
    )(proj, proj, proj, bias, bias4)


_B_LEVELS = (32, 16, 8, 4, 2, 1)


def _hgrn_constants():
    c = B_CHUNK
    t = np.arange(c)[:, None]
    r = np.arange(c)[None, :]
    mats = [(r <= t), (r > t)]
    for h in _B_LEVELS:
        base = (t // (2 * h)) * (2 * h)
        mid = base + h - 1
        upper = (t & h) != 0
        mats.append(np.where(upper, (r > mid) & (r <= t), (r > t) & (r <= mid)))
    sums = np.concatenate(mats, axis=0).astype(np.float32)
    s = np.arange(c)[None, :]
    x = t ^ s
    level = np.where(s > t, -1, np.where(s == t, 0, 1 << (np.floor(np.log2(np.maximum(x, 1))).astype(np.int64))))
    return np.concatenate([sums, sums], axis=1), level.astype(np.int32)


def _hgrn_kernel(q_ref, f_ref, i_ref, g_ref, lb_ref, gain_ref, sums_ref, level_ref, o_ref, *, group):
    seq = q_ref.shape[1]
    c = B_CHUNK
    span = group * c
    lb = lb_ref[0]
    gain = gain_ref[...]
    sums2 = sums_ref[...]
    level = level_ref[...]
    chunks = range(group)

    def piece(x, j):
        return x[j * c:(j + 1) * c]

    def step(n, state_t):
        rows = pl.ds(pl.multiple_of(n * span, span), span)
        gate = lb + (1.0 - lb) * _sigmoid(f_ref[0, rows, :])
        log_gate = jnp.log2(gate)
        k = 1.0 - gate
        qraw = q_ref[0, rows, :]
        q = qraw * _sigmoid(qraw) * (B_DIM ** -0.5)
        v = i_ref[0, rows, :].astype(_BF16)

        g_hi = log_gate.astype(_BF16)
        g_lo = (log_gate - g_hi.astype(_F32)).astype(_BF16)
        hi_lo = [jnp.concatenate([piece(g_hi, j), piece(g_lo, j)], axis=0) for j in chunks]
        expo = []
        for j in range(0, group, 2):
            both = jnp.dot(sums2, jnp.concatenate([hi_lo[j], hi_lo[j + 1]], axis=1),
                           preferred_element_type=_F32)
            expo += [both[:, :B_DIM], both[:, B_DIM:]]
        decay = [jnp.exp2(e) for e in expo]

        qc = [piece(q, j) for j in chunks]
        kc = [piece(k, j) for j in chunks]
        vc = [piece(v, j) for j in chunks]
        q_bf = [qc[j].astype(_BF16) for j in chunks]
        k_bf = [kc[j].astype(_BF16) for j in chunks]
        level_decay = [[piece(decay[j], 2 + i).astype(_BF16) for i in range(len(_B_LEVELS))] for j in chunks]
        scaled_q = [[q_bf[j]] + [q_bf[j] * d for d in level_decay[j]] for j in chunks]
        scaled_k = [[k_bf[j]] + [k_bf[j] * d for d in level_decay[j]] for j in chunks]
        q_in = [(qc[j] * piece(decay[j], 0)).astype(_BF16) for j in chunks]
        k_out = [(kc[j] * piece(decay[j], 1)).astype(_BF16) for j in chunks]
        parts = [[lax.dot_general(a, b, _NT, preferred_element_type=_F32)
                  for a, b in zip(scaled_q[j], scaled_k[j])] for j in chunks]
        attn = []
        for j in chunks:
            total = jnp.where(level == 0, parts[j][0], 0.0)
            for i, h in enumerate(_B_LEVELS):
                total = total + jnp.where(level == h, parts[j][1 + i], 0.0)
            attn.append(total.astype(_BF16))
        intra = [jnp.dot(attn[j], vc[j], preferred_element_type=_F32) for j in chunks]
        update = [lax.dot_general(vc[j], k_out[j], _TN, preferred_element_type=_F32) for j in chunks]

        outs = []
        for j in chunks:
            outs.append(intra[j] + lax.dot_general(q_in[j], state_t.astype(_BF16), _NT,
                                                   preferred_element_type=_F32))
            state_t = state_t * decay[j][c - 1:c] + update[j]

        y = _rmsnorm_rows(jnp.concatenate(outs, axis=0), gain)
        graw = g_ref[0, rows, :]
        y = y * (graw * _sigmoid(graw))
        o_ref[0, rows, :] = y.astype(o_ref.dtype)
        return state_t

    lax.fori_loop(0, seq // span, step, jnp.zeros((B_DIM, B_DIM), _F32))


def _hgrn2(proj, lower_bound, gain, sums, level, group):
    bsz, seq, _ = proj.shape
    assert group % 2 == 0 and seq % (group * B_CHUNK) == 0
    head = lambda off: pl.BlockSpec((1, seq, B_DIM), lambda b, h, off=off: (b, 0, off + h))
    blocks = 4 * _nbytes((seq, B_DIM), _F32) + _nbytes((seq, B_DIM), _BF16) \
        + _nbytes(sums.shape, _BF16) + _nbytes(level.shape, jnp.int32) + 2 * _nbytes((8, LANES), _F32)
    temporaries = 3 * group * _nbytes((sums.shape[0], B_DIM), _F32)
    return pl.pallas_call(
        functools.partial(_hgrn_kernel, group=group),
        grid=(bsz, B_HEADS),
        in_specs=[head(0), head(B_HEADS), head(2 * B_HEADS), head(3 * B_HEADS),
                  pl.BlockSpec((1, 1, B_DIM), lambda b, h: (h, 0, 0)),
                  pl.BlockSpec((1, B_DIM), lambda b, h: (0, 0)),
                  pl.BlockSpec(sums.shape, lambda b, h: (0, 0)),
                  pl.BlockSpec(level.shape, lambda b, h: (0, 0))],
        out_specs=pl.BlockSpec((1, seq, B_DIM), lambda b, h: (b, 0, h)),
        out_shape=jax.ShapeDtypeStruct((bsz, seq, B_WIDTH), _BF16),
        compiler_params=pltpu.CompilerParams(
            dimension_semantics=("arbitrary", "arbitrary"),
            vmem_limit_bytes=_vmem_limit(blocks, temporaries)),
        name="hgrn2",
    )(proj, proj, proj, proj, lower_bound.reshape(B_HEADS, 1, B_DIM), gain.reshape(1, B_DIM), sums, level)


def _swa_kernel(q_ref, k_ref, v_ref, sink_ref, bias_ref, o_ref, klo_ref, khi_ref, vlo_ref, vhi_ref, *, joint):
    seq = q_ref.shape[1]
    kv_head = pl.program_id(1)
    lane_half = lax.broadcasted_iota(jnp.int32, (seq, LANES), 1) // C_DIM
    odd_head = (kv_head % 2) == 1

    def place(x):
        own = jnp.where(lane_half == kv_head % 2, x, 0.0)
        other = pltpu.roll(own, C_DIM, 1)
        lo = jnp.where(odd_head, other, own)
        hi = jnp.where(odd_head, own, other)
        return lo.astype(_BF16), hi.astype(_BF16)

    klo_ref[...], khi_ref[...] = place(k_ref[0])
    vlo_ref[...], vhi_ref[...] = place(v_ref[0])
    pairs = C_GROUP // 2

    heads = [(j, side) for side in range(2) for j in range(pairs)]
    sinks = [sink_ref[kv_head * C_GROUP + 2 * j + side] * LOG2E for j, side in heads]
    low_q = lax.broadcasted_iota(jnp.int32, (BAND, LANES), 1) < C_DIM
    k_refs, v_refs = (klo_ref, khi_ref), (vlo_ref, vhi_ref)

    def attend(block_ids, with_prev):
        curs = [pl.ds(pl.multiple_of(n * BAND, BAND), BAND) for n in block_ids]
        if with_prev:
            keys = [pl.ds(pl.multiple_of((n - 1) * BAND, BAND), 2 * BAND) for n in block_ids]
            bias = bias_ref[...]
        else:
            keys = curs
            bias = bias_ref[:, BAND:]
        count = range(len(block_ids))
        q4 = [jnp.concatenate([q_ref[0, curs[b], j * LANES:(j + 1) * LANES] for j in range(pairs)],
                              axis=0).astype(_BF16) for b in count]
        s_side = [[lax.dot_general(q4[b], k_refs[side][keys[b], :], _NT, preferred_element_type=_F32)
                   for side in range(2)] for b in count]
        s = [[s_side[b][side][j * BAND:(j + 1) * BAND] + bias for j, side in heads] for b in count]
        m = [[jnp.maximum(jnp.max(s_h, axis=-1, keepdims=True), sink) for s_h, sink in zip(s[b], sinks)]
             for b in count]
        p = [[jnp.exp2(s_h - m_h) for s_h, m_h in zip(s[b], m[b])] for b in count]
        sums = [[jnp.sum(p_h, axis=-1, keepdims=True) for p_h in p[b]] for b in count]
        num = [[jnp.dot(jnp.concatenate(p[b][side * pairs:(side + 1) * pairs], axis=0).astype(_BF16),
                        v_refs[side][keys[b], :], preferred_element_type=_F32) for side in range(2)]
               for b in count]
        both = lambda lo, hi: jnp.where(low_q, lo, hi)
        den = [[both(sums[b][j], sums[b][pairs + j])
                + jnp.exp2(both(sinks[j], sinks[pairs + j]) - both(m[b][j], m[b][pairs + j]))
                for j in range(pairs)] for b in count]
        for b in count:
            for j in range(pairs):
                rows = slice(j * BAND, (j + 1) * BAND)
                out = both(num[b][0][rows], num[b][1][rows]) / den[b][j]
                o_ref[0, curs[b], j * LANES:(j + 1) * LANES] = out.astype(o_ref.dtype)

    attend([0], False)
    later = seq // BAND - 1
    rounds = later // joint

    def round_(t, carry):
        attend([1 + t * joint + b for b in range(joint)], True)
        return carry

    lax.fori_loop(0, rounds, round_, 0)
    if later % joint:
        attend(list(range(1 + rounds * joint, later + 1)), True)


def _swa(proj, sinks, bias, joint):
    bsz, seq, _ = proj.shape
    qw = C_GROUP * C_DIM
    k_base = C_Q_HEADS * C_DIM // LANES
    v_base = k_base + C_KV_HEADS * C_DIM // LANES
    blocks = _nbytes((seq, qw), _F32) + 2 * _nbytes((seq, LANES), _F32) + _nbytes((seq, qw), _BF16) \
        + _nbytes(bias.shape, _F32)
    scratch = 4 * _nbytes((seq, LANES), _BF16)
    temporaries = 2 * joint * C_GROUP * _nbytes((BAND, 2 * BAND), _F32)
    return pl.pallas_call(
        functools.partial(_swa_kernel, joint=joint),
        grid=(bsz, C_KV_HEADS),
        in_specs=[pl.BlockSpec((1, seq, qw), lambda b, h: (b, 0, h)),
                  pl.BlockSpec((1, seq, LANES), lambda b, h: (b, 0, k_base + h // 2)),
                  pl.BlockSpec((1, seq, LANES), lambda b, h: (b, 0, v_base + h // 2)),
                  pl.BlockSpec(memory_space=pltpu.SMEM),
                  pl.BlockSpec(bias.shape, lambda b, h: (0, 0))],
        out_specs=pl.BlockSpec((1, seq, qw), lambda b, h: (b, 0, h)),
        out_shape=jax.ShapeDtypeStruct((bsz, seq, C_Q_HEADS * C_DIM), _BF16),
        scratch_shapes=[pltpu.VMEM((seq, LANES), _BF16) for _ in range(4)],
        compiler_params=pltpu.CompilerParams(
            dimension_semantics=("arbitrary", "arbitrary"),
            vmem_limit_bytes=_vmem_limit(blocks, scratch + temporaries)),
        name="swa",
    )(proj, proj, proj, sinks.astype(_F32), bias)


def kernel(x, norm_mix_g, norm_mlp_g, final_norm_g, even_w_in, even_w_out, hgrn_lb_raw, hgrn_norm_g,
           odd_w_qkv, odd_b_qkv, odd_sinks, odd_w_o, odd_b_o, mlp_w1, mlp_w2):
    bsz, seq, d = x.shape
    depth = norm_mix_g.shape[0]
    t = _tiles()
    m = bsz * seq

    tables_a = _rope_tables(seq, A_DIM)
    tables_c = _rope_tables(seq, C_DIM)
    bias_a = jnp.asarray(_band_bias_window(BAND))
    bias_a4 = jnp.asarray(_by4_bias(seq // 4))
    bias_c = jnp.asarray(_band_bias_window(C_WINDOW - 1))
    sums_np, level_np = _hgrn_constants()
    sums, level = jnp.asarray(sums_np, _BF16), jnp.asarray(level_np)

    lb_soft = jax.nn.softmax(hgrn_lb_raw.astype(_F32), axis=0)
    lower_bounds = jnp.cumsum(lb_soft, axis=0) - lb_soft[0:1]

    w_in_a = even_w_in[:, :, :A_IN].astype(_BF16)
    w_in_b = even_w_in[:, :, A_IN:].astype(_BF16)
    w_out = even_w_out.astype(_BF16)
    w_qkv = odd_w_qkv.astype(_BF16)
    w_o = odd_w_o.astype(_BF16)
    w1 = mlp_w1.astype(_BF16)
    w2 = mlp_w2.astype(_BF16)

    group_a = A_WIDTH // LANES
    rope_a = [A_DIM ** -0.5 * LOG2E] * group_a + [1.0] * group_a + [None] * group_a
    q_groups = C_Q_HEADS * C_DIM // LANES
    kv_groups = C_KV_HEADS * C_DIM // LANES
    rope_c = [C_DIM ** -0.5 * LOG2E] * q_groups + [1.0] * kv_groups + [None] * kv_groups
    zeros_d = jnp.zeros((d,), _F32)

    x2 = x.reshape(m, d)
    for layer in range(depth):
        if layer % 2 == 0:
            e = layer // 2
            proj_a, h_mix = _norm_proj(x2, norm_mix_g[layer], w_in_a, e, A_IN, jnp.zeros((A_IN,), _F32),
                                       tables_a, rope_a, seq, t["proj_tm"], emit_h=True)
            proj_a = proj_a.reshape(bsz, seq, A_IN)
            proj_b = _proj(h_mix, w_in_b, e, t["proj_tm"]).reshape(bsz, seq, B_IN)
            oa = _dilated_attention(proj_a, bias_a, bias_a4, t["dilated_group"]).reshape(m, A_WIDTH)
            ob = _hgrn2(proj_b, lower_bounds[e], hgrn_norm_g[e], sums, level, t["hgrn_group"]).reshape(m, B_WIDTH)
            x2, h2 = _res_matmul([oa, ob], w_out, e, zeros_d, x2, norm_mlp_g[layer], t["out_tm"])
        else:
            o = layer // 2
            proj = _norm_proj(x2, norm_mix_g[layer], w_qkv, o, C_QKV, odd_b_qkv[o],
                              tables_c, rope_c, seq, t["proj_tm"], emit_h=False).reshape(bsz, seq, C_QKV)
            attn = _swa(proj, odd_sinks[o], bias_c, t["swa_joint"]).reshape(m, C_Q_HEADS * C_DIM)
            x2, h2 = _res_matmul([attn], w_o, o, odd_b_o[o], x2, norm_mlp_g[layer], t["out_tm"])
        last = layer == depth - 1
        x2 = _mlp(x2, h2, w1, w2, layer, final_norm_g if last else None, t["mlp_tm"], t["mlp_tf"])
    return x2.reshape(bsz, seq, d)
```

```python
import functools
import math

import numpy as np
import jax
import jax.numpy as jnp
from jax import lax
from jax.experimental import pallas as pl
from jax.experimental.pallas import tpu as pltpu

NORM_EPS = 1e-5
ROPE_THETA = 500000.0
ROPE_FRACTION = 4
BAND = 128

A_DIM = 128
A_HEADS = 8
A_BRANCHES = ((128, 1), (512, 4), (2048, 16))
A_WIDTH = A_HEADS * A_DIM
B_DIM = 128
B_HEADS = 8
B_WIDTH = B_HEADS * B_DIM
B_CHUNK = 64
A_IN = 3 * A_WIDTH
B_IN = 4 * B_WIDTH

C_DIM = 64
C_Q_HEADS = 32
C_KV_HEADS = 4
C_GROUP = C_Q_HEADS // C_KV_HEADS
C_WINDOW = 128
C_QKV = (C_Q_HEADS + 2 * C_KV_HEADS) * C_DIM

LANES = 128
V7X_VMEM_BYTES = 64 * 1024 * 1024
COMPILER_TEMP_BYTES = 6 * 1024 * 1024

MASKED = -1e30
LOG2E = math.log2(math.e)

_BF16 = jnp.bfloat16
_F32 = jnp.float32
_NT = (((1,), (1,)), ((), ()))
_TN = (((0,), (0,)), ((), ()))


def _tiles():
    return dict(
        proj_tm=512,
        out_tm=512,
        mlp_tm=1024, mlp_tf=1024,
        dilated_group=16,
        hgrn_group=32,
        swa_joint=15,
    )


def _vmem_limit(pipelined_bytes, single_bytes=0):
    need = 2 * pipelined_bytes + single_bytes + COMPILER_TEMP_BYTES
    assert need <= V7X_VMEM_BYTES, need
    return int(need)


def _nbytes(shape, dtype):
    return int(np.prod(shape)) * jnp.dtype(dtype).itemsize


def _rmsnorm_rows(x, g):
    ms = jnp.mean(x * x, axis=-1, keepdims=True)
    return x * lax.rsqrt(ms + NORM_EPS) * g


def _sigmoid(x):
    return 1.0 / (1.0 + jnp.exp2(x * (-LOG2E)))


def _rope_tables(seq, head_dim):
    rot = head_dim // ROPE_FRACTION
    half = rot // 2
    inv_freq = 1.0 / (ROPE_THETA ** (jnp.arange(0, rot, 2, dtype=_F32) / rot))
    ang = jnp.arange(seq, dtype=_F32)[:, None] * inv_freq[None, :]
    cos, sin = jnp.cos(ang), jnp.sin(ang)
    pad = head_dim - 2 * half
    ones = jnp.ones((seq, pad), _F32)
    zeros = jnp.zeros((seq, pad), _F32)
    zh = jnp.zeros((seq, half), _F32)
    reps = LANES // head_dim
    cos_full = jnp.tile(jnp.concatenate([cos, cos, ones], axis=1), (1, reps))
    sin_lo = jnp.tile(jnp.concatenate([-sin, zh, zeros], axis=1), (1, reps))
    sin_hi = jnp.tile(jnp.concatenate([zh, sin, zeros], axis=1), (1, reps))
    return cos_full, sin_lo, sin_hi, half


def _rope(x, cos_full, sin_lo, sin_hi, half):
    up = pltpu.roll(x, LANES - half, 1)
    down = pltpu.roll(x, half, 1)
    return x * cos_full + up * sin_lo + down * sin_hi


def _band_bias_window(window):
    i = np.arange(BAND)[:, None]
    j = np.arange(2 * BAND)[None, :]
    dist = i + BAND - j
    return np.where((dist >= 0) & (dist <= window), 0.0, MASKED).astype(np.float32)


def _norm_proj_kernel(*refs, rope_scales, half, emit_h):
    refs = list(refs)
    h_ref = refs.pop() if emit_h else None
    x_ref, g_ref, w_ref, b_ref, cos_ref, slo_ref, shi_ref, o_ref = refs
    cos_full, sin_lo, sin_hi = cos_ref[...], slo_ref[...], shi_ref[...]
    h = _rmsnorm_rows(x_ref[...], g_ref[...]).astype(_BF16)
    if emit_h:
        h_ref[...] = h
    acc = jnp.dot(h, w_ref[0], preferred_element_type=_F32) + b_ref[...]
    for gi, scale in enumerate(rope_scales):
        sl = slice(gi * LANES, (gi + 1) * LANES)
        if scale is None:
            o_ref[:, sl] = acc[:, sl]
        else:
            o_ref[:, sl] = _rope(acc[:, sl], cos_full, sin_lo, sin_hi, half) * scale


def _norm_proj(x, g, w_stack, layer, n, b, tables, rope_scales, seq, tm, emit_h):
    m, d = x.shape
    assert len(rope_scales) * LANES == n
    cos_full, sin_lo, sin_hi, half = tables
    per_seq = seq // tm
    in_specs = [
        pl.BlockSpec((tm, d), lambda i: (i, 0)),
        pl.BlockSpec((1, d), lambda i: (0, 0)),
        pl.BlockSpec((1, d, n), lambda i: (layer, 0, 0), pipeline_mode=pl.Buffered(1)),
        pl.BlockSpec((1, n), lambda i: (0, 0)),
    ] + [pl.BlockSpec((tm, LANES), lambda i: (i % per_seq, 0))] * 3
    blocks = (_nbytes((tm, d), _F32) + _nbytes((tm, n), _F32) + _nbytes((1, d), _F32) + _nbytes((1, n), _F32)
              + 3 * _nbytes((tm, LANES), _F32))
    out_specs = [pl.BlockSpec((tm, n), lambda i: (i, 0))]
    out_shape = [jax.ShapeDtypeStruct((m, n), _F32)]
    if emit_h:
        out_specs.append(pl.BlockSpec((tm, d), lambda i: (i, 0)))
        out_shape.append(jax.ShapeDtypeStruct((m, d), _BF16))
        blocks += _nbytes((tm, d), _BF16)
    single = _nbytes((d, n), _BF16) + _nbytes((tm, d), _BF16) + _nbytes((tm, n), _F32)
    outs = pl.pallas_call(
        functools.partial(_norm_proj_kernel, rope_scales=tuple(rope_scales), half=half, emit_h=emit_h),
        grid=(m // tm,),
        in_specs=in_specs,
        out_specs=out_specs,
        out_shape=out_shape,
        compiler_params=pltpu.CompilerParams(
            dimension_semantics=("arbitrary",), vmem_limit_bytes=_vmem_limit(blocks, single)),
        name="norm_proj",
    )(x, g.reshape(1, d), w_stack, b.reshape(1, n), cos_full, sin_lo, sin_hi)
    return outs if emit_h else outs[0]


def _proj_kernel(h_ref, w_ref, o_ref):
    o_ref[...] = jnp.dot(h_ref[...], w_ref[0], preferred_element_type=_F32)


def _proj(h, w_stack, layer, tm):
    m, d = h.shape
    n = w_stack.shape[2]
    blocks = _nbytes((tm, d), _BF16) + _nbytes((tm, n), _F32)
    return pl.pallas_call(
        _proj_kernel,
        grid=(m // tm,),
        in_specs=[pl.BlockSpec((tm, d), lambda i: (i, 0)),
                  pl.BlockSpec((1, d, n), lambda i: (layer, 0, 0), pipeline_mode=pl.Buffered(1))],
        out_specs=pl.BlockSpec((tm, n), lambda i: (i, 0)),
        out_shape=jax.ShapeDtypeStruct((m, n), _F32),
        compiler_params=pltpu.CompilerParams(
            dimension_semantics=("arbitrary",),
            vmem_limit_bytes=_vmem_limit(blocks, _nbytes((d, n), _BF16) + _nbytes((tm, n), _F32))),
        name="proj",
    )(h, w_stack)


def _res_matmul_kernel(*refs, n_in):
    a_refs, w_refs = refs[:n_in], refs[n_in:2 * n_in]
    b_ref, r_ref, g_ref, o_ref, h_ref = refs[2 * n_in:]
    acc = r_ref[...] + b_ref[...]
    for a_ref, w_ref in zip(a_refs, w_refs):
        acc = acc + jnp.dot(a_ref[...], w_ref[0], preferred_element_type=_F32)
    o_ref[...] = acc
    h_ref[...] = _rmsnorm_rows(acc, g_ref[...]).astype(_BF16)


def _res_matmul(a_list, w_stack, layer, b, res, g_next, tm):
    m, n = res.shape
    k = a_list[0].shape[1]
    blocks = (_nbytes((tm, n), _F32) * 2 + _nbytes((tm, n), _BF16) + 2 * _nbytes((1, n), _F32)
              + len(a_list) * _nbytes((tm, k), _BF16))
    in_specs = [pl.BlockSpec((tm, k), lambda i: (i, 0)) for _ in a_list]
    in_specs += [pl.BlockSpec((1, k, n), lambda i, r=r: (layer, r, 0), pipeline_mode=pl.Buffered(1))
                 for r in range(len(a_list))]
    in_specs += [pl.BlockSpec((1, n), lambda i: (0, 0)), pl.BlockSpec((tm, n), lambda i: (i, 0)),
                 pl.BlockSpec((1, n), lambda i: (0, 0))]
    return pl.pallas_call(
        functools.partial(_res_matmul_kernel, n_in=len(a_list)),
        grid=(m // tm,),
        in_specs=in_specs,
        out_specs=[pl.BlockSpec((tm, n), lambda i: (i, 0)), pl.BlockSpec((tm, n), lambda i: (i, 0))],
        out_shape=[jax.ShapeDtypeStruct((m, n), _F32), jax.ShapeDtypeStruct((m, n), _BF16)],
        compiler_params=pltpu.CompilerParams(
            dimension_semantics=("arbitrary",),
            vmem_limit_bytes=_vmem_limit(blocks, len(a_list) * _nbytes((k, n), _BF16))),
        name="res_matmul",
    )(*a_list, *([w_stack] * len(a_list)), b.reshape(1, n), res, g_next.reshape(1, n))


def _mlp_kernel(*refs, final):
    if final:
        x_ref, h_ref, w1_ref, w2_ref, gf_ref, o_ref = refs
    else:
        x_ref, h_ref, w1_ref, w2_ref, o_ref = refs
    j = pl.program_id(1)
    a = jnp.dot(h_ref[...], w1_ref[0], preferred_element_type=_F32)
    a = jnp.square(jnp.maximum(a, 0.0)).astype(_BF16)
    so_far = jnp.where(j == 0, 0.0, o_ref[...])
    o_ref[...] = so_far + jnp.dot(a, w2_ref[0], preferred_element_type=_F32)
    piece = x_ref.shape[0]
    rows = pl.ds(pl.multiple_of(j * piece, piece), piece)
    o_ref[rows, :] += x_ref[...]
    if final:
        @pl.when(j == pl.num_programs(1) - 1)
        def _():
            o_ref[...] = _rmsnorm_rows(o_ref[...], gf_ref[...])


def _mlp(x, h, w1_stack, w2_stack, layer, final_gain, tm, tf):
    m, d = x.shape
    f = w1_stack.shape[2]
    n_tiles = f // tf
    piece = tm // n_tiles
    blocks = (_nbytes((tm, d), _F32) + _nbytes((piece, d), _F32) + _nbytes((tm, d), _BF16) + _nbytes((1, d), _F32)
              + _nbytes((d, tf), _BF16) + _nbytes((tf, d), _BF16))
    temporaries = _nbytes((tm, tf), _F32) + _nbytes((tm, tf), _BF16)
    in_specs = [
        pl.BlockSpec((piece, d), lambda i, j: (i * n_tiles + j, 0)),
        pl.BlockSpec((tm, d), lambda i, j: (i, 0)),
        pl.BlockSpec((1, d, tf), lambda i, j: (layer, 0, j)),
        pl.BlockSpec((1, tf, d), lambda i, j: (layer, j, 0)),
    ]
    args = [x, h, w1_stack, w2_stack]
    if final_gain is not None:
        in_specs.append(pl.BlockSpec((1, d), lambda i, j: (0, 0)))
        args.append(final_gain.reshape(1, d))
    return pl.pallas_call(
        functools.partial(_mlp_kernel, final=final_gain is not None),
        grid=(m // tm, f // tf),
        in_specs=in_specs,
        out_specs=pl.BlockSpec((tm, d), lambda i, j: (i, 0)),
        out_shape=jax.ShapeDtypeStruct((m, d), _F32),
        compiler_params=pltpu.CompilerParams(
            dimension_semantics=("arbitrary", "arbitrary"),
            vmem_limit_bytes=_vmem_limit(blocks, temporaries)),
        name="mlp",
    )(*args)


def _by4_bias(class_len):
    assert A_BRANCHES == ((128, 1), (512, 4), (2048, 16))
    blocks = class_len // BAND
    i = np.arange(BAND)[:, None]
    j = np.arange(BAND)[None, :]
    tiles = []
    for off in range(blocks - 1, -1, -1):
        dist = off * BAND + i - j
        in_d4 = (dist >= 0) & (dist <= A_BRANCHES[1][0] // A_BRANCHES[1][1])
        in_d16 = (dist >= 0) & (dist % 4 == 0) & (dist // 4 <= A_BRANCHES[2][0] // A_BRANCHES[2][1])
        mult = in_d4.astype(np.int64) + in_d16.astype(np.int64)
        tiles.append(np.where(mult > 0, np.log2(np.maximum(mult, 1)), MASKED))
    return np.concatenate(tiles, axis=1).astype(np.float32)


def _dilated_kernel(q_ref, k_ref, v_ref, bias_ref, bias4_ref, o_ref,
                    qp_ref, kp_ref, vp_ref, q4_ref, k4_ref, v4_ref,
                    acc_ref, m_ref, l_ref, acc4_ref, m4_ref, l4_ref, *, group):
    seq = acc_ref.shape[0]
    n_all = seq // BAND
    quarter = seq // 4
    n_class = quarter // BAND
    bias_full = bias_ref[...]
    col = lax.broadcasted_iota(jnp.int32, (BAND, 2 * BAND), 1)
    bias_first = jnp.where(col < BAND, MASKED, bias_full)
    batch_nt = (((2,), (2,)), ((0,), (0,)))
    batch_nn = (((2,), (1,)), ((0,), (0,)))

    def by4_rows(c):
        return pl.ds(c, quarter, stride=4)

    def softmax_parts(items):
        s = [lax.dot_general(q, k, batch_nt, preferred_element_type=_F32) for q, k, _, _ in items]
        s = [jnp.stack([s_i[g] + biases[g] for g in range(len(biases))]) for s_i, (_, _, _, biases) in zip(s, items)]
        m = [jnp.max(s_i, axis=-1, keepdims=True) for s_i in s]
        p = [jnp.exp2(s_i - m_i) for s_i, m_i in zip(s, m)]
        l = [jnp.sum(p_i, axis=-1, keepdims=True) for p_i in p]
        num = [lax.dot_general(p_i.astype(_BF16), v, batch_nn, preferred_element_type=_F32)
               for p_i, (_, _, v, _) in zip(p, items)]
        return [(n_i, jnp.broadcast_to(m_i, n_i.shape), jnp.broadcast_to(l_i, n_i.shape))
                for n_i, m_i, l_i in zip(num, m, l)]

    qp_ref[...] = q_ref[0].astype(_BF16)
    kp_ref[...] = k_ref[0].astype(_BF16)
    vp_ref[...] = v_ref[0].astype(_BF16)
    for c in range(4):
        seg = slice(c * quarter, (c + 1) * quarter)
        q4_ref[seg, :] = q_ref[0, by4_rows(c), :].astype(_BF16)
        k4_ref[seg, :] = k_ref[0, by4_rows(c), :].astype(_BF16)
        v4_ref[seg, :] = v_ref[0, by4_rows(c), :].astype(_BF16)

    items = []
    for g0 in range(0, n_all, group):
        cur = slice(g0 * BAND, (g0 + group) * BAND)
        if g0 == 0:
            prev_of = lambda ref: jnp.concatenate([ref[0:BAND, :], ref[0:(group - 1) * BAND, :]], axis=0)
        else:
            prev_of = lambda ref, g0=g0: ref[(g0 - 1) * BAND:(g0 + group - 1) * BAND, :]
        shape = (group, BAND, A_DIM)
        k = jnp.concatenate([prev_of(kp_ref).reshape(shape), kp_ref[cur, :].reshape(shape)], axis=1)
        v = jnp.concatenate([prev_of(vp_ref).reshape(shape), vp_ref[cur, :].reshape(shape)], axis=1)
        biases = [bias_first if g0 + g == 0 else bias_full for g in range(group)]
        items.append((qp_ref[cur, :].reshape(shape), k, v, biases))
    n_position = len(items)
    bias4 = bias4_ref[...]
    for n in range(n_class):
        q = jnp.stack([q4_ref[c * quarter + n * BAND:c * quarter + (n + 1) * BAND, :] for c in range(4)])
        k = jnp.stack([k4_ref[c * quarter:c * quarter + (n + 1) * BAND, :] for c in range(4)])
        v = jnp.stack([v4_ref[c * quarter:c * quarter + (n + 1) * BAND, :] for c in range(4)])
        items.append((q, k, v, [bias4[:, (n_class - 1 - n) * BAND:]] * 4))

    parts = softmax_parts(items)
    for i, (num, m, l) in enumerate(parts[:n_position]):
        cur = slice(i * group * BAND, (i + 1) * group * BAND)
        acc_ref[cur, :] = num.reshape(group * BAND, A_DIM)
        m_ref[cur, :] = m.reshape(group * BAND, A_DIM)
        l_ref[cur, :] = l.reshape(group * BAND, A_DIM)
    for n, (num, m, l) in enumerate(parts[n_position:]):
        for c in range(4):
            rows = slice(c * quarter + n * BAND, c * quarter + (n + 1) * BAND)
            acc4_ref[rows, :] = num[c]
            m4_ref[rows, :] = m[c]
            l4_ref[rows, :] = l[c]

    for c in range(4):
        seg = slice(c * quarter, (c + 1) * quarter)
        rows = by4_rows(c)
        m_a, m_b = m_ref[rows, :], m4_ref[seg, :]
        m_new = jnp.maximum(m_a, m_b)
        a = jnp.exp2(m_a - m_new)
        b = jnp.exp2(m_b - m_new)
        num = acc_ref[rows, :] * a + acc4_ref[seg, :] * b
        den = l_ref[rows, :] * a + l4_ref[seg, :] * b
        acc_ref[rows, :] = num / den
    o_ref[0] = acc_ref[...].astype(o_ref.dtype)


def _dilated_attention(proj, bias, bias4, group):
    bsz, seq, _ = proj.shape
    head = lambda off: pl.BlockSpec((1, seq, A_DIM), lambda b, h, off=off: (b, 0, off + h))
    blocks = (3 * _nbytes((seq, A_DIM), _F32) + _nbytes((seq, A_DIM), _BF16)
              + _nbytes(bias.shape, _F32) + _nbytes(bias4.shape, _F32))
    scratch = 6 * _nbytes((seq, LANES), _F32) + 6 * _nbytes((seq, LANES), _BF16)
    n_blocks = seq // BAND
    temporaries = 2 * (2 * n_blocks + 10) * _nbytes((BAND, BAND), _F32)
    return pl.pallas_call(
        functools.partial(_dilated_kernel, group=group),
        grid=(bsz, A_HEADS),
        in_specs=[head(0), head(A_HEADS), head(2 * A_HEADS),
                  pl.BlockSpec(bias.shape, lambda b, h: (0, 0)),
                  pl.BlockSpec(bias4.shape, lambda b, h: (0, 0))],
        out_specs=pl.BlockSpec((1, seq, A_DIM), lambda b, h: (b, 0, h)),
        out_shape=jax.ShapeDtypeStruct((bsz, seq, A_WIDTH), _BF16),
        scratch_shapes=[pltpu.VMEM((seq, A_DIM), _BF16) for _ in range(6)]
        + [pltpu.VMEM((seq, LANES), _F32) for _ in range(6)],
        compiler_params=pltpu.CompilerParams(
            dimension_semantics=("arbitrary", "arbitrary"),
            vmem_limit_bytes=_vmem_limit(blocks, scratch + temporaries)),
        name="dilated",
    )(proj, proj, proj, bias, bias4)


_B_LEVELS = (32, 16, 8, 4, 2, 1)


def _hgrn_constants():
    c = B_CHUNK
    t = np.arange(c)[:, None]
    r = np.arange(c)[None, :]
    mats = [(r <= t), (r > t)]
    for h in _B_LEVELS:
        base = (t // (2 * h)) * (2 * h)
        mid = base + h - 1
        upper = (t & h) != 0
        mats.append(np.where(upper, (r > mid) & (r <= t), (r > t) & (r <= mid)))
    sums = np.concatenate(mats, axis=0).astype(np.float32)
    s = np.arange(c)[None, :]
    x = t ^ s
    level = np.where(s > t, -1, np.where(s == t, 0, 1 << (np.floor(np.log2(np.maximum(x, 1))).astype(np.int64))))
    return np.concatenate([sums, sums], axis=1), level.astype(np.int32)


def _hgrn_kernel(q_ref, f_ref, i_ref, g_ref, lb_ref, gain_ref, sums_ref, level_ref, o_ref, *, group):
    seq = q_ref.shape[1]
    c = B_CHUNK
    span = group * c
    lb = lb_ref[0]
    gain = gain_ref[...]
    sums2 = sums_ref[...]
    level = level_ref[...]
    chunks = range(group)

    def piece(x, j):
        return x[j * c:(j + 1) * c]

    def step(n, state_t):
        rows = pl.ds(pl.multiple_of(n * span, span), span)
        gate = lb + (1.0 - lb) * _sigmoid(f_ref[0, rows, :])
        log_gate = jnp.log2(gate)
        k = 1.0 - gate
        qraw = q_ref[0, rows, :]
        q = qraw * _sigmoid(qraw) * (B_DIM ** -0.5)
        v = i_ref[0, rows, :].astype(_BF16)

        g_hi = log_gate.astype(_BF16)
        g_lo = (log_gate - g_hi.astype(_F32)).astype(_BF16)
        hi_lo = [jnp.concatenate([piece(g_hi, j), piece(g_lo, j)], axis=0) for j in chunks]
        expo = []
        for j in range(0, group, 2):
            both = jnp.dot(sums2, jnp.concatenate([hi_lo[j], hi_lo[j + 1]], axis=1),
                           preferred_element_type=_F32)
            expo += [both[:, :B_DIM], both[:, B_DIM:]]
        decay = [jnp.exp2(e) for e in expo]

        qc = [piece(q, j) for j in chunks]
        kc = [piece(k, j) for j in chunks]
        vc = [piece(v, j) for j in chunks]
        q_bf = [qc[j].astype(_BF16) for j in chunks]
        k_bf = [kc[j].astype(_BF16) for j in chunks]
        level_decay = [[piece(decay[j], 2 + i).astype(_BF16) for i in range(len(_B_LEVELS))] for j in chunks]
        scaled_q = [[q_bf[j]] + [q_bf[j] * d for d in level_decay[j]] for j in chunks]
        scaled_k = [[k_bf[j]] + [k_bf[j] * d for d in level_decay[j]] for j in chunks]
        q_in = [q_bf[j] * piece(decay[j], 0).astype(_BF16) for j in chunks]
        k_out = [k_bf[j] * piece(decay[j], 1).astype(_BF16) for j in chunks]
        parts = [[lax.dot_general(a, b, _NT, preferred_element_type=_F32)
                  for a, b in zip(scaled_q[j], scaled_k[j])] for j in chunks]
        attn = []
        for j in chunks:
            total = jnp.where(level == 0, parts[j][0], 0.0)
            for i, h in enumerate(_B_LEVELS):
                total = total + jnp.where(level == h, parts[j][1 + i], 0.0)
            attn.append(total.astype(_BF16))
        intra = [jnp.dot(attn[j], vc[j], preferred_element_type=_F32) for j in chunks]
        update = [lax.dot_general(vc[j], k_out[j], _TN, preferred_element_type=_F32) for j in chunks]

        outs = []
        for j in chunks:
            outs.append(intra[j] + lax.dot_general(q_in[j], state_t.astype(_BF16), _NT,
                                                   preferred_element_type=_F32))
            state_t = state_t * decay[j][c - 1:c] + update[j]

        y = _rmsnorm_rows(jnp.concatenate(outs, axis=0), gain)
        graw = g_ref[0, rows, :]
        y = y * (graw * _sigmoid(graw))
        o_ref[0, rows, :] = y.astype(o_ref.dtype)
        return state_t

    lax.fori_loop(0, seq // span, step, jnp.zeros((B_DIM, B_DIM), _F32))


def _hgrn2(proj, lower_bound, gain, sums, level, group):
    bsz, seq, _ = proj.shape
    assert group % 2 == 0 and seq % (group * B_CHUNK) == 0
    head = lambda off: pl.BlockSpec((1, seq, B_DIM), lambda b, h, off=off: (b, 0, off + h))
    blocks = 4 * _nbytes((seq, B_DIM), _F32) + _nbytes((seq, B_DIM), _BF16) \
        + _nbytes(sums.shape, _BF16) + _nbytes(level.shape, jnp.int32) + 2 * _nbytes((8, LANES), _F32)
    temporaries = 3 * group * _nbytes((sums.shape[0], B_DIM), _F32)
    return pl.pallas_call(
        functools.partial(_hgrn_kernel, group=group),
        grid=(bsz, B_HEADS),
        in_specs=[head(0), head(B_HEADS), head(2 * B_HEADS), head(3 * B_HEADS),
                  pl.BlockSpec((1, 1, B_DIM), lambda b, h: (h, 0, 0)),
                  pl.BlockSpec((1, B_DIM), lambda b, h: (0, 0)),
                  pl.BlockSpec(sums.shape, lambda b, h: (0, 0)),
                  pl.BlockSpec(level.shape, lambda b, h: (0, 0))],
        out_specs=pl.BlockSpec((1, seq, B_DIM), lambda b, h: (b, 0, h)),
        out_shape=jax.ShapeDtypeStruct((bsz, seq, B_WIDTH), _BF16),
        compiler_params=pltpu.CompilerParams(
            dimension_semantics=("arbitrary", "arbitrary"),
            vmem_limit_bytes=_vmem_limit(blocks, temporaries)),
        name="hgrn2",
    )(proj, proj, proj, proj, lower_bound.reshape(B_HEADS, 1, B_DIM), gain.reshape(1, B_DIM), sums, level)


def _swa_kernel(q_ref, k_ref, v_ref, sink_ref, bias_ref, o_ref, klo_ref, khi_ref, vlo_ref, vhi_ref, *, joint):
    seq = q_ref.shape[1]
    kv_head = pl.program_id(1)
    lane_half = lax.broadcasted_iota(jnp.int32, (seq, LANES), 1) // C_DIM
    odd_head = (kv_head % 2) == 1

    def place(x):
        own = jnp.where(lane_half == kv_head % 2, x, 0.0)
        other = pltpu.roll(own, C_DIM, 1)
        lo = jnp.where(odd_head, other, own)
        hi = jnp.where(odd_head, own, other)
        return lo.astype(_BF16), hi.astype(_BF16)

    klo_ref[...], khi_ref[...] = place(k_ref[0])
    vlo_ref[...], vhi_ref[...] = place(v_ref[0])
    pairs = C_GROUP // 2

    heads = [(j, side) for side in range(2) for j in range(pairs)]
    sinks = [sink_ref[kv_head * C_GROUP + 2 * j + side] * LOG2E for j, side in heads]
    low_q = lax.broadcasted_iota(jnp.int32, (BAND, LANES), 1) < C_DIM
    k_refs, v_refs = (klo_ref, khi_ref), (vlo_ref, vhi_ref)

    def attend(block_ids, with_prev):
        curs = [slice(n * BAND, (n + 1) * BAND) for n in block_ids]
        if with_prev:
            keys = [slice((n - 1) * BAND, (n + 1) * BAND) for n in block_ids]
            bias = bias_ref[...]
        else:
            keys = curs
            bias = bias_ref[:, BAND:]
        count = range(len(block_ids))
        q4 = [jnp.concatenate([q_ref[0, curs[b], j * LANES:(j + 1) * LANES] for j in range(pairs)],
                              axis=0).astype(_BF16) for b in count]
        s_side = [[lax.dot_general(q4[b], k_refs[side][keys[b], :], _NT, preferred_element_type=_F32)
                   for side in range(2)] for b in count]
        s = [[s_side[b][side][j * BAND:(j + 1) * BAND] + bias for j, side in heads] for b in count]
        m = [[jnp.maximum(jnp.max(s_h, axis=-1, keepdims=True), sink) for s_h, sink in zip(s[b], sinks)]
             for b in count]
        p = [[jnp.exp2(s_h - m_h) for s_h, m_h in zip(s[b], m[b])] for b in count]
        sums = [[jnp.sum(p_h, axis=-1, keepdims=True) for p_h in p[b]] for b in count]
        num = [[jnp.dot(jnp.concatenate(p[b][side * pairs:(side + 1) * pairs], axis=0).astype(_BF16),
                        v_refs[side][keys[b], :], preferred_element_type=_F32) for side in range(2)]
               for b in count]
        both = lambda lo, hi: jnp.where(low_q, lo, hi)
        den = [[both(sums[b][j], sums[b][pairs + j])
                + jnp.exp2(both(sinks[j], sinks[pairs + j]) - both(m[b][j], m[b][pairs + j]))
                for j in range(pairs)] for b in count]
        for b in count:
            for j in range(pairs):
                rows = slice(j * BAND, (j + 1) * BAND)
                out = both(num[b][0][rows], num[b][1][rows]) / den[b][j]
                o_ref[0, curs[b], j * LANES:(j + 1) * LANES] = out.astype(o_ref.dtype)

    attend([0], False)
    later = list(range(1, seq // BAND))
    for first in range(0, len(later), joint):
        attend(later[first:first + joint], True)


def _swa(proj, sinks, bias, joint):
    bsz, seq, _ = proj.shape
    qw = C_GROUP * C_DIM
    k_base = C_Q_HEADS * C_DIM // LANES
    v_base = k_base + C_KV_HEADS * C_DIM // LANES
    blocks = _nbytes((seq, qw), _F32) + 2 * _nbytes((seq, LANES), _F32) + _nbytes((seq, qw), _BF16) \
        + _nbytes(bias.shape, _F32)
    scratch = 4 * _nbytes((seq, LANES), _BF16)
    temporaries = 2 * joint * C_GROUP * _nbytes((BAND, 2 * BAND), _F32)
    return pl.pallas_call(
        functools.partial(_swa_kernel, joint=joint),
        grid=(bsz, C_KV_HEADS),
        in_specs=[pl.BlockSpec((1, seq, qw), lambda b, h: (b, 0, h)),
                  pl.BlockSpec((1, seq, LANES), lambda b, h: (b, 0, k_base + h // 2)),
                  pl.BlockSpec((1, seq, LANES), lambda b, h: (b, 0, v_base + h // 2)),
                  pl.BlockSpec(memory_space=pltpu.SMEM),
                  pl.BlockSpec(bias.shape, lambda b, h: (0, 0))],
        out_specs=pl.BlockSpec((1, seq, qw), lambda b, h: (b, 0, h)),
        out_shape=jax.ShapeDtypeStruct((bsz, seq, C_Q_HEADS * C_DIM), _BF16),
        scratch_shapes=[pltpu.VMEM((seq, LANES), _BF16) for _ in range(4)],
        compiler_params=pltpu.CompilerParams(
            dimension_semantics=("arbitrary", "arbitrary"),
            vmem_limit_bytes=_vmem_limit(blocks, scratch + temporaries)),
        name="swa",
    )(proj, proj, proj, sinks.astype(_F32), bias)


def kernel(x, norm_mix_g, norm_mlp_g, final_norm_g, even_w_in, even_w_out, hgrn_lb_raw, hgrn_norm_g,
           odd_w_qkv, odd_b_qkv, odd_sinks, odd_w_o, odd_b_o, mlp_w1, mlp_w2):
    bsz, seq, d = x.shape
    depth = norm_mix_g.shape[0]
    t = _tiles()
    m = bsz * seq

    tables_a = _rope_tables(seq, A_DIM)
    tables_c = _rope_tables(seq, C_DIM)
    bias_a = jnp.asarray(_band_bias_window(BAND))
    bias_a4 = jnp.asarray(_by4_bias(seq // 4))
    bias_c = jnp.asarray(_band_bias_window(C_WINDOW - 1))
    sums_np, level_np = _hgrn_constants()
    sums, level = jnp.asarray(sums_np, _BF16), jnp.asarray(level_np)

    lb_soft = jax.nn.softmax(hgrn_lb_raw.astype(_F32), axis=0)
    lower_bounds = jnp.cumsum(lb_soft, axis=0) - lb_soft[0:1]

    w_in_a = even_w_in[:, :, :A_IN].astype(_BF16)
    w_in_b = even_w_in[:, :, A_IN:].astype(_BF16)
    w_out = even_w_out.astype(_BF16)
    w_qkv = odd_w_qkv.astype(_BF16)
    w_o = odd_w_o.astype(_BF16)
    w1 = mlp_w1.astype(_BF16)
    w2 = mlp_w2.astype(_BF16)

    group_a = A_WIDTH // LANES
    rope_a = [A_DIM ** -0.5 * LOG2E] * group_a + [1.0] * group_a + [None] * group_a
    q_groups = C_Q_HEADS * C_DIM // LANES
    kv_groups = C_KV_HEADS * C_DIM // LANES
    rope_c = [C_DIM ** -0.5 * LOG2E] * q_groups + [1.0] * kv_groups + [None] * kv_groups
    zeros_d = jnp.zeros((d,), _F32)

    x2 = x.reshape(m, d)
    for layer in range(depth):
        if layer % 2 == 0:
            e = layer // 2
            proj_a, h_mix = _norm_proj(x2, norm_mix_g[layer], w_in_a, e, A_IN, jnp.zeros((A_IN,), _F32),
                                       tables_a, rope_a, seq, t["proj_tm"], emit_h=True)
            proj_a = proj_a.reshape(bsz, seq, A_IN)
            proj_b = _proj(h_mix, w_in_b, e, t["proj_tm"]).reshape(bsz, seq, B_IN)
            oa = _dilated_attention(proj_a, bias_a, bias_a4, t["dilated_group"]).reshape(m, A_WIDTH)
            ob = _hgrn2(proj_b, lower_bounds[e], hgrn_norm_g[e], sums, level, t["hgrn_group"]).reshape(m, B_WIDTH)
            x2, h2 = _res_matmul([oa, ob], w_out, e, zeros_d, x2, norm_mlp_g[layer], t["out_tm"])
        else:
            o = layer // 2
            proj = _norm_proj(x2, norm_mix_g[layer], w_qkv, o, C_QKV, odd_b_qkv[o],
                              tables_c, rope_c, seq, t["proj_tm"], emit_h=False).reshape(bsz, seq, C_QKV)
            attn = _swa(proj, odd_sinks[o], bias_c, t["swa_joint"]).reshape(m, C_Q_HEADS * C_DIM)
            x2, h2 = _res_matmul([attn], w_o, o, odd_b_o[o], x2, norm_mlp_g[layer], t["out_tm"])
        last = layer == depth - 1
        x2 = _mlp(x2, h2, w1, w2, layer, final_norm_g if last else None, t["mlp_tm"], t["mlp_tf"])
    return x2.reshape(bsz, seq, d)
```
